```python
import jax, jax.numpy as jnp
from jax import lax
import numpy as np

D_MODEL = 4096
BATCH = 2
SEQ = 4096
DEPTH = 2

N_MEM = 256
EPS = 1e-6
ROPE_BASE = 10000.0
ATTN_BLOCK = 128
A_HEADS = 16
A_Q_RANK = 1024
A_KV_RANK = 512
A_NOPE = 128
A_ROPE = 64
A_V = 128
B_HEADS = 8
B_QK = 256
B_V = 256
RET_CHUNK = 128
C_GROUPS = 8
C_WIDTH = 4096
C_CHUNK = 128
X_HEADS = 4
X_HEAD_DIM = 128
D_FF = 14336
N_EXPERTS = 8
TOP_K = 2
D_EXPERT = 3584

A_IN = A_Q_RANK + A_KV_RANK + A_ROPE
B_QK_W = B_HEADS * B_QK
B_V_W = B_HEADS * B_V
AB_IN = A_IN + 2 * B_QK_W + 2 * B_V_W
AB_SPLITS = (A_Q_RANK, A_Q_RANK + A_KV_RANK, A_IN, A_IN + B_QK_W,
             A_IN + 2 * B_QK_W, A_IN + 2 * B_QK_W + B_V_W)
AB_OUT = A_HEADS * A_V + B_V_W

kernel_name = "hybrid_mla_retention_gmlp_moe"


def rms_norm(x, g):
    xf = x.astype(jnp.float32)
    y = xf * lax.rsqrt(jnp.mean(xf * xf, axis=-1, keepdims=True) + EPS)
    return (y * g.astype(jnp.float32)).astype(x.dtype)


def layer_norm(x, g):
    xf = x.astype(jnp.float32)
    mu = jnp.mean(xf, axis=-1, keepdims=True)
    var = jnp.mean(jnp.square(xf - mu), axis=-1, keepdims=True)
    return ((xf - mu) * lax.rsqrt(var + EPS) * g.astype(jnp.float32)).astype(x.dtype)


def rope(x, pos):
    d = x.shape[-1]
    inv = ROPE_BASE ** (-jnp.arange(0, d, 2, dtype=jnp.float32) / d)
    ang = pos.astype(jnp.float32)[..., None] * inv
    cos = jnp.cos(ang)[:, :, None, :]
    sin = jnp.sin(ang)[:, :, None, :]
    xf = x.astype(jnp.float32)
    x1, x2 = xf[..., : d // 2], xf[..., d // 2:]
    return jnp.concatenate([x1 * cos - x2 * sin, x1 * sin + x2 * cos], -1).astype(x.dtype)


def causal_block_attention(q, k, v):
    B, S, H, dk = q.shape
    dv = v.shape[-1]
    nb = S // ATTN_BLOCK
    scale = dk ** -0.5
    qb = q.reshape(B, nb, ATTN_BLOCK, H, dk).transpose(1, 0, 2, 3, 4)
    kpos = jnp.arange(S)

    def one_block(args):
        qi, i = args
        s = jnp.einsum('bqhd,bkhd->bhqk', qi, k).astype(jnp.float32) * scale
        qpos = i * ATTN_BLOCK + jnp.arange(ATTN_BLOCK)
        s = jnp.where(kpos[None, :] <= qpos[:, None], s, -jnp.inf)
        p = jax.nn.softmax(s, axis=-1).astype(v.dtype)
        return jnp.einsum('bhqk,bkhd->bqhd', p, v)

    o = lax.map(one_block, (qb, jnp.arange(nb)))
    return o.transpose(1, 0, 2, 3, 4).reshape(B, S, H, dv)


def retention(q, k, v, pos):
    B, S, H, dk = q.shape
    dv = v.shape[-1]
    C = RET_CHUNK
    n = S // C
    q = rope(q, pos).astype(jnp.float32)
    k = rope(k, pos).astype(jnp.float32) * (dk ** -0.5)
    vf = v.astype(jnp.float32)
    log_g = jnp.log1p(-jnp.exp2(-5.0 - jnp.arange(H, dtype=jnp.float32)))
    idx = jnp.arange(C, dtype=jnp.float32)
    diff = idx[:, None] - idx[None, :]
    intra = jnp.where(diff >= 0, jnp.exp(log_g[:, None, None] * jnp.maximum(diff, 0.0)), 0.0)
    q_decay = jnp.exp(log_g[:, None] * (idx + 1.0))[None, :, :, None]
    k_decay = jnp.exp(log_g[:, None] * (C - 1.0 - idx))[None, :, :, None]
    chunk_decay = jnp.exp(log_g * C)[None, :, None, None]

    def to_chunks(t):
        return t.reshape(B, n, C, H, t.shape[-1]).transpose(1, 0, 3, 2, 4)

    def step(state, inp):
        qi, ki, vi = inp
        s = jnp.einsum('bhcd,bhed->bhce', qi, ki) * intra
        inner = jnp.einsum('bhce,bhev->bhcv', s, vi)
        cross = jnp.einsum('bhcd,bhdv->bhcv', qi, state) * q_decay
        new_state = state * chunk_decay + jnp.einsum('bhcd,bhcv->bhdv', ki * k_decay, vi)
        return new_state, inner + cross

    state0 = jnp.zeros((B, H, dk, dv), jnp.float32)
    _, out = lax.scan(step, state0, (to_chunks(q), to_chunks(k), to_chunks(vf)))
    return out.transpose(1, 0, 3, 2, 4).reshape(B, S, H, dv).astype(v.dtype)


def head_group_norm(y, g):
    B, S, H, dv = y.shape
    yf = y.astype(jnp.float32)
    mu = jnp.mean(yf, axis=-1, keepdims=True)
    var = jnp.mean(jnp.square(yf - mu), axis=-1, keepdims=True)
    yn = ((yf - mu) * lax.rsqrt(var + EPS)).reshape(B, S, H * dv)
    return (yn * g.astype(jnp.float32)).astype(y.dtype)


def mixer_ab(h, pos, w_in, q_norm, w_uq, kv_norm, w_ukv, ret_norm, w_out):
    B, S, _ = h.shape
    z = h @ w_in
    cq, ckv, kr, rq, rk, rv, rg = jnp.split(z, AB_SPLITS, axis=-1)
    q = (rms_norm(cq, q_norm) @ w_uq).reshape(B, S, A_HEADS, A_NOPE + A_ROPE)
    q = jnp.concatenate([q[..., :A_NOPE], rope(q[..., A_NOPE:], pos)], axis=-1)
    kv = (rms_norm(ckv, kv_norm) @ w_ukv).reshape(B, S, A_HEADS, A_NOPE + A_V)
    k_pe = jnp.broadcast_to(rope(kr[:, :, None, :], pos), (B, S, A_HEADS, A_ROPE))
    k = jnp.concatenate([kv[..., :A_NOPE], k_pe], axis=-1)
    ya = causal_block_attention(q, k, kv[..., A_NOPE:]).reshape(B, S, A_HEADS * A_V)
    yb = retention(rq.reshape(B, S, B_HEADS, B_QK), rk.reshape(B, S, B_HEADS, B_QK),
                   rv.reshape(B, S, B_HEADS, B_V), pos)
    yb = head_group_norm(yb, ret_norm) * jax.nn.silu(rg)
    return jnp.concatenate([ya, yb], axis=-1) @ w_out


def mixer_c(h, w_in, v_norm, w_s, b_s, w_out):
    B, S, _ = h.shape
    n = S // C_CHUNK
    gw = C_WIDTH // C_GROUPS
    z = jax.nn.gelu(h @ w_in)
    u, v = z[..., :C_WIDTH], z[..., C_WIDTH:]
    v = layer_norm(v, v_norm).reshape(B, n, C_CHUNK, C_GROUPS, gw)
    ws = jnp.where(jnp.tril(jnp.ones((C_CHUNK, C_CHUNK), bool))[None], w_s, 0.0).astype(v.dtype)
    mixed = jnp.einsum('gts,bnsgc->bntgc', ws, v) + b_s.T[None, None, :, :, None].astype(v.dtype)
    return (u * mixed.reshape(B, S, C_WIDTH)) @ w_out


def cross_attn(h, mem_n, w_q, w_kv, w_o):
    B, S, _ = h.shape
    M = mem_n.shape[1]
    q = (h @ w_q).reshape(B, S, X_HEADS, X_HEAD_DIM)
    kv = (mem_n @ w_kv).reshape(B, M, 2, X_HEADS, X_HEAD_DIM)
    s = jnp.einsum('bqhd,bkhd->bhqk', q, kv[:, :, 0]).astype(jnp.float32) * (X_HEAD_DIM ** -0.5)
    p = jax.nn.softmax(s, axis=-1).astype(h.dtype)
    o = jnp.einsum('bhqk,bkhd->bqhd', p, kv[:, :, 1])
    return o.reshape(B, S, X_HEADS * X_HEAD_DIM) @ w_o


def swiglu(h, w_gu, w_down):
    gu = h @ w_gu
    f = w_down.shape[0]
    return (jax.nn.silu(gu[..., :f]) * gu[..., f:]) @ w_down


def moe(h, w_router, w_gu, w_down):
    B, S, D = h.shape
    t = h.reshape(B * S, D)
    logits = (t @ w_router).astype(jnp.float32)
    top_val, top_idx = lax.top_k(logits, TOP_K)
    gates = jax.nn.softmax(top_val, axis=-1)
    combine = jnp.einsum('tk,tke->te', gates, jax.nn.one_hot(top_idx, N_EXPERTS, dtype=jnp.float32))
    out = jnp.zeros_like(t)
    for e in range(N_EXPERTS):
        out = out + combine[:, e:e + 1].astype(t.dtype) * swiglu(t, w_gu[e], w_down[e])
    return out.reshape(B, S, D)


def setup_inputs(seed: int = 0) -> dict:
    key = jax.random.key(seed)
    ks = iter(jax.random.split(key, 40))
    NE = (DEPTH + 1) // 2
    NO = DEPTH // 2
    f32 = jnp.float32

    def w(shape, fan_in):
        return jax.random.normal(next(ks), shape, f32) * (fan_in ** -0.5)

    def gain(shape):
        return 1.0 + 0.02 * jax.random.normal(next(ks), shape, f32)

    x = jax.random.normal(next(ks), (BATCH, SEQ, D_MODEL), f32)
    mem = jax.random.normal(next(ks), (BATCH, N_MEM, D_MODEL), f32)
    offset = jax.random.randint(next(ks), (BATCH,), 0, 1024, dtype=jnp.int32)
    positions = offset[:, None] + jnp.arange(SEQ, dtype=jnp.int32)[None, :]
    return {
        "x": x,
        "mem": mem,
        "positions": positions,
        "norm_mix_pre": gain((DEPTH, D_MODEL)),
        "norm_mix_post": gain((DEPTH, D_MODEL)),
        "norm_x_pre": gain((DEPTH, D_MODEL)),
        "norm_x_post": gain((DEPTH, D_MODEL)),
        "norm_ffn_pre": gain((DEPTH, D_MODEL)),
        "norm_ffn_post": gain((DEPTH, D_MODEL)),
        "norm_mem": gain((DEPTH, D_MODEL)),
        "x_w_q": w((DEPTH, D_MODEL, X_HEADS * X_HEAD_DIM), D_MODEL),
        "x_w_kv": w((DEPTH, D_MODEL, 2 * X_HEADS * X_HEAD_DIM), D_MODEL),
        "x_w_o": w((DEPTH, X_HEADS * X_HEAD_DIM, D_MODEL), X_HEADS * X_HEAD_DIM),
        "ab_w_in": w((NE, D_MODEL, AB_IN), D_MODEL),
        "ab_q_norm": gain((NE, A_Q_RANK)),
        "ab_w_uq": w((NE, A_Q_RANK, A_HEADS * (A_NOPE + A_ROPE)), A_Q_RANK),
        "ab_kv_norm": gain((NE, A_KV_RANK)),
        "ab_w_ukv": w((NE, A_KV_RANK, A_HEADS * (A_NOPE + A_V)), A_KV_RANK),
        "ab_ret_norm": gain((NE, B_V_W)),
        "ab_w_out": w((NE, AB_OUT, D_MODEL), AB_OUT),
        "ffn_w_gu": w((NE, D_MODEL, 2 * D_FF), D_MODEL),
        "ffn_w_down": w((NE, D_FF, D_MODEL), D_FF),
        "c_w_in": w((NO, D_MODEL, 2 * C_WIDTH), D_MODEL),
        "c_v_norm": gain((NO, C_WIDTH)),
        "c_w_s": w((NO, C_GROUPS, C_CHUNK, C_CHUNK), C_CHUNK),
        "c_b_s": gain((NO, C_GROUPS, C_CHUNK)),
        "c_w_out": w((NO, C_WIDTH, D_MODEL), C_WIDTH),
        "moe_router": w((NO, D_MODEL, N_EXPERTS), D_MODEL),
        "moe_w_gu": w((NO, N_EXPERTS, D_MODEL, 2 * D_EXPERT), D_MODEL),
        "moe_w_down": w((NO, N_EXPERTS, D_EXPERT, D_MODEL), D_EXPERT),
    }


def reference(x, mem, positions, norm_mix_pre, norm_mix_post, norm_x_pre, norm_x_post,
              norm_ffn_pre, norm_ffn_post, norm_mem, x_w_q, x_w_kv, x_w_o,
              ab_w_in, ab_q_norm, ab_w_uq, ab_kv_norm, ab_w_ukv, ab_ret_norm, ab_w_out,
              ffn_w_gu, ffn_w_down, c_w_in, c_v_norm, c_w_s, c_b_s, c_w_out,
              moe_router, moe_w_gu, moe_w_down):
    for layer in range(DEPTH):
        i = layer // 2
        h = rms_norm(x, norm_mix_pre[layer])
        if layer % 2 == 0:
            h = mixer_ab(h, positions, ab_w_in[i], ab_q_norm[i], ab_w_uq[i], ab_kv_norm[i],
                         ab_w_ukv[i], ab_ret_norm[i], ab_w_out[i])
        else:
            h = mixer_c(h, c_w_in[i], c_v_norm[i], c_w_s[i], c_b_s[i], c_w_out[i])
        x = x + rms_norm(h, norm_mix_post[layer])
        h = cross_attn(rms_norm(x, norm_x_pre[layer]), rms_norm(mem, norm_mem[layer]),
                       x_w_q[layer], x_w_kv[layer], x_w_o[layer])
        x = x + rms_norm(h, norm_x_post[layer])
        h = rms_norm(x, norm_ffn_pre[layer])
        if layer % 2 == 0:
            h = swiglu(h, ffn_w_gu[i], ffn_w_down[i])
        else:
            h = moe(h, moe_router[i], moe_w_gu[i], moe_w_down[i])
        x = x + rms_norm(h, norm_ffn_post[layer])
    return x
```

```python
import functools

import jax
import jax.numpy as jnp
from jax import lax
from jax.experimental import pallas as pl
from jax.experimental.pallas import tpu as pltpu

F32 = jnp.float32
BF16 = jnp.bfloat16

D_MODEL = 4096
N_MEM = 256
EPS = 1e-6
ROPE_BASE = 10000.0
A_HEADS = 16
A_Q_RANK = 1024
A_KV_RANK = 512
A_NOPE = 128
A_ROPE = 64
A_V = 128
B_HEADS = 8
B_QK = 256
B_V = 256
RET_CHUNK = 128
C_GROUPS = 8
C_WIDTH = 4096
C_CHUNK = 128
X_HEADS = 4
X_HEAD_DIM = 128
D_FF = 14336
N_EXPERTS = 8
D_EXPERT = 3584

LANES = 128
VMEM_LIMIT = 56 * 1024 * 1024
NEG_BIG = -1e30

Z_CQ = 0
Z_CKV = A_Q_RANK
Z_RQ = A_Q_RANK + A_KV_RANK
Z_RK = Z_RQ + B_HEADS * B_QK
Z_RV = Z_RK + B_HEADS * B_QK
Z_RG = Z_RV + B_HEADS * B_V
Z_KR = Z_RG + B_HEADS * B_V
Z_W = Z_KR + 2 * A_ROPE

MOE_TM = 512


def _params(sem, vmem=VMEM_LIMIT):
    return pltpu.CompilerParams(dimension_semantics=sem, vmem_limit_bytes=vmem)


def _rms(xf, g):
    ms = jnp.mean(xf * xf, axis=-1, keepdims=True)
    return xf * lax.rsqrt(ms + EPS) * g


def _bf(w):
    return w if w.dtype == BF16 else w.astype(BF16)


def _dot(a, b):
    return jnp.dot(a, b, preferred_element_type=F32)


def _dot_nt(a, b):
    return lax.dot_general(a, b, (((1,), (1,)), ((), ())), preferred_element_type=F32)


def _dot_tn(a, b):
    return lax.dot_general(a, b, (((0,), (0,)), ((), ())), preferred_element_type=F32)


def _norm_kernel(x_ref, g_ref, o_ref):
    o_ref[...] = _rms(x_ref[...].astype(F32), g_ref[...]).astype(o_ref.dtype)


def rms_norm_bf16(x, g, tm=512):
    m, d = x.shape
    return pl.pallas_call(
        _norm_kernel,
        grid=(m // tm,),
        in_specs=[pl.BlockSpec((tm, d), lambda i: (i, 0)),
                  pl.BlockSpec((1, d), lambda i: (0, 0))],
        out_specs=pl.BlockSpec((tm, d), lambda i: (i, 0)),
        out_shape=jax.ShapeDtypeStruct((m, d), BF16),
        compiler_params=_params(("arbitrary",)),
        name="rms_norm",
    )(x, g.reshape(1, d))


def _resid_kernel(*refs, n_h, with_next):
    h_refs = refs[:n_h]
    x_ref, g_ref = refs[n_h], refs[n_h + 1]
    h = h_refs[0][...]
    for r in h_refs[1:]:
        h = h + r[...]
    xn = x_ref[...] + _rms(h, g_ref[...])
    if with_next:
        gn_ref, xo_ref, hn_ref = refs[n_h + 2:]
        xo_ref[...] = xn
        hn_ref[...] = _rms(xn, gn_ref[...]).astype(hn_ref.dtype)
    else:
        xo_ref = refs[n_h + 2]
        xo_ref[...] = xn


def resid_norm(hs, x, g_post, g_next=None, tm=256):
    m, d = x.shape
    row = pl.BlockSpec((tm, d), lambda i: (i, 0))
    vec = pl.BlockSpec((1, d), lambda i: (0, 0))
    with_next = g_next is not None
    in_specs = [row] * len(hs) + [row, vec] + ([vec] if with_next else [])
    args = list(hs) + [x, g_post.reshape(1, d)] + ([g_next.reshape(1, d)] if with_next else [])
    out_shape = [jax.ShapeDtypeStruct((m, d), F32)]
    out_specs = [row]
    if with_next:
        out_shape.append(jax.ShapeDtypeStruct((m, d), BF16))
        out_specs.append(row)
    res = pl.pallas_call(
        functools.partial(_resid_kernel, n_h=len(hs), with_next=with_next),
        grid=(m // tm,),
        in_specs=in_specs,
        out_specs=out_specs,
        out_shape=out_shape,
        compiler_params=_params(("arbitrary",)),
        name="resid_norm",
    )(*args)
    return res if with_next else res[0]


def _mm_kernel(x_ref, w_ref, o_ref, *, act):
    acc = _dot(x_ref[...], _bf(w_ref[...]))
    if act == "gelu":
        acc = jax.nn.gelu(acc)
    o_ref[...] = acc.astype(o_ref.dtype)


def matmul(x, w, out_dtype, tm, tn, act=None, name="matmul"):
    m, k = x.shape
    n = w.shape[1]
    return pl.pallas_call(
        functools.partial(_mm_kernel, act=act),
        grid=(n // tn, m // tm),
        in_specs=[pl.BlockSpec((tm, k), lambda j, i: (i, 0)),
                  pl.BlockSpec((k, tn), lambda j, i: (0, j))],
        out_specs=pl.BlockSpec((tm, tn), lambda j, i: (i, j)),
        out_shape=jax.ShapeDtypeStruct((m, n), out_dtype),
        compiler_params=_params(("arbitrary", "arbitrary")),
        name=name,
    )(x, w)


def _mm2_kernel(xa_ref, xb_ref, w_ref, o_ref):
    ka = xa_ref.shape[1]
    acc = _dot(xa_ref[...], _bf(w_ref[:ka, :]))
    acc = acc + _dot(xb_ref[...], _bf(w_ref[ka:, :]))
    o_ref[...] = acc.astype(o_ref.dtype)


def matmul_cat(xa, xb, w, out_dtype, tm, tn, name="matmul_cat"):
    m, ka = xa.shape
    kb = xb.shape[1]
    n = w.shape[1]
    return pl.pallas_call(
        _mm2_kernel,
        grid=(n // tn, m // tm),
        in_specs=[pl.BlockSpec((tm, ka), lambda j, i: (i, 0)),
                  pl.BlockSpec((tm, kb), lambda j, i: (i, 0)),
                  pl.BlockSpec((ka + kb, tn), lambda j, i: (0, j))],
        out_specs=pl.BlockSpec((tm, tn), lambda j, i: (i, j)),
        out_shape=jax.ShapeDtypeStruct((m, n), out_dtype),
        compiler_params=_params(("arbitrary", "arbitrary")),
        name=name,
    )(xa, xb, w)


def _swiglu_kernel(x_ref, wg_ref, wu_ref, o_ref):
    x = x_ref[...]
    g = _dot(x, _bf(wg_ref[...]))
    u = _dot(x, _bf(wu_ref[...]))
    o_ref[...] = (jax.nn.silu(g) * u).astype(o_ref.dtype)


def matmul_swiglu(x, w_gu, tm, tn, name="ffn_gate_up"):
    m, k = x.shape
    f = w_gu.shape[1] // 2
    nf = f // tn
    return pl.pallas_call(
        _swiglu_kernel,
        grid=(nf, m // tm),
        in_specs=[pl.BlockSpec((tm, k), lambda j, i: (i, 0)),
                  pl.BlockSpec((k, tn), lambda j, i: (0, j)),
                  pl.BlockSpec((k, tn), lambda j, i: (0, j + nf))],
        out_specs=pl.BlockSpec((tm, tn), lambda j, i: (i, j)),
        out_shape=jax.ShapeDtypeStruct((m, f), BF16),
        compiler_params=_params(("arbitrary", "arbitrary")),
        name=name,
    )(x, w_gu, w_gu)


def _mm_acc_kernel(x_ref, w_ref, o_ref):
    @pl.when(pl.program_id(2) == 0)
    def _():
        o_ref[...] = jnp.zeros_like(o_ref)

    o_ref[...] += _dot(x_ref[...], _bf(w_ref[...]))


def matmul_ksplit(x, w, tm, tn, tk, name="ffn_down"):
    m, k = x.shape
    n = w.shape[1]
    return pl.pallas_call(
        _mm_acc_kernel,
        grid=(m // tm, n // tn, k // tk),
        in_specs=[pl.BlockSpec((tm, tk), lambda i, j, kk: (i, kk)),
                  pl.BlockSpec((tk, tn), lambda i, j, kk: (kk, j))],
        out_specs=pl.BlockSpec((tm, tn), lambda i, j, kk: (i, j)),
        out_shape=jax.ShapeDtypeStruct((m, n), F32),
        compiler_params=_params(("arbitrary", "arbitrary", "arbitrary")),
        name=name,
    )(x, w)


def _rope64(pe_pair, tab):
    p = pe_pair * tab
    return p + pltpu.roll(p, A_ROPE, axis=1)


def _a_norm_kernel(cq_ref, ckv_ref, kr_ref, tab_ref, qn_ref, kvn_ref, cqn_ref, ckvn_ref, kpe_ref):
    cqn_ref[...] = _rms(cq_ref[...].astype(F32), qn_ref[...]).astype(BF16)
    ckvn_ref[...] = _rms(ckv_ref[...].astype(F32), kvn_ref[...]).astype(BF16)
    r = _rope64(kr_ref[...].astype(F32), tab_ref[...])
    lane = lax.broadcasted_iota(jnp.int32, r.shape, 1)
    kpe_ref[...] = jnp.where(lane < A_ROPE, r, 0.0).astype(BF16)


def a_norm(z, tab64, q_norm, kv_norm, tm=512):
    t = z.shape[0]
    return pl.pallas_call(
        _a_norm_kernel,
        grid=(t // tm,),
        in_specs=[pl.BlockSpec((tm, A_Q_RANK), lambda i: (i, Z_CQ // A_Q_RANK)),
                  pl.BlockSpec((tm, A_KV_RANK), lambda i: (i, Z_CKV // A_KV_RANK)),
                  pl.BlockSpec((tm, LANES), lambda i: (i, Z_KR // LANES)),
                  pl.BlockSpec((tm, LANES), lambda i: (i, 0)),
                  pl.BlockSpec((1, A_Q_RANK), lambda i: (0, 0)),
                  pl.BlockSpec((1, A_KV_RANK), lambda i: (0, 0))],
        out_specs=[pl.BlockSpec((tm, A_Q_RANK), lambda i: (i, 0)),
                   pl.BlockSpec((tm, A_KV_RANK), lambda i: (i, 0)),
                   pl.BlockSpec((tm, LANES), lambda i: (i, 0))],
        out_shape=[jax.ShapeDtypeStruct((t, A_Q_RANK), BF16),
                   jax.ShapeDtypeStruct((t, A_KV_RANK), BF16),
                   jax.ShapeDtypeStruct((t, LANES), BF16)],
        compiler_params=_params(("arbitrary",)),
        name="a_norm",
    )(z, z, z, tab64, q_norm.reshape(1, -1), kv_norm.reshape(1, -1))


def _q_proj_kernel(x_ref, w_ref, tab_ref, o_ref, *, scale):
    acc = _dot(x_ref[...], w_ref[0])
    nope = acc[:, :A_NOPE] * scale
    pe = _rope64(acc[:, A_NOPE:], tab_ref[...]) * scale
    o_ref[0, 0] = jnp.concatenate([nope, pe], axis=1).astype(o_ref.dtype)


def q_proj(cqn, wq_r, tab64, batch, seq, tm=1024):
    nsb = seq // tm
    scale = float((A_NOPE + A_ROPE) ** -0.5)
    return pl.pallas_call(
        functools.partial(_q_proj_kernel, scale=scale),
        grid=(A_HEADS, batch * nsb),
        in_specs=[pl.BlockSpec((tm, A_Q_RANK), lambda h, i: (i, 0)),
                  pl.BlockSpec((1, A_Q_RANK, 2 * LANES), lambda h, i: (h, 0, 0)),
                  pl.BlockSpec((tm, LANES), lambda h, i: (i, 0))],
        out_specs=pl.BlockSpec((1, 1, tm, 2 * LANES), lambda h, i: (i // nsb, h, i % nsb, 0)),
        out_shape=jax.ShapeDtypeStruct((batch, A_HEADS, seq, 2 * LANES), BF16),
        compiler_params=_params(("arbitrary", "arbitrary")),
        name="q_proj",
    )(cqn, wq_r, tab64)


def _kv_proj_kernel(x_ref, w_ref, kpe_ref, k_ref, v_ref):
    acc = _dot(x_ref[...], _bf(w_ref[...]))
    k_ref[0, 0] = jnp.concatenate([acc[:, :A_NOPE].astype(BF16), kpe_ref[...]], axis=1)
    v_ref[0, 0] = acc[:, A_NOPE:].astype(BF16)


def kv_proj(ckvn, w_ukv, kpe, batch, seq, tm=1024):
    nsb = seq // tm
    hw = A_NOPE + A_V
    return pl.pallas_call(
        _kv_proj_kernel,
        grid=(A_HEADS, batch * nsb),
        in_specs=[pl.BlockSpec((tm, A_KV_RANK), lambda h, i: (i, 0)),
                  pl.BlockSpec((A_KV_RANK, hw), lambda h, i: (0, h)),
                  pl.BlockSpec((tm, LANES), lambda h, i: (i, 0))],
        out_specs=[pl.BlockSpec((1, 1, tm, 2 * LANES), lambda h, i: (i // nsb, h, i % nsb, 0)),
                   pl.BlockSpec((1, 1, tm, A_V), lambda h, i: (i // nsb, h, i % nsb, 0))],
        out_shape=[jax.ShapeDtypeStruct((batch, A_HEADS, seq, 2 * LANES), BF16),
                   jax.ShapeDtypeStruct((batch, A_HEADS, seq, A_V), BF16)],
        compiler_params=_params(("arbitrary", "arbitrary")),
        name="kv_proj",
    )(ckvn, w_ukv, kpe)


def _flash_kernel(q_ref, k_ref, v_ref, o_ref, m_sc, l_sc, acc_sc, *, blk):
    qi = pl.program_id(2)
    ki = pl.program_id(3)

    @pl.when(ki == 0)
    def _():
        m_sc[...] = jnp.full_like(m_sc, NEG_BIG)
        l_sc[...] = jnp.zeros_like(l_sc)
        acc_sc[...] = jnp.zeros_like(acc_sc)

    @pl.when(ki <= qi)
    def _():
        s = _dot_nt(q_ref[0, 0], k_ref[0, 0])
        row = qi * blk + lax.broadcasted_iota(jnp.int32, s.shape, 0)
        col = ki * blk + lax.broadcasted_iota(jnp.int32, s.shape, 1)
        s = jnp.where(col <= row, s, NEG_BIG)
        m_prev = m_sc[...]
        m_new = jnp.maximum(m_prev, jnp.max(s, axis=-1, keepdims=True))
        alpha = jnp.exp(m_prev - m_new)
        p = jnp.exp(s - m_new)
        l_sc[...] = alpha * l_sc[...] + jnp.sum(p, axis=-1, keepdims=True)
        acc_sc[...] = alpha * acc_sc[...] + _dot(p.astype(BF16), v_ref[0, 0])
        m_sc[...] = m_new

    @pl.when(ki == pl.num_programs(3) - 1)
    def _():
        o_ref[...] = (acc_sc[...] / l_sc[...]).astype(o_ref.dtype)


def causal_attention(q, k, v, blk=512):
    b, h, s, dk = q.shape
    dv = v.shape[-1]
    nb = s // blk
    return pl.pallas_call(
        functools.partial(_flash_kernel, blk=blk),
        grid=(b, h, nb, nb),
        in_specs=[pl.BlockSpec((1, 1, blk, dk), lambda bi, hi, qi, ki: (bi, hi, qi, 0)),
                  pl.BlockSpec((1, 1, blk, dk), lambda bi, hi, qi, ki: (bi, hi, jnp.minimum(ki, qi), 0)),
                  pl.BlockSpec((1, 1, blk, dv), lambda bi, hi, qi, ki: (bi, hi, jnp.minimum(ki, qi), 0))],
        out_specs=pl.BlockSpec((blk, dv), lambda bi, hi, qi, ki: (bi * nb + qi, hi)),
        out_shape=jax.ShapeDtypeStruct((b * s, h * dv), BF16),
        scratch_shapes=[pltpu.VMEM((blk, 1), F32), pltpu.VMEM((blk, 1), F32),
                        pltpu.VMEM((blk, dv), F32)],
        compiler_params=_params(("arbitrary", "arbitrary", "arbitrary", "arbitrary")),
        name="causal_attention",
    )(q, k, v)


def _rope_half(x, cos, sin):
    half = x.shape[1] // 2
    x1, x2 = x[:, :half], x[:, half:]
    return jnp.concatenate([x1 * cos - x2 * sin, x1 * sin + x2 * cos], axis=1)


def _retention_kernel(cdec_ref, q_ref, k_ref, v_ref, g_ref, cos_ref, sin_ref, intra_ref, qdec_ref,
                      kdec_ref, rn_ref, o_ref, state_sc, *, n_chunks):
    hi = pl.program_id(1)

    @pl.when(pl.program_id(2) == 0)
    def _():
        state_sc[...] = jnp.zeros_like(state_sc)

    chunk_decay = cdec_ref[hi]
    intra = intra_ref[0]
    q_decay = qdec_ref[0]
    k_decay = kdec_ref[0]
    rn = rn_ref[...]
    k_scale = float(B_QK ** -0.5)
    for c in range(n_chunks):
        sl = slice(c * RET_CHUNK, (c + 1) * RET_CHUNK)
        cos = cos_ref[sl, :]
        sin = sin_ref[sl, :]
        qr = _rope_half(q_ref[sl, :].astype(F32), cos, sin)
        kr = _rope_half(k_ref[sl, :].astype(F32), cos, sin) * k_scale
        v = v_ref[sl, :]
        qb = qr.astype(BF16)
        s = _dot_nt(qb, kr.astype(BF16)) * intra
        inner = _dot(s.astype(BF16), v)
        state = state_sc[...]
        cross = _dot(qb, state.astype(BF16)) * q_decay
        state_sc[...] = state * chunk_decay + _dot_tn((kr * k_decay).astype(BF16), v)
        y = inner + cross
        mu = jnp.mean(y, axis=-1, keepdims=True)
        yc = y - mu
        var = jnp.mean(yc * yc, axis=-1, keepdims=True)
        yn = yc * lax.rsqrt(var + EPS) * rn
        o_ref[sl, :] = (yn * jax.nn.silu(g_ref[sl, :].astype(F32))).astype(o_ref.dtype)


def retention(z, cos, sin, ret_norm, batch, seq, rows=1024):
    t = z.shape[0]
    nr = seq // rows
    hh = jnp.arange(B_HEADS, dtype=F32)
    log_g = jnp.log1p(-jnp.exp2(-5.0 - hh))
    idx = jnp.arange(RET_CHUNK, dtype=F32)
    diff = idx[:, None] - idx[None, :]
    intra = jnp.where(diff >= 0, jnp.exp(log_g[:, None, None] * jnp.maximum(diff, 0.0)), 0.0)
    ones = jnp.ones((1, 1, B_QK), F32)
    q_decay = jnp.exp(log_g[:, None] * (idx + 1.0))[:, :, None] * ones
    k_decay = jnp.exp(log_g[:, None] * (RET_CHUNK - 1.0 - idx))[:, :, None] * ones
    chunk_decay = jnp.exp(log_g * RET_CHUNK)

    def zspec(off):
        return pl.BlockSpec((rows, B_QK), lambda b, h, r, cd: (b * nr + r, off // B_QK + h))

    tab = pl.BlockSpec((rows, B_QK // 2), lambda b, h, r, cd: (b * nr + r, 0))
    per_head = pl.BlockSpec((1, RET_CHUNK, B_QK), lambda b, h, r, cd: (h, 0, 0))
    grid_spec = pltpu.PrefetchScalarGridSpec(
        num_scalar_prefetch=1,
        grid=(batch, B_HEADS, nr),
        in_specs=[zspec(Z_RQ), zspec(Z_RK), zspec(Z_RV), zspec(Z_RG), tab, tab,
                  pl.BlockSpec((1, RET_CHUNK, RET_CHUNK), lambda b, h, r, cd: (h, 0, 0)),
                  per_head, per_head,
                  pl.BlockSpec((1, B_V), lambda b, h, r, cd: (0, h))],
        out_specs=pl.BlockSpec((rows, B_V), lambda b, h, r, cd: (b * nr + r, h)),
        scratch_shapes=[pltpu.VMEM((B_QK, B_V), F32)],
    )
    return pl.pallas_call(
        functools.partial(_retention_kernel, n_chunks=rows // RET_CHUNK),
        grid_spec=grid_spec,
        out_shape=jax.ShapeDtypeStruct((t, B_HEADS * B_V), BF16),
        compiler_params=_params(("arbitrary", "arbitrary", "arbitrary")),
        name="retention",
    )(chunk_decay, z, z, z, z, cos, sin, intra, q_decay, k_decay, ret_norm.reshape(1, -1))


def _sgu_kernel(u_ref, v_ref, vn_ref, ws_ref, bs_ref, o_ref, *, n_chunks):
    gw = C_WIDTH // C_GROUPS
    v = v_ref[...].astype(F32)
    mu = jnp.mean(v, axis=-1, keepdims=True)
    vc = v - mu
    var = jnp.mean(vc * vc, axis=-1, keepdims=True)
    vn = (vc * lax.rsqrt(var + EPS) * vn_ref[...]).astype(BF16)
    r = lax.broadcasted_iota(jnp.int32, (C_CHUNK, C_CHUNK), 0)
    c = lax.broadcasted_iota(jnp.int32, (C_CHUNK, C_CHUNK), 1)
    bs = bs_ref[...]
    for g in range(C_GROUPS):
        ws = jnp.where(c <= r, ws_ref[g], 0.0).astype(BF16)
        bias = bs[:, g:g + 1]
        cols = slice(g * gw, (g + 1) * gw)
        for ch in range(n_chunks):
            rows = slice(ch * C_CHUNK, (ch + 1) * C_CHUNK)
            mixed = _dot(ws, vn[rows, cols]) + bias
            o_ref[rows, cols] = (u_ref[rows, cols].astype(F32) * mixed).astype(o_ref.dtype)


def spatial_gate(z, v_norm, w_s, b_s, rows=512):
    t = z.shape[0]
    return pl.pallas_call(
        functools.partial(_sgu_kernel, n_chunks=rows // C_CHUNK),
        grid=(t // rows,),
        in_specs=[pl.BlockSpec((rows, C_WIDTH), lambda i: (i, 0)),
                  pl.BlockSpec((rows, C_WIDTH), lambda i: (i, 1)),
                  pl.BlockSpec((1, C_WIDTH), lambda i: (0, 0)),
                  pl.BlockSpec((C_GROUPS, C_CHUNK, C_CHUNK), lambda i: (0, 0, 0)),
                  pl.BlockSpec((C_CHUNK, C_GROUPS), lambda i: (0, 0))],
        out_specs=pl.BlockSpec((rows, C_WIDTH), lambda i: (i, 0)),
        out_shape=jax.ShapeDtypeStruct((t, C_WIDTH), BF16),
        compiler_params=_params(("arbitrary",)),
        name="spatial_gate",
    )(z, z, v_norm.reshape(1, -1), w_s, b_s.T)


def _xattn_kernel(x_ref, gpre_ref, wq_ref, k_ref, v_ref, wo_ref, gpost_ref, gnext_ref, xo_ref, hn_ref):
    x = x_ref[...]
    hn = _rms(x, gpre_ref[...]).astype(BF16)
    q = (_dot(hn, wq_ref[...]) * float(X_HEAD_DIM ** -0.5)).astype(BF16)
    k = k_ref[...]
    v = v_ref[...]
    outs = []
    for h in range(X_HEADS):
        cols = slice(h * X_HEAD_DIM, (h + 1) * X_HEAD_DIM)
        s = _dot_nt(q[:, cols], k[:, cols])
        p = jnp.exp(s - jnp.max(s, axis=-1, keepdims=True))
        o = _dot(p.astype(BF16), v[:, cols]) / jnp.sum(p, axis=-1, keepdims=True)
        outs.append(o.astype(BF16))
    hh = _dot(jnp.concatenate(outs, axis=1), wo_ref[...])
    xn = x + _rms(hh, gpost_ref[...])
    xo_ref[...] = xn
    hn_ref[...] = _rms(xn, gnext_ref[...]).astype(hn_ref.dtype)


def cross_attention(x, kv, wq, wo, g_pre, g_post, g_next, seq, tm=256):
    t, d = x.shape
    hd = X_HEADS * X_HEAD_DIM
    nsb = seq // tm
    row = pl.BlockSpec((tm, d), lambda i: (i, 0))
    vec = pl.BlockSpec((1, d), lambda i: (0, 0))
    return pl.pallas_call(
        _xattn_kernel,
        grid=(t // tm,),
        in_specs=[row, vec,
                  pl.BlockSpec((d, hd), lambda i: (0, 0)),
                  pl.BlockSpec((N_MEM, hd), lambda i: (i // nsb, 0)),
                  pl.BlockSpec((N_MEM, hd), lambda i: (i // nsb, 1)),
                  pl.BlockSpec((hd, d), lambda i: (0, 0)),
                  vec, vec],
        out_specs=[row, row],
        out_shape=[jax.ShapeDtypeStruct((t, d), F32), jax.ShapeDtypeStruct((t, d), BF16)],
        compiler_params=_params(("arbitrary",)),
        name="cross_attention",
    )(x, g_pre.reshape(1, d), wq, kv, kv, wo, g_post.reshape(1, d), g_next.reshape(1, d))


def _router_kernel(x_ref, g_ref, wt_ref, idx_ref, gate_ref):
    hn = _rms(x_ref[...], g_ref[...])
    wt = wt_ref[...]
    h_hi = hn.astype(BF16)
    h_lo = (hn - h_hi.astype(F32)).astype(BF16)
    w_hi = wt.astype(BF16)
    w_lo = (wt - w_hi.astype(F32)).astype(BF16)
    logits = _dot_nt(w_hi, h_hi) + (_dot_nt(w_hi, h_lo) + _dot_nt(w_lo, h_hi))
    e = lax.broadcasted_iota(jnp.int32, logits.shape, 0)
    m1 = jnp.max(logits, axis=0, keepdims=True)
    i1 = jnp.min(jnp.where(logits == m1, e, N_EXPERTS), axis=0, keepdims=True)
    rest = jnp.where(e == i1, -jnp.inf, logits)
    m2 = jnp.max(rest, axis=0, keepdims=True)
    i2 = jnp.min(jnp.where(rest == m2, e, N_EXPERTS), axis=0, keepdims=True)
    ex = jnp.exp(m2 - m1)
    g1 = 1.0 / (1.0 + ex)
    idx_ref[...] = jnp.concatenate([i1, i2], axis=0)
    gate_ref[...] = jnp.concatenate([g1, ex * g1], axis=0)


def router(x, g, w_router, tm=512):
    t, d = x.shape
    return pl.pallas_call(
        _router_kernel,
        grid=(t // tm,),
        in_specs=[pl.BlockSpec((tm, d), lambda i: (i, 0)),
                  pl.BlockSpec((1, d), lambda i: (0, 0)),
                  pl.BlockSpec((N_EXPERTS, d), lambda i: (0, 0))],
        out_specs=[pl.BlockSpec((2, tm), lambda i: (0, i)),
                   pl.BlockSpec((2, tm), lambda i: (0, i))],
        out_shape=[jax.ShapeDtypeStruct((2, t), jnp.int32), jax.ShapeDtypeStruct((2, t), F32)],
        compiler_params=_params(("arbitrary",)),
        name="router",
    )(x, g.reshape(1, d), w_router.T)


def _moe_gu_kernel(te_ref, tx_ref, nu_ref, x_ref, wg_ref, wu_ref, o_ref):
    t = pl.program_id(1)

    @pl.when(t < nu_ref[0])
    def _():
        x = x_ref[...]
        g = _dot(x, _bf(wg_ref[0]))
        u = _dot(x, _bf(wu_ref[0]))
        o_ref[...] = (jax.nn.silu(g) * u).astype(o_ref.dtype)

    @pl.when(t >= nu_ref[0])
    def _():
        o_ref[...] = jnp.zeros_like(o_ref)


def moe_gate_up(xs, w_gu, tile_e, tile_x, n_used, tn=256):
    p, d = xs.shape
    f = w_gu.shape[2] // 2
    nf = f // tn
    nt = p // MOE_TM
    grid_spec = pltpu.PrefetchScalarGridSpec(
        num_scalar_prefetch=3,
        grid=(nf, nt),
        in_specs=[pl.BlockSpec((MOE_TM, d), lambda j, t, te, tx, nu: (tx[t], 0)),
                  pl.BlockSpec((1, d, tn), lambda j, t, te, tx, nu: (te[t], 0, j)),
                  pl.BlockSpec((1, d, tn), lambda j, t, te, tx, nu: (te[t], 0, j + nf))],
        out_specs=pl.BlockSpec((MOE_TM, tn), lambda j, t, te, tx, nu: (t, j)),
    )
    return pl.pallas_call(
        _moe_gu_kernel,
        grid_spec=grid_spec,
        out_shape=jax.ShapeDtypeStruct((p, f), BF16),
        compiler_params=_params(("arbitrary", "arbitrary")),
        name="moe_gate_up",
    )(tile_e, tile_x, n_used, xs, w_gu, w_gu)


def _moe_down_kernel(te_ref, tx_ref, nu_ref, a_ref, w_ref, g_ref, o_ref):
    t = pl.program_id(1)

    @pl.when(t < nu_ref[0])
    def _():
        o_ref[...] = _dot(a_ref[...], _bf(w_ref[0])) * g_ref[...]

    @pl.when(t >= nu_ref[0])
    def _():
        o_ref[...] = jnp.zeros_like(o_ref)


def moe_down(act, w_down, row_gate, tile_e, tile_x, n_used, tn=512):
    p, f = act.shape
    d = w_down.shape[2]
    nt = p // MOE_TM
    grid_spec = pltpu.PrefetchScalarGridSpec(
        num_scalar_prefetch=3,
        grid=(d // tn, nt),
        in_specs=[pl.BlockSpec((MOE_TM, f), lambda j, t, te, tx, nu: (tx[t], 0)),
                  pl.BlockSpec((1, f, tn), lambda j, t, te, tx, nu: (te[t], 0, j)),
                  pl.BlockSpec((MOE_TM, 1), lambda j, t, te, tx, nu: (tx[t], 0))],
        out_specs=pl.BlockSpec((MOE_TM, tn), lambda j, t, te, tx, nu: (t, j)),
    )
    return pl.pallas_call(
        _moe_down_kernel,
        grid_spec=grid_spec,
        out_shape=jax.ShapeDtypeStruct((p, d), F32),
        compiler_params=_params(("arbitrary", "arbitrary")),
        name="moe_down",
    )(tile_e, tile_x, n_used, act, w_down, row_gate)


def moe_layer(x, g_pre, hn, w_router, w_gu, w_down):
    t, d = x.shape
    idx, gates = router(x, g_pre, w_router)
    e_flat = idx.reshape(-1)
    onehot = (e_flat[:, None] == jnp.arange(N_EXPERTS, dtype=jnp.int32)[None, :]).astype(jnp.int32)
    csum = jnp.cumsum(onehot, axis=0)
    rank = jnp.take_along_axis(csum, e_flat[:, None], axis=1)[:, 0] - 1
    n_tiles = (csum[-1] + MOE_TM - 1) // MOE_TM
    tile_end = jnp.cumsum(n_tiles)
    pos = (tile_end - n_tiles)[e_flat] * MOE_TM + rank
    p = 2 * t + N_EXPERTS * MOE_TM
    nt = p // MOE_TM
    row_tok = jnp.zeros((p,), jnp.int32).at[pos].set(jnp.tile(jnp.arange(t, dtype=jnp.int32), 2))
    row_gate = jnp.zeros((p,), F32).at[pos].set(gates.reshape(-1))
    n_used = tile_end[-1:].astype(jnp.int32)
    tile_x = jnp.minimum(jnp.arange(nt, dtype=jnp.int32), n_used[0] - 1)
    tile_e = jnp.minimum(jnp.searchsorted(tile_end, tile_x, side="right"), N_EXPERTS - 1).astype(jnp.int32)
    xs = jnp.take(hn, row_tok, axis=0)
    act = moe_gate_up(xs, w_gu, tile_e, tile_x, n_used)
    y = moe_down(act, w_down, row_gate.reshape(p, 1), tile_e, tile_x, n_used)
    return jnp.take(y, pos[:t], axis=0), jnp.take(y, pos[t:], axis=0)


def _rope_tables(positions):
    pos = positions.reshape(-1).astype(F32)
    inv_a = ROPE_BASE ** (-jnp.arange(0, A_ROPE, 2, dtype=F32) / A_ROPE)
    ang_a = pos[:, None] * inv_a
    cos_a, sin_a = jnp.cos(ang_a), jnp.sin(ang_a)
    tab64 = jnp.concatenate([cos_a, cos_a, -sin_a, sin_a], axis=1)
    inv_b = ROPE_BASE ** (-jnp.arange(0, B_QK, 2, dtype=F32) / B_QK)
    ang_b = pos[:, None] * inv_b
    return tab64, jnp.cos(ang_b), jnp.sin(ang_b)


def _swap_halves(w):
    half = w.shape[-1] // 2
    return jnp.concatenate([w[..., half:], w[..., :half]], axis=-1)


def kernel(x, mem, positions, norm_mix_pre, norm_mix_post, norm_x_pre, norm_x_post, norm_ffn_pre, norm_ffn_post, norm_mem, x_w_q, x_w_kv, x_w_o, ab_w_in, ab_q_norm, ab_w_uq, ab_kv_norm, ab_w_ukv, ab_ret_norm, ab_w_out, ffn_w_gu, ffn_w_down, c_w_in, c_v_norm, c_w_s, c_b_s, c_w_out, moe_router, moe_w_gu, moe_w_down):
    batch, seq, d = x.shape
    t = batch * seq
    x = x.reshape(t, d)
    mem2 = mem.reshape(batch * N_MEM, d)
    tab64, cos_b, sin_b = _rope_tables(positions)

    a_in = A_Q_RANK + A_KV_RANK + A_ROPE
    w_in = ab_w_in[0]
    w_kr = w_in[:, A_Q_RANK + A_KV_RANK:a_in]
    w_ab = jnp.concatenate([w_in[:, :A_Q_RANK + A_KV_RANK], w_in[:, a_in:], w_kr, _swap_halves(w_kr)],
                           axis=1).astype(BF16)
    w_uq = ab_w_uq[0].reshape(A_Q_RANK, A_HEADS, A_NOPE + A_ROPE)
    w_pe = w_uq[:, :, A_NOPE:]
    wq_r = jnp.concatenate([w_uq[:, :, :A_NOPE], w_pe, _swap_halves(w_pe)], axis=-1)
    wq_r = wq_r.transpose(1, 0, 2).astype(BF16)

    hn = rms_norm_bf16(x, norm_mix_pre[0])
    z = matmul(hn, w_ab, BF16, tm=1024, tn=896, name="ab_in_proj")
    cqn, ckvn, kpe = a_norm(z, tab64, ab_q_norm[0], ab_kv_norm[0])
    q = q_proj(cqn, wq_r, tab64, batch, seq)
    k, v = kv_proj(ckvn, ab_w_ukv[0], kpe, batch, seq)
    ya = causal_attention(q, k, v)
    yb = retention(z, cos_b, sin_b, ab_ret_norm[0], batch, seq)
    h = matmul_cat(ya, yb, ab_w_out[0], F32, tm=1024, tn=512, name="ab_out_proj")
    x = resid_norm([h], x, norm_mix_post[0])
    memn = rms_norm_bf16(mem2, norm_mem[0])
    kv = matmul(memn, x_w_kv[0], BF16, tm=512, tn=512, name="mem_kv_proj")
    x, hn = cross_attention(x, kv, x_w_q[0].astype(BF16), x_w_o[0].astype(BF16),
                            norm_x_pre[0], norm_x_post[0], norm_ffn_pre[0], seq)
    act = matmul_swiglu(hn, ffn_w_gu[0], tm=1024, tn=256)
    h = matmul_ksplit(act, ffn_w_down[0].astype(BF16), tm=1024, tn=2048, tk=512)
    x, hn = resid_norm([h], x, norm_ffn_post[0], norm_mix_pre[1])

    zc = matmul(hn, c_w_in[0], BF16, tm=1024, tn=512, act="gelu", name="c_in_proj")
    yc = spatial_gate(zc, c_v_norm[0], c_w_s[0], c_b_s[0])
    h = matmul(yc, c_w_out[0], F32, tm=1024, tn=512, name="c_out_proj")
    x = resid_norm([h], x, norm_mix_post[1])
    memn = rms_norm_bf16(mem2, norm_mem[1])
    kv = matmul(memn, x_w_kv[1], BF16, tm=512, tn=512, name="mem_kv_proj")
    x, hn = cross_attention(x, kv, x_w_q[1].astype(BF16), x_w_o[1].astype(BF16),
                            norm_x_pre[1], norm_x_post[1], norm_ffn_pre[1], seq)
    y0, y1 = moe_layer(x, norm_ffn_pre[1], hn, moe_router[0], moe_w_gu[0], moe_w_down[0])
    x = resid_norm([y0, y1], x, norm_ffn_post[1])
    return x.reshape(batch, seq, d)
```

```python
import functools

import jax
import jax.numpy as jnp
from jax import lax
from jax.experimental import pallas as pl
from jax.experimental.pallas import tpu as pltpu

F32 = jnp.float32
BF16 = jnp.bfloat16

D_MODEL = 4096
N_MEM = 256
EPS = 1e-6
ROPE_BASE = 10000.0
A_HEADS = 16
A_Q_RANK = 1024
A_KV_RANK = 512
A_NOPE = 128
A_ROPE = 64
A_V = 128
B_HEADS = 8
B_QK = 256
B_V = 256
RET_CHUNK = 128
C_GROUPS = 8
C_WIDTH = 4096
C_CHUNK = 128
X_HEADS = 4
X_HEAD_DIM = 128
D_FF = 14336
N_EXPERTS = 8
D_EXPERT = 3584

LANES = 128
VMEM_LIMIT = 56 * 1024 * 1024
NEG_BIG = -1e30

Z_CQ = 0
Z_CKV = A_Q_RANK
Z_RQ = A_Q_RANK + A_KV_RANK
Z_RK = Z_RQ + B_HEADS * B_QK
Z_RV = Z_RK + B_HEADS * B_QK
Z_RG = Z_RV + B_HEADS * B_V
Z_W = Z_RG + B_HEADS * B_V
A_IN = A_Q_RANK + A_KV_RANK + A_ROPE

MOE_TM = 1024
MOE_SUB = 256


def _params(sem, vmem=VMEM_LIMIT):
    return pltpu.CompilerParams(dimension_semantics=sem, vmem_limit_bytes=vmem)


def _rms(xf, g):
    ms = jnp.mean(xf * xf, axis=-1, keepdims=True)
    return xf * lax.rsqrt(ms + EPS) * g


def _bf(w):
    return w if w.dtype == BF16 else w.astype(BF16)


def _dot(a, b):
    return jnp.dot(a, b, preferred_element_type=F32)


def _dot_nt(a, b):
    return lax.dot_general(a, b, (((1,), (1,)), ((), ())), preferred_element_type=F32)


def _dot_tn(a, b):
    return lax.dot_general(a, b, (((0,), (0,)), ((), ())), preferred_element_type=F32)


def _norm_kernel(x_ref, g_ref, o_ref):
    o_ref[...] = _rms(x_ref[...].astype(F32), g_ref[...]).astype(o_ref.dtype)


def rms_norm_bf16(x, g, tm=512):
    m, d = x.shape
    return pl.pallas_call(
        _norm_kernel,
        grid=(m // tm,),
        in_specs=[pl.BlockSpec((tm, d), lambda i: (i, 0)),
                  pl.BlockSpec((1, d), lambda i: (0, 0))],
        out_specs=pl.BlockSpec((tm, d), lambda i: (i, 0)),
        out_shape=jax.ShapeDtypeStruct((m, d), BF16),
        compiler_params=_params(("arbitrary",)),
        name="rms_norm",
    )(x, g.reshape(1, d))


def _resid_kernel(*refs, n_h, gated, with_next):
    h_refs, rest = refs[:n_h], refs[n_h:]
    if gated:
        w = rest[0][...]
        rest = rest[1:]
        h = h_refs[0][...] * w[:, 0:1]
        for c, r in enumerate(h_refs[1:], start=1):
            h = h + r[...] * w[:, c:c + 1]
    else:
        h = h_refs[0][...]
        for r in h_refs[1:]:
            h = h + r[...]
    x_ref, g_ref = rest[0], rest[1]
    xn = x_ref[...] + _rms(h, g_ref[...])
    if with_next:
        gn_ref, xo_ref, hn_ref = rest[2:]
        xo_ref[...] = xn
        hn_ref[...] = _rms(xn, gn_ref[...]).astype(hn_ref.dtype)
    else:
        xo_ref = rest[2]
        xo_ref[...] = xn


def resid_norm(hs, x, g_post, g_next=None, weights=None, tm=256):
    m, d = x.shape
    row = pl.BlockSpec((tm, d), lambda i: (i, 0))
    vec = pl.BlockSpec((1, d), lambda i: (0, 0))
    with_next = g_next is not None
    gated = weights is not None
    in_specs = [row] * len(hs) + ([pl.BlockSpec((tm, len(hs)), lambda i: (i, 0))] if gated else [])
    in_specs += [row, vec] + ([vec] if with_next else [])
    args = list(hs) + ([weights] if gated else [])
    args += [x, g_post.reshape(1, d)] + ([g_next.reshape(1, d)] if with_next else [])
    out_shape = [jax.ShapeDtypeStruct((m, d), F32)]
    out_specs = [row]
    if with_next:
        out_shape.append(jax.ShapeDtypeStruct((m, d), BF16))
        out_specs.append(row)
    res = pl.pallas_call(
        functools.partial(_resid_kernel, n_h=len(hs), gated=gated, with_next=with_next),
        grid=(m // tm,),
        in_specs=in_specs,
        out_specs=out_specs,
        out_shape=out_shape,
        compiler_params=_params(("arbitrary",)),
        name="resid_norm",
    )(*args)
    return res if with_next else res[0]


def _mm_kernel(x_ref, w_ref, o_ref, *, act):
    acc = _dot(x_ref[...], _bf(w_ref[...]))
    if act == "gelu":
        acc = jax.nn.gelu(acc)
    o_ref[...] = acc.astype(o_ref.dtype)


def matmul(x, w, out_dtype, tm, tn, act=None, name="matmul"):
    m, k = x.shape
    n = w.shape[1]
    return pl.pallas_call(
        functools.partial(_mm_kernel, act=act),
        grid=(n // tn, m // tm),
        in_specs=[pl.BlockSpec((tm, k), lambda j, i: (i, 0)),
                  pl.BlockSpec((k, tn), lambda j, i: (0, j))],
        out_specs=pl.BlockSpec((tm, tn), lambda j, i: (i, j)),
        out_shape=jax.ShapeDtypeStruct((m, n), out_dtype),
        compiler_params=_params(("arbitrary", "arbitrary")),
        name=name,
    )(x, w)


def _mm2_kernel(xa_ref, xb_ref, w_ref, o_ref):
    ka = xa_ref.shape[1]
    acc = _dot(xa_ref[...], _bf(w_ref[:ka, :]))
    acc = acc + _dot(xb_ref[...], _bf(w_ref[ka:, :]))
    o_ref[...] = acc.astype(o_ref.dtype)


def matmul_cat(xa, xb, w, out_dtype, tm, tn, name="matmul_cat"):
    m, ka = xa.shape
    kb = xb.shape[1]
    n = w.shape[1]
    return pl.pallas_call(
        _mm2_kernel,
        grid=(n // tn, m // tm),
        in_specs=[pl.BlockSpec((tm, ka), lambda j, i: (i, 0)),
                  pl.BlockSpec((tm, kb), lambda j, i: (i, 0)),
                  pl.BlockSpec((ka + kb, tn), lambda j, i: (0, j))],
        out_specs=pl.BlockSpec((tm, tn), lambda j, i: (i, j)),
        out_shape=jax.ShapeDtypeStruct((m, n), out_dtype),
        compiler_params=_params(("arbitrary", "arbitrary")),
        name=name,
    )(xa, xb, w)


def _swiglu_kernel(x_ref, wg_ref, wu_ref, o_ref):
    x = x_ref[...]
    g = _dot(x, _bf(wg_ref[...]))
    u = _dot(x, _bf(wu_ref[...]))
    o_ref[...] = (jax.nn.silu(g) * u).astype(o_ref.dtype)


def matmul_swiglu(x, w_gu, tm, tn, name="ffn_gate_up"):
    m, k = x.shape
    f = w_gu.shape[1] // 2
    nf = f // tn
    return pl.pallas_call(
        _swiglu_kernel,
        grid=(nf, m // tm),
        in_specs=[pl.BlockSpec((tm, k), lambda j, i: (i, 0)),
                  pl.BlockSpec((k, tn), lambda j, i: (0, j)),
                  pl.BlockSpec((k, tn), lambda j, i: (0, j + nf))],
        out_specs=pl.BlockSpec((tm, tn), lambda j, i: (i, j)),
        out_shape=jax.ShapeDtypeStruct((m, f), BF16),
        compiler_params=_params(("arbitrary", "arbitrary")),
        name=name,
    )(x, w_gu, w_gu)


def _mm_acc_kernel(x_ref, w_ref, o_ref):
    @pl.when(pl.program_id(2) == 0)
    def _():
        o_ref[...] = jnp.zeros_like(o_ref)

    o_ref[...] += _dot(x_ref[...], _bf(w_ref[...]))


def matmul_ksplit(x, w, tm, tn, tk, name="ffn_down"):
    m, k = x.shape
    n = w.shape[1]
    return pl.pallas_call(
        _mm_acc_kernel,
        grid=(m // tm, n // tn, k // tk),
        in_specs=[pl.BlockSpec((tm, tk), lambda i, j, kk: (i, kk)),
                  pl.BlockSpec((tk, tn), lambda i, j, kk: (kk, j))],
        out_specs=pl.BlockSpec((tm, tn), lambda i, j, kk: (i, j)),
        out_shape=jax.ShapeDtypeStruct((m, n), F32),
        compiler_params=_params(("arbitrary", "arbitrary", "arbitrary")),
        name=name,
    )(x, w)


def _rope64(pe_pair, tab):
    p = pe_pair * tab
    return p + pltpu.roll(p, A_ROPE, axis=1)


def _ab_in_kernel(x_ref, wa_ref, wb_ref, o_ref, w_sc, *, n_plain, shift, rows):
    j = pl.program_id(0)
    first = pl.program_id(1) == 0

    @pl.when(jnp.logical_and(first, j < n_plain))
    def _():
        w_sc[...] = wa_ref[...].astype(BF16)

    @pl.when(jnp.logical_and(first, j >= n_plain))
    def _():
        for r in range(0, wa_ref.shape[0], rows):
            wa = wa_ref[r:r + rows, :]
            wb = wb_ref[r:r + rows, :]
            w_sc[r:r + rows, :] = jnp.concatenate([wa[:, shift:], wb[:, :shift]], axis=1).astype(BF16)

    o_ref[...] = _dot(x_ref[...], w_sc[...]).astype(o_ref.dtype)


def ab_in_proj(x, w_in, tm=1024, tn=512):
    m, k = x.shape
    plain = A_Q_RANK + A_KV_RANK
    shift = A_IN - plain
    assert plain % tn == 0 and tn % LANES == 0 and 0 < shift < LANES and Z_W % tn == 0
    lanes_per_tile = tn // LANES
    return pl.pallas_call(
        functools.partial(_ab_in_kernel, n_plain=plain // tn, shift=shift, rows=256),
        grid=(Z_W // tn, m // tm),
        in_specs=[pl.BlockSpec((tm, k), lambda j, i: (i, 0)),
                  pl.BlockSpec((k, tn), lambda j, i: (0, j)),
                  pl.BlockSpec((k, LANES), lambda j, i: (0, (j + 1) * lanes_per_tile))],
        out_specs=pl.BlockSpec((tm, tn), lambda j, i: (i, j)),
        out_shape=jax.ShapeDtypeStruct((m, Z_W), BF16),
        scratch_shapes=[pltpu.VMEM((k, tn), BF16)],
        compiler_params=_params(("arbitrary", "arbitrary")),
        name="ab_in_proj",
    )(x, w_in, w_in)


def _a_norm_kernel(cq_ref, ckv_ref, x_ref, wkr_ref, tabc_ref, tabs_ref, qn_ref, kvn_ref,
                   cqn_ref, ckvn_ref, kpe_ref):
    cqn_ref[...] = _rms(cq_ref[...].astype(F32), qn_ref[...]).astype(BF16)
    ckvn_ref[...] = _rms(ckv_ref[...].astype(F32), kvn_ref[...]).astype(BF16)
    kr = _dot(x_ref[...], _bf(wkr_ref[...]))
    lane = lax.broadcasted_iota(jnp.int32, kr.shape, 1)
    half = A_ROPE // 2
    swapped = jnp.where(lane < half, pltpu.roll(kr, LANES - half, axis=1), pltpu.roll(kr, half, axis=1))
    kpe_ref[...] = (kr * tabc_ref[...] + swapped * tabs_ref[...]).astype(BF16)


def a_norm(z, hn, w_in, tab_kc, tab_ks, q_norm, kv_norm, tm=512):
    t, d = hn.shape
    kr_block = (A_Q_RANK + A_KV_RANK) // LANES
    return pl.pallas_call(
        _a_norm_kernel,
        grid=(t // tm,),
        in_specs=[pl.BlockSpec((tm, A_Q_RANK), lambda i: (i, Z_CQ // A_Q_RANK)),
                  pl.BlockSpec((tm, A_KV_RANK), lambda i: (i, Z_CKV // A_KV_RANK)),
                  pl.BlockSpec((tm, d), lambda i: (i, 0)),
                  pl.BlockSpec((d, LANES), lambda i: (0, kr_block)),
                  pl.BlockSpec((tm, LANES), lambda i: (i, 0)),
                  pl.BlockSpec((tm, LANES), lambda i: (i, 0)),
                  pl.BlockSpec((1, A_Q_RANK), lambda i: (0, 0)),
                  pl.BlockSpec((1, A_KV_RANK), lambda i: (0, 0))],
        out_specs=[pl.BlockSpec((tm, A_Q_RANK), lambda i: (i, 0)),
                   pl.BlockSpec((tm, A_KV_RANK), lambda i: (i, 0)),
                   pl.BlockSpec((tm, LANES), lambda i: (i, 0))],
        out_shape=[jax.ShapeDtypeStruct((t, A_Q_RANK), BF16),
                   jax.ShapeDtypeStruct((t, A_KV_RANK), BF16),
                   jax.ShapeDtypeStruct((t, LANES), BF16)],
        compiler_params=_params(("arbitrary",)),
        name="a_norm",
    )(z, z, hn, w_in, tab_kc, tab_ks, q_norm.reshape(1, -1), kv_norm.reshape(1, -1))


def _q_proj_kernel(x_ref, w_ref, tab_ref, o_ref, *, scale):
    acc = _dot(x_ref[...], w_ref[0])
    nope = acc[:, :A_NOPE] * scale
    pe = _rope64(acc[:, A_NOPE:], tab_ref[...]) * scale
    o_ref[0, 0] = jnp.concatenate([nope, pe], axis=1).astype(o_ref.dtype)


def q_proj(cqn, wq_r, tab64, batch, seq, tm=1024):
    nsb = seq // tm
    scale = float((A_NOPE + A_ROPE) ** -0.5)
    return pl.pallas_call(
        functools.partial(_q_proj_kernel, scale=scale),
        grid=(A_HEADS, batch * nsb),
        in_specs=[pl.BlockSpec((tm, A_Q_RANK), lambda h, i: (i, 0)),
                  pl.BlockSpec((1, A_Q_RANK, 2 * LANES), lambda h, i: (h, 0, 0)),
                  pl.BlockSpec((tm, LANES), lambda h, i: (i, 0))],
        out_specs=pl.BlockSpec((1, 1, tm, 2 * LANES), lambda h, i: (i // nsb, h, i % nsb, 0)),
        out_shape=jax.ShapeDtypeStruct((batch, A_HEADS, seq, 2 * LANES), BF16),
        compiler_params=_params(("arbitrary", "arbitrary")),
        name="q_proj",
    )(cqn, wq_r, tab64)


def _kv_proj_kernel(x_ref, w_ref, kpe_ref, k_ref, v_ref):
    acc = _dot(x_ref[...], _bf(w_ref[...]))
    k_ref[0, 0] = jnp.concatenate([acc[:, :A_NOPE].astype(BF16), kpe_ref[...]], axis=1)
    v_ref[0, 0] = acc[:, A_NOPE:].astype(BF16)


def kv_proj(ckvn, w_ukv, kpe, batch, seq, tm=1024):
    nsb = seq // tm
    hw = A_NOPE + A_V
    return pl.pallas_call(
        _kv_proj_kernel,
        grid=(A_HEADS, batch * nsb),
        in_specs=[pl.BlockSpec((tm, A_KV_RANK), lambda h, i: (i, 0)),
                  pl.BlockSpec((A_KV_RANK, hw), lambda h, i: (0, h)),
                  pl.BlockSpec((tm, LANES), lambda h, i: (i, 0))],
        out_specs=[pl.BlockSpec((1, 1, tm, 2 * LANES), lambda h, i: (i // nsb, h, i % nsb, 0)),
                   pl.BlockSpec((1, 1, tm, A_V), lambda h, i: (i // nsb, h, i % nsb, 0))],
        out_shape=[jax.ShapeDtypeStruct((batch, A_HEADS, seq, 2 * LANES), BF16),
                   jax.ShapeDtypeStruct((batch, A_HEADS, seq, A_V), BF16)],
        compiler_params=_params(("arbitrary", "arbitrary")),
        name="kv_proj",
    )(ckvn, w_ukv, kpe)


def _flash_kernel(q_ref, k_ref, v_ref, o_ref, *, tq, tk):
    qi = pl.program_id(2)
    q = q_ref[0, 0]
    dv = v_ref.shape[-1]

    def step(k, v, carry, mask):
        m, l, acc = carry
        s = _dot_nt(q, k)
        if mask is not None:
            s = jnp.where(mask, s, NEG_BIG)
        m_new = jnp.maximum(m, jnp.max(s, axis=-1, keepdims=True))
        alpha = jnp.exp(m - m_new)
        p = jnp.exp(s - m_new)
        l = alpha * l + jnp.sum(p, axis=-1, keepdims=True)
        acc = alpha * acc + _dot(p.astype(BF16), v)
        return m_new, l, acc

    def body(j, carry):
        off = pl.multiple_of(j * tk, tk)
        return step(k_ref[0, 0, pl.ds(off, tk), :], v_ref[0, 0, pl.ds(off, tk), :], carry, None)

    init = (jnp.full((tq, 1), NEG_BIG, F32), jnp.zeros((tq, 1), F32), jnp.zeros((tq, dv), F32))
    carry = lax.fori_loop(0, qi * (tq // tk), body, init)
    off = pl.multiple_of(qi * tq, tq)
    row = lax.broadcasted_iota(jnp.int32, (tq, tq), 0)
    col = lax.broadcasted_iota(jnp.int32, (tq, tq), 1)
    _, l, acc = step(k_ref[0, 0, pl.ds(off, tq), :], v_ref[0, 0, pl.ds(off, tq), :], carry, col <= row)
    o_ref[...] = (acc / l).astype(o_ref.dtype)


def causal_attention(q, k, v, tq=512, tk=512):
    b, h, s, dk = q.shape
    dv = v.shape[-1]
    nq = s // tq
    return pl.pallas_call(
        functools.partial(_flash_kernel, tq=tq, tk=tk),
        grid=(b, h, nq),
        in_specs=[pl.BlockSpec((1, 1, tq, dk), lambda bi, hi, qi: (bi, hi, qi, 0)),
                  pl.BlockSpec((1, 1, s, dk), lambda bi, hi, qi: (bi, hi, 0, 0)),
                  pl.BlockSpec((1, 1, s, dv), lambda bi, hi, qi: (bi, hi, 0, 0))],
        out_specs=pl.BlockSpec((tq, dv), lambda bi, hi, qi: (bi * nq + qi, hi)),
        out_shape=jax.ShapeDtypeStruct((b * s, h * dv), BF16),
        compiler_params=_params(("arbitrary", "arbitrary", "arbitrary")),
        name="causal_attention",
    )(q, k, v)


def _rope_half(x, cos, sin):
    half = x.shape[1] // 2
    x1, x2 = x[:, :half], x[:, half:]
    return jnp.concatenate([x1 * cos - x2 * sin, x1 * sin + x2 * cos], axis=1)


def _retention_kernel(cdec_ref, q_ref, k_ref, v_ref, g_ref, cos_ref, sin_ref, intra_ref, qdec_ref,
                      kdec_ref, rn_ref, o_ref, state_sc, *, n_chunks):
    hi = pl.program_id(1)

    @pl.when(pl.program_id(2) == 0)
    def _():
        state_sc[...] = jnp.zeros_like(state_sc)

    chunk_decay = cdec_ref[hi]
    intra = intra_ref[0]
    q_decay = qdec_ref[0]
    k_decay = kdec_ref[0]
    rn = rn_ref[...]
    k_scale = float(B_QK ** -0.5)
    for c in range(n_chunks):
        sl = slice(c * RET_CHUNK, (c + 1) * RET_CHUNK)
        cos = cos_ref[sl, :]
        sin = sin_ref[sl, :]
        qr = _rope_half(q_ref[sl, :].astype(F32), cos, sin)
        kr = _rope_half(k_ref[sl, :].astype(F32), cos, sin) * k_scale
        v = v_ref[sl, :]
        qb = qr.astype(BF16)
        s = _dot_nt(qb, kr.astype(BF16)) * intra
        inner = _dot(s.astype(BF16), v)
        state = state_sc[...]
        cross = _dot(qb, state.astype(BF16)) * q_decay
        state_sc[...] = state * chunk_decay + _dot_tn((kr * k_decay).astype(BF16), v)
        y = inner + cross
        mu = jnp.mean(y, axis=-1, keepdims=True)
        yc = y - mu
        var = jnp.mean(yc * yc, axis=-1, keepdims=True)
        yn = yc * lax.rsqrt(var + EPS) * rn
        o_ref[sl, :] = (yn * jax.nn.silu(g_ref[sl, :].astype(F32))).astype(o_ref.dtype)


def retention(z, cos, sin, ret_norm, batch, seq, rows=1024):
    t = z.shape[0]
    nr = seq // rows
    hh = jnp.arange(B_HEADS, dtype=F32)
    log_g = jnp.log1p(-jnp.exp2(-5.0 - hh))
    idx = jnp.arange(RET_CHUNK, dtype=F32)
    diff = idx[:, None] - idx[None, :]
    intra = jnp.where(diff >= 0, jnp.exp(log_g[:, None, None] * jnp.maximum(diff, 0.0)), 0.0)
    ones = jnp.ones((1, 1, B_QK), F32)
    q_decay = jnp.exp(log_g[:, None] * (idx + 1.0))[:, :, None] * ones
    k_decay = jnp.exp(log_g[:, None] * (RET_CHUNK - 1.0 - idx))[:, :, None] * ones
    chunk_decay = jnp.exp(log_g * RET_CHUNK)

    def zspec(off):
        return pl.BlockSpec((rows, B_QK), lambda b, h, r, cd: (b * nr + r, off // B_QK + h))

    tab = pl.BlockSpec((rows, B_QK // 2), lambda b, h, r, cd: (b * nr + r, 0))
    per_head = pl.BlockSpec((1, RET_CHUNK, B_QK), lambda b, h, r, cd: (h, 0, 0))
    grid_spec = pltpu.PrefetchScalarGridSpec(
        num_scalar_prefetch=1,
        grid=(batch, B_HEADS, nr),
        in_specs=[zspec(Z_RQ), zspec(Z_RK), zspec(Z_RV), zspec(Z_RG), tab, tab,
                  pl.BlockSpec((1, RET_CHUNK, RET_CHUNK), lambda b, h, r, cd: (h, 0, 0)),
                  per_head, per_head,
                  pl.BlockSpec((1, B_V), lambda b, h, r, cd: (0, h))],
        out_specs=pl.BlockSpec((rows, B_V), lambda b, h, r, cd: (b * nr + r, h)),
        scratch_shapes=[pltpu.VMEM((B_QK, B_V), F32)],
    )
    return pl.pallas_call(
        functools.partial(_retention_kernel, n_chunks=rows // RET_CHUNK),
        grid_spec=grid_spec,
        out_shape=jax.ShapeDtypeStruct((t, B_HEADS * B_V), BF16),
        compiler_params=_params(("arbitrary", "arbitrary", "arbitrary")),
        name="retention",
    )(chunk_decay, z, z, z, z, cos, sin, intra, q_decay, k_decay, ret_norm.reshape(1, -1))


def _sgu_kernel(u_ref, v_ref, vn_ref, ws_ref, bs_ref, o_ref, *, n_chunks):
    gw = C_WIDTH // C_GROUPS
    v = v_ref[...].astype(F32)
    mu = jnp.mean(v, axis=-1, keepdims=True)
    vc = v - mu
    var = jnp.mean(vc * vc, axis=-1, keepdims=True)
    vn = (vc * lax.rsqrt(var + EPS) * vn_ref[...]).astype(BF16)
    r = lax.broadcasted_iota(jnp.int32, (C_CHUNK, C_CHUNK), 0)
    c = lax.broadcasted_iota(jnp.int32, (C_CHUNK, C_CHUNK), 1)
    bs = bs_ref[...]
    for g in range(C_GROUPS):
        ws = jnp.where(c <= r, ws_ref[g], 0.0).astype(BF16)
        bias = bs[:, g:g + 1]
        cols = slice(g * gw, (g + 1) * gw)
        for ch in range(n_chunks):
            rows = slice(ch * C_CHUNK, (ch + 1) * C_CHUNK)
            mixed = _dot(ws, vn[rows, cols]) + bias
            o_ref[rows, cols] = (u_ref[rows, cols].astype(F32) * mixed).astype(o_ref.dtype)


def spatial_gate(z, v_norm, w_s, b_s, rows=512):
    t = z.shape[0]
    return pl.pallas_call(
        functools.partial(_sgu_kernel, n_chunks=rows // C_CHUNK),
        grid=(t // rows,),
        in_specs=[pl.BlockSpec((rows, C_WIDTH), lambda i: (i, 0)),
                  pl.BlockSpec((rows, C_WIDTH), lambda i: (i, 1)),
                  pl.BlockSpec((1, C_WIDTH), lambda i: (0, 0)),
                  pl.BlockSpec((C_GROUPS, C_CHUNK, C_CHUNK), lambda i: (0, 0, 0)),
                  pl.BlockSpec((C_CHUNK, C_GROUPS), lambda i: (0, 0))],
        out_specs=pl.BlockSpec((rows, C_WIDTH), lambda i: (i, 0)),
        out_shape=jax.ShapeDtypeStruct((t, C_WIDTH), BF16),
        compiler_params=_params(("arbitrary",)),
        name="spatial_gate",
    )(z, z, v_norm.reshape(1, -1), w_s, b_s.T)


def _pack_bf16_pair(lo, hi):
    lo_bits = lax.bitcast_convert_type(lo.astype(BF16).astype(F32), jnp.uint32)
    hi_bits = lax.bitcast_convert_type(hi.astype(BF16).astype(F32), jnp.uint32)
    return (lo_bits >> 16) | (hi_bits & jnp.uint32(0xFFFF0000))


def _unpack_bf16_pair(words):
    lo = lax.bitcast_convert_type(words << 16, F32).astype(BF16)
    hi = lax.bitcast_convert_type(words & jnp.uint32(0xFFFF0000), F32).astype(BF16)
    return lo, hi


def _xattn_kernel(x_ref, gpre_ref, wq_ref, k_ref, v_ref, wo_ref, gpost_ref, gnext_ref, xo_ref, hn_ref,
                  *, pack_next):
    x = x_ref[...]
    hn = _rms(x, gpre_ref[...]).astype(BF16)
    q = (_dot(hn, wq_ref[...]) * float(X_HEAD_DIM ** -0.5)).astype(BF16)
    k = k_ref[...]
    v = v_ref[...]
    outs = []
    for h in range(X_HEADS):
        cols = slice(h * X_HEAD_DIM, (h + 1) * X_HEAD_DIM)
        s = _dot_nt(q[:, cols], k[:, cols])
        p = jnp.exp(s - jnp.max(s, axis=-1, keepdims=True))
        o = _dot(p.astype(BF16), v[:, cols]) / jnp.sum(p, axis=-1, keepdims=True)
        outs.append(o.astype(BF16))
    hh = _dot(jnp.concatenate(outs, axis=1), wo_ref[...])
    xn = x + _rms(hh, gpost_ref[...])
    xo_ref[...] = xn
    hn = _rms(xn, gnext_ref[...])
    if pack_next:
        half = hn.shape[1] // 2
        hn_ref[...] = _pack_bf16_pair(hn[:, :half], hn[:, half:])
    else:
        hn_ref[...] = hn.astype(hn_ref.dtype)


def cross_attention(x, kv, wq, wo, g_pre, g_post, g_next, seq, pack_next=False, tm=256):
    t, d = x.shape
    hd = X_HEADS * X_HEAD_DIM
    nsb = seq // tm
    row = pl.BlockSpec((tm, d), lambda i: (i, 0))
    vec = pl.BlockSpec((1, d), lambda i: (0, 0))
    if pack_next:
        hn_spec = pl.BlockSpec((tm, d // 2), lambda i: (i, 0))
        hn_shape = jax.ShapeDtypeStruct((t, d // 2), jnp.uint32)
    else:
        hn_spec = row
        hn_shape = jax.ShapeDtypeStruct((t, d), BF16)
    return pl.pallas_call(
        functools.partial(_xattn_kernel, pack_next=pack_next),
        grid=(t // tm,),
        in_specs=[row, vec,
                  pl.BlockSpec((d, hd), lambda i: (0, 0)),
                  pl.BlockSpec((N_MEM, hd), lambda i: (i // nsb, 0)),
                  pl.BlockSpec((N_MEM, hd), lambda i: (i // nsb, 1)),
                  pl.BlockSpec((hd, d), lambda i: (0, 0)),
                  vec, vec],
        out_specs=[row, hn_spec],
        out_shape=[jax.ShapeDtypeStruct((t, d), F32), hn_shape],
        compiler_params=_params(("arbitrary",)),
        name="cross_attention",
    )(x, g_pre.reshape(1, d), wq, kv, kv, wo, g_post.reshape(1, d), g_next.reshape(1, d))


def _router_kernel(x_ref, g_ref, wt_ref, idx_ref, gate_ref):
    hn = _rms(x_ref[...], g_ref[...])
    wt = wt_ref[...]
    h_hi = hn.astype(BF16)
    h_lo = (hn - h_hi.astype(F32)).astype(BF16)
    w_hi = wt.astype(BF16)
    w_lo = (wt - w_hi.astype(F32)).astype(BF16)
    logits = _dot_nt(w_hi, h_hi) + (_dot_nt(w_hi, h_lo) + _dot_nt(w_lo, h_hi))
    e = lax.broadcasted_iota(jnp.int32, logits.shape, 0)
    m1 = jnp.max(logits, axis=0, keepdims=True)
    i1 = jnp.min(jnp.where(logits == m1, e, N_EXPERTS), axis=0, keepdims=True)
    rest = jnp.where(e == i1, -jnp.inf, logits)
    m2 = jnp.max(rest, axis=0, keepdims=True)
    i2 = jnp.min(jnp.where(rest == m2, e, N_EXPERTS), axis=0, keepdims=True)
    ex = jnp.exp(m2 - m1)
    g1 = 1.0 / (1.0 + ex)
    idx_ref[...] = jnp.concatenate([i1, i2], axis=0)
    gate_ref[...] = jnp.concatenate([g1, ex * g1], axis=0)


def router(x, g, w_router, tm=512):
    t, d = x.shape
    return pl.pallas_call(
        _router_kernel,
        grid=(t // tm,),
        in_specs=[pl.BlockSpec((tm, d), lambda i: (i, 0)),
                  pl.BlockSpec((1, d), lambda i: (0, 0)),
                  pl.BlockSpec((N_EXPERTS, d), lambda i: (0, 0))],
        out_specs=[pl.BlockSpec((2, tm), lambda i: (0, i)),
                   pl.BlockSpec((2, tm), lambda i: (0, i))],
        out_shape=[jax.ShapeDtypeStruct((2, t), jnp.int32), jax.ShapeDtypeStruct((2, t), F32)],
        compiler_params=_params(("arbitrary",)),
        name="router",
    )(x, g.reshape(1, d), w_router.T)


def _moe_gu_kernel(te_ref, tx_ref, tq_ref, x_ref, wg_ref, wu_ref, o_ref):
    n_sub = tq_ref[pl.program_id(1)]
    tm = o_ref.shape[0]
    half = wg_ref.shape[1] // 2

    for q in range(1, tm // MOE_SUB + 1):
        rows = q * MOE_SUB

        @pl.when(n_sub == q)
        def _(rows=rows):
            x_lo, x_hi = _unpack_bf16_pair(x_ref[:rows, :])
            g = _dot(x_lo, _bf(wg_ref[0, :half, :])) + _dot(x_hi, _bf(wg_ref[0, half:, :]))
            u = _dot(x_lo, _bf(wu_ref[0, :half, :])) + _dot(x_hi, _bf(wu_ref[0, half:, :]))
            o_ref[:rows, :] = (jax.nn.silu(g) * u).astype(o_ref.dtype)
            if rows < tm:
                o_ref[rows:, :] = jnp.zeros((tm - rows, o_ref.shape[1]), o_ref.dtype)

    @pl.when(n_sub == 0)
    def _():
        o_ref[...] = jnp.zeros_like(o_ref)


def moe_gate_up(xs, w_gu, tile_e, tile_x, tile_sub, tn=256):
    p = xs.shape[0]
    d = w_gu.shape[1]
    f = w_gu.shape[2] // 2
    nf = f // tn
    nt = p // MOE_TM
    grid_spec = pltpu.PrefetchScalarGridSpec(
        num_scalar_prefetch=3,
        grid=(nf, nt),
        in_specs=[pl.BlockSpec((MOE_TM, d // 2), lambda j, t, te, tx, ts: (tx[t], 0)),
                  pl.BlockSpec((1, d, tn), lambda j, t, te, tx, ts: (te[t], 0, j)),
                  pl.BlockSpec((1, d, tn), lambda j, t, te, tx, ts: (te[t], 0, j + nf))],
        out_specs=pl.BlockSpec((MOE_TM, tn), lambda j, t, te, tx, ts: (t, j)),
    )
    return pl.pallas_call(
        _moe_gu_kernel,
        grid_spec=grid_spec,
        out_shape=jax.ShapeDtypeStruct((p, f), BF16),
        compiler_params=_params(("arbitrary", "arbitrary")),
        name="moe_gate_up",
    )(tile_e, tile_x, tile_sub, xs, w_gu, w_gu)


def _moe_down_kernel(te_ref, tx_ref, tq_ref, a_ref, w_ref, o_ref):
    n_sub = tq_ref[pl.program_id(1)]
    tm = o_ref.shape[0]

    for q in range(1, tm // MOE_SUB + 1):
        rows = q * MOE_SUB

        @pl.when(n_sub == q)
        def _(rows=rows):
            o_ref[:rows, :] = _dot(a_ref[:rows, :], _bf(w_ref[0]))
            if rows < tm:
                o_ref[rows:, :] = jnp.zeros((tm - rows, o_ref.shape[1]), o_ref.dtype)

    @pl.when(n_sub == 0)
    def _():
        o_ref[...] = jnp.zeros_like(o_ref)


def moe_down(act, w_down, tile_e, tile_x, tile_sub, tn=512):
    p, f = act.shape
    d = w_down.shape[2]
    nt = p // MOE_TM
    grid_spec = pltpu.PrefetchScalarGridSpec(
        num_scalar_prefetch=3,
        grid=(d // tn, nt),
        in_specs=[pl.BlockSpec((MOE_TM, f), lambda j, t, te, tx, ts: (tx[t], 0)),
                  pl.BlockSpec((1, f, tn), lambda j, t, te, tx, ts: (te[t], 0, j))],
        out_specs=pl.BlockSpec((MOE_TM, tn), lambda j, t, te, tx, ts: (t, j)),
    )
    return pl.pallas_call(
        _moe_down_kernel,
        grid_spec=grid_spec,
        out_shape=jax.ShapeDtypeStruct((p, d), F32),
        compiler_params=_params(("arbitrary", "arbitrary")),
        name="moe_down",
    )(tile_e, tile_x, tile_sub, act, w_down)


def moe_layer(x, g_pre, hn_packed, w_router, w_gu, w_down):
    t = x.shape[0]
    idx, gates = router(x, g_pre, w_router)
    experts = jnp.arange(N_EXPERTS, dtype=jnp.int32)
    e_flat = idx.reshape(-1)
    csum = jnp.cumsum((e_flat[:, None] == experts[None, :]).astype(jnp.int32), axis=0)
    rank = jnp.take_along_axis(csum, e_flat[:, None], axis=1)[:, 0] - 1
    count = csum[-1]
    n_tiles = (count + MOE_TM - 1) // MOE_TM
    tile_end = jnp.cumsum(n_tiles)
    tile_start = tile_end - n_tiles
    pos = tile_start[e_flat] * MOE_TM + rank
    nt = 2 * t // MOE_TM + N_EXPERTS
    p = nt * MOE_TM
    row_tok = jnp.zeros((p,), jnp.int32).at[pos].set(jnp.tile(jnp.arange(t, dtype=jnp.int32), 2))
    n_used = tile_end[-1]
    tile_id = jnp.arange(nt, dtype=jnp.int32)
    tile_x = jnp.minimum(tile_id, n_used - 1)
    tile_e = jnp.sum((tile_end[None, :] <= tile_x[:, None]).astype(jnp.int32), axis=1)
    tile_e = jnp.minimum(tile_e, N_EXPERTS - 1)
    rows_left = count[tile_e] - (tile_x - tile_start[tile_e]) * MOE_TM
    tile_sub = (jnp.clip(rows_left, 0, MOE_TM) + MOE_SUB - 1) // MOE_SUB
    tile_sub = jnp.where(tile_id < n_used, tile_sub, 0).astype(jnp.int32)
    xs = hn_packed.at[row_tok].get(mode="promise_in_bounds")
    act = moe_gate_up(xs, w_gu, tile_e, tile_x, tile_sub)
    y = moe_down(act, w_down, tile_e, tile_x, tile_sub)
    y0 = y.at[pos[:t]].get(mode="promise_in_bounds")
    y1 = y.at[pos[t:]].get(mode="promise_in_bounds")
    return y0, y1, gates.T


def _rope_tables(positions):
    pos = positions.reshape(-1).astype(F32)
    inv_a = ROPE_BASE ** (-jnp.arange(0, A_ROPE, 2, dtype=F32) / A_ROPE)
    ang_a = pos[:, None] * inv_a
    cos_a, sin_a = jnp.cos(ang_a), jnp.sin(ang_a)
    zero = jnp.zeros_like(cos_a)
    tab64 = jnp.concatenate([cos_a, cos_a, -sin_a, sin_a], axis=1)
    tab_kc = jnp.concatenate([cos_a, cos_a, zero, zero], axis=1)
    tab_ks = jnp.concatenate([-sin_a, sin_a, zero, zero], axis=1)
    inv_b = ROPE_BASE ** (-jnp.arange(0, B_QK, 2, dtype=F32) / B_QK)
    ang_b = pos[:, None] * inv_b
    return tab64, tab_kc, tab_ks, jnp.cos(ang_b), jnp.sin(ang_b)


def _swap_halves(w):
    half = w.shape[-1] // 2
    return jnp.concatenate([w[..., half:], w[..., :half]], axis=-1)


def kernel(x, mem, positions, norm_mix_pre, norm_mix_post, norm_x_pre, norm_x_post, norm_ffn_pre, norm_ffn_post, norm_mem, x_w_q, x_w_kv, x_w_o, ab_w_in, ab_q_norm, ab_w_uq, ab_kv_norm, ab_w_ukv, ab_ret_norm, ab_w_out, ffn_w_gu, ffn_w_down, c_w_in, c_v_norm, c_w_s, c_b_s, c_w_out, moe_router, moe_w_gu, moe_w_down):
    batch, seq, d = x.shape
    t = batch * seq
    x = x.reshape(t, d)
    mem2 = mem.reshape(batch * N_MEM, d)
    tab64, tab_kc, tab_ks, cos_b, sin_b = _rope_tables(positions)

    w_uq = ab_w_uq[0].reshape(A_Q_RANK, A_HEADS, A_NOPE + A_ROPE)
    w_pe = w_uq[:, :, A_NOPE:]
    wq_r = jnp.concatenate([w_uq[:, :, :A_NOPE], w_pe, _swap_halves(w_pe)], axis=-1)
    wq_r = wq_r.transpose(1, 0, 2).astype(BF16)

    hn = rms_norm_bf16(x, norm_mix_pre[0])
    z = ab_in_proj(hn, ab_w_in[0])
    cqn, ckvn, kpe = a_norm(z, hn, ab_w_in[0], tab_kc, tab_ks, ab_q_norm[0], ab_kv_norm[0])
    q = q_proj(cqn, wq_r, tab64, batch, seq)
    k, v = kv_proj(ckvn, ab_w_ukv[0], kpe, batch, seq)
    ya = causal_attention(q, k, v)
    yb = retention(z, cos_b, sin_b, ab_ret_norm[0], batch, seq)
    h = matmul_cat(ya, yb, ab_w_out[0], F32, tm=1024, tn=512, name="ab_out_proj")
    x = resid_norm([h], x, norm_mix_post[0])
    memn = rms_norm_bf16(mem2, norm_mem[0])
    kv = matmul(memn, x_w_kv[0], BF16, tm=512, tn=512, name="mem_kv_proj")
    x, hn = cross_attention(x, kv, x_w_q[0].astype(BF16), x_w_o[0].astype(BF16),
                            norm_x_pre[0], norm_x_post[0], norm_ffn_pre[0], seq)
    act = matmul_swiglu(hn, ffn_w_gu[0], tm=1024, tn=256)
    h = matmul_ksplit(act, ffn_w_down[0].astype(BF16), tm=1024, tn=2048, tk=512)
    x, hn = resid_norm([h], x, norm_ffn_post[0], norm_mix_pre[1])

    zc = matmul(hn, c_w_in[0], BF16, tm=1024, tn=512, act="gelu", name="c_in_proj")
    yc = spatial_gate(zc, c_v_norm[0], c_w_s[0], c_b_s[0])
    h = matmul(yc, c_w_out[0], F32, tm=1024, tn=512, name="c_out_proj")
    x = resid_norm([h], x, norm_mix_post[1])
    memn = rms_norm_bf16(mem2, norm_mem[1])
    kv = matmul(memn, x_w_kv[1], BF16, tm=512, tn=512, name="mem_kv_proj")
    x, hn_packed = cross_attention(x, kv, x_w_q[1].astype(BF16), x_w_o[1].astype(BF16),
                                   norm_x_pre[1], norm_x_post[1], norm_ffn_pre[1], seq, pack_next=True)
    y0, y1, gates = moe_layer(x, norm_ffn_pre[1], hn_packed, moe_router[0], moe_w_gu[0], moe_w_down[0])
    x = resid_norm([y0, y1], x, norm_ffn_post[1], weights=gates)
    return x.reshape(batch, seq, d)
```

```python
import functools

import jax
import jax.numpy as jnp
from jax import lax
from jax.experimental import pallas as pl
from jax.experimental.pallas import tpu as pltpu

F32 = jnp.float32
BF16 = jnp.bfloat16

D_MODEL = 4096
N_MEM = 256
EPS = 1e-6
ROPE_BASE = 10000.0
A_HEADS = 16
A_Q_RANK = 1024
A_KV_RANK = 512
A_NOPE = 128
A_ROPE = 64
A_V = 128
B_HEADS = 8
B_QK = 256
B_V = 256
RET_CHUNK = 128
C_GROUPS = 8
C_WIDTH = 4096
C_CHUNK = 128
X_HEADS = 4
X_HEAD_DIM = 128
D_FF = 14336
N_EXPERTS = 8
D_EXPERT = 3584

LANES = 128
VMEM_LIMIT = 56 * 1024 * 1024
NEG_BIG = -1e30

Z_CQ = 0
Z_CKV = A_Q_RANK
Z_RQ = A_Q_RANK + A_KV_RANK
Z_RK = Z_RQ + B_HEADS * B_QK
Z_RV = Z_RK + B_HEADS * B_QK
Z_RG = Z_RV + B_HEADS * B_V
Z_W = Z_RG + B_HEADS * B_V
A_IN = A_Q_RANK + A_KV_RANK + A_ROPE

MOE_TM = 1024
MOE_SUB = 256


def _params(sem, vmem=VMEM_LIMIT):
    return pltpu.CompilerParams(dimension_semantics=sem, vmem_limit_bytes=vmem)


def _rms(xf, g):
    ms = jnp.mean(xf * xf, axis=-1, keepdims=True)
    return xf * lax.rsqrt(ms + EPS) * g


def _bf(w):
    return w if w.dtype == BF16 else w.astype(BF16)


def _dot(a, b):
    return jnp.dot(a, b, preferred_element_type=F32)


def _dot_nt(a, b):
    return lax.dot_general(a, b, (((1,), (1,)), ((), ())), preferred_element_type=F32)


def _dot_tn(a, b):
    return lax.dot_general(a, b, (((0,), (0,)), ((), ())), preferred_element_type=F32)


def _norm_kernel(x_ref, g_ref, o_ref):
    o_ref[...] = _rms(x_ref[...].astype(F32), g_ref[...]).astype(o_ref.dtype)


def rms_norm_bf16(x, g, tm=512):
    m, d = x.shape
    return pl.pallas_call(
        _norm_kernel,
        grid=(m // tm,),
        in_specs=[pl.BlockSpec((tm, d), lambda i: (i, 0)),
                  pl.BlockSpec((1, d), lambda i: (0, 0))],
        out_specs=pl.BlockSpec((tm, d), lambda i: (i, 0)),
        out_shape=jax.ShapeDtypeStruct((m, d), BF16),
        compiler_params=_params(("arbitrary",)),
        name="rms_norm",
    )(x, g.reshape(1, d))


def _resid_kernel(*refs, n_h, gated, with_next):
    h_refs, rest = refs[:n_h], refs[n_h:]
    if gated:
        w = rest[0][...]
        rest = rest[1:]
        h = h_refs[0][...] * w[:, 0:1]
        for c, r in enumerate(h_refs[1:], start=1):
            h = h + r[...] * w[:, c:c + 1]
    else:
        h = h_refs[0][...]
        for r in h_refs[1:]:
            h = h + r[...]
    x_ref, g_ref = rest[0], rest[1]
    xn = x_ref[...] + _rms(h, g_ref[...])
    if with_next:
        gn_ref, xo_ref, hn_ref = rest[2:]
        xo_ref[...] = xn
        hn_ref[...] = _rms(xn, gn_ref[...]).astype(hn_ref.dtype)
    else:
        xo_ref = rest[2]
        xo_ref[...] = xn


def resid_norm(hs, x, g_post, g_next=None, weights=None, tm=256):
    m, d = x.shape
    row = pl.BlockSpec((tm, d), lambda i: (i, 0))
    vec = pl.BlockSpec((1, d), lambda i: (0, 0))
    with_next = g_next is not None
    gated = weights is not None
    in_specs = [row] * len(hs) + ([pl.BlockSpec((tm, len(hs)), lambda i: (i, 0))] if gated else [])
    in_specs += [row, vec] + ([vec] if with_next else [])
    args = list(hs) + ([weights] if gated else [])
    args += [x, g_post.reshape(1, d)] + ([g_next.reshape(1, d)] if with_next else [])
    out_shape = [jax.ShapeDtypeStruct((m, d), F32)]
    out_specs = [row]
    if with_next:
        out_shape.append(jax.ShapeDtypeStruct((m, d), BF16))
        out_specs.append(row)
    res = pl.pallas_call(
        functools.partial(_resid_kernel, n_h=len(hs), gated=gated, with_next=with_next),
        grid=(m // tm,),
        in_specs=in_specs,
        out_specs=out_specs,
        out_shape=out_shape,
        compiler_params=_params(("arbitrary",)),
        name="resid_norm",
    )(*args)
    return res if with_next else res[0]


def _mm_kernel(x_ref, w_ref, o_ref, *, act):
    acc = _dot(x_ref[...], _bf(w_ref[...]))
    if act == "gelu":
        acc = jax.nn.gelu(acc)
    o_ref[...] = acc.astype(o_ref.dtype)


def matmul(x, w, out_dtype, tm, tn, act=None, name="matmul"):
    m, k = x.shape
    n = w.shape[1]
    return pl.pallas_call(
        functools.partial(_mm_kernel, act=act),
        grid=(n // tn, m // tm),
        in_specs=[pl.BlockSpec((tm, k), lambda j, i: (i, 0)),
                  pl.BlockSpec((k, tn), lambda j, i: (0, j))],
        out_specs=pl.BlockSpec((tm, tn), lambda j, i: (i, j)),
        out_shape=jax.ShapeDtypeStruct((m, n), out_dtype),
        compiler_params=_params(("arbitrary", "arbitrary")),
        name=name,
    )(x, w)


def _mm2_kernel(xa_ref, xb_ref, w_ref, o_ref):
    ka = xa_ref.shape[1]
    acc = _dot(xa_ref[...], _bf(w_ref[:ka, :]))
    acc = acc + _dot(xb_ref[...], _bf(w_ref[ka:, :]))
    o_ref[...] = acc.astype(o_ref.dtype)


def matmul_cat(xa, xb, w, out_dtype, tm, tn, name="matmul_cat"):
    m, ka = xa.shape
    kb = xb.shape[1]
    n = w.shape[1]
    return pl.pallas_call(
        _mm2_kernel,
        grid=(n // tn, m // tm),
        in_specs=[pl.BlockSpec((tm, ka), lambda j, i: (i, 0)),
                  pl.BlockSpec((tm, kb), lambda j, i: (i, 0)),
                  pl.BlockSpec((ka + kb, tn), lambda j, i: (0, j))],
        out_specs=pl.BlockSpec((tm, tn), lambda j, i: (i, j)),
        out_shape=jax.ShapeDtypeStruct((m, n), out_dtype),
        compiler_params=_params(("arbitrary", "arbitrary")),
        name=name,
    )(xa, xb, w)


def _swiglu_kernel(x_ref, wg_ref, wu_ref, o_ref):
    x = x_ref[...]
    g = _dot(x, _bf(wg_ref[...]))
    u = _dot(x, _bf(wu_ref[...]))
    o_ref[...] = (jax.nn.silu(g) * u).astype(o_ref.dtype)


def matmul_swiglu(x, w_gu, tm, tn, name="ffn_gate_up"):
    m, k = x.shape
    f = w_gu.shape[1] // 2
    nf = f // tn
    return pl.pallas_call(
        _swiglu_kernel,
        grid=(nf, m // tm),
        in_specs=[pl.BlockSpec((tm, k), lambda j, i: (i, 0)),
                  pl.BlockSpec((k, tn), lambda j, i: (0, j)),
                  pl.BlockSpec((k, tn), lambda j, i: (0, j + nf))],
        out_specs=pl.BlockSpec((tm, tn), lambda j, i: (i, j)),
        out_shape=jax.ShapeDtypeStruct((m, f), BF16),
        compiler_params=_params(("arbitrary", "arbitrary")),
        name=name,
    )(x, w_gu, w_gu)


def _mm_acc_kernel(x_ref, w_ref, o_ref):
    @pl.when(pl.program_id(2) == 0)
    def _():
        o_ref[...] = jnp.zeros_like(o_ref)

    o_ref[...] += _dot(x_ref[...], _bf(w_ref[...]))


def matmul_ksplit(x, w, tm, tn, tk, name="ffn_down"):
    m, k = x.shape
    n = w.shape[1]
    return pl.pallas_call(
        _mm_acc_kernel,
        grid=(m // tm, n // tn, k // tk),
        in_specs=[pl.BlockSpec((tm, tk), lambda i, j, kk: (i, kk)),
                  pl.BlockSpec((tk, tn), lambda i, j, kk: (kk, j))],
        out_specs=pl.BlockSpec((tm, tn), lambda i, j, kk: (i, j)),
        out_shape=jax.ShapeDtypeStruct((m, n), F32),
        compiler_params=_params(("arbitrary", "arbitrary", "arbitrary")),
        name=name,
    )(x, w)


def _rope64(pe_pair, tab):
    p = pe_pair * tab
    return p + pltpu.roll(p, A_ROPE, axis=1)


def _ab_in_kernel(x_ref, wa_ref, wb_ref, o_ref, w_sc, *, n_plain, shift, rows):
    j = pl.program_id(0)
    first = pl.program_id(1) == 0

    @pl.when(jnp.logical_and(first, j < n_plain))
    def _():
        w_sc[...] = wa_ref[...].astype(BF16)

    @pl.when(jnp.logical_and(first, j >= n_plain))
    def _():
        for r in range(0, wa_ref.shape[0], rows):
            wa = wa_ref[r:r + rows, :]
            wb = wb_ref[r:r + rows, :]
            w_sc[r:r + rows, :] = jnp.concatenate([wa[:, shift:], wb[:, :shift]], axis=1).astype(BF16)

    o_ref[...] = _dot(x_ref[...], w_sc[...]).astype(o_ref.dtype)


def ab_in_proj(x, w_in, tm=1024, tn=512):
    m, k = x.shape
    plain = A_Q_RANK + A_KV_RANK
    shift = A_IN - plain
    assert plain % tn == 0 and tn % LANES == 0 and 0 < shift < LANES and Z_W % tn == 0
    lanes_per_tile = tn // LANES
    return pl.pallas_call(
        functools.partial(_ab_in_kernel, n_plain=plain // tn, shift=shift, rows=256),
        grid=(Z_W // tn, m // tm),
        in_specs=[pl.BlockSpec((tm, k), lambda j, i: (i, 0)),
                  pl.BlockSpec((None, k, tn), lambda j, i: (0, 0, j)),
                  pl.BlockSpec((None, k, LANES), lambda j, i: (0, 0, (j + 1) * lanes_per_tile))],
        out_specs=pl.BlockSpec((tm, tn), lambda j, i: (i, j)),
        out_shape=jax.ShapeDtypeStruct((m, Z_W), BF16),
        scratch_shapes=[pltpu.VMEM((k, tn), BF16)],
        compiler_params=_params(("arbitrary", "arbitrary")),
        name="ab_in_proj",
    )(x, w_in, w_in)


def _a_norm_kernel(cq_ref, ckv_ref, x_ref, wkr_ref, tabc_ref, tabs_ref, qn_ref, kvn_ref,
                   cqn_ref, ckvn_ref, kpe_ref):
    cqn_ref[...] = _rms(cq_ref[...].astype(F32), qn_ref[...]).astype(BF16)
    ckvn_ref[...] = _rms(ckv_ref[...].astype(F32), kvn_ref[...]).astype(BF16)
    kr = _dot(x_ref[...], _bf(wkr_ref[...]))
    lane = lax.broadcasted_iota(jnp.int32, kr.shape, 1)
    half = A_ROPE // 2
    swapped = jnp.where(lane < half, pltpu.roll(kr, LANES - half, axis=1), pltpu.roll(kr, half, axis=1))
    kpe_ref[...] = (kr * tabc_ref[...] + swapped * tabs_ref[...]).astype(BF16)


def a_norm(z, hn, w_in, tab_kc, tab_ks, q_norm, kv_norm, tm=512):
    t, d = hn.shape
    kr_block = (A_Q_RANK + A_KV_RANK) // LANES
    return pl.pallas_call(
        _a_norm_kernel,
        grid=(t // tm,),
        in_specs=[pl.BlockSpec((tm, A_Q_RANK), lambda i: (i, Z_CQ // A_Q_RANK)),
                  pl.BlockSpec((tm, A_KV_RANK), lambda i: (i, Z_CKV // A_KV_RANK)),
                  pl.BlockSpec((tm, d), lambda i: (i, 0)),
                  pl.BlockSpec((None, d, LANES), lambda i: (0, 0, kr_block)),
                  pl.BlockSpec((tm, LANES), lambda i: (i, 0)),
                  pl.BlockSpec((tm, LANES), lambda i: (i, 0)),
                  pl.BlockSpec((1, A_Q_RANK), lambda i: (0, 0)),
                  pl.BlockSpec((1, A_KV_RANK), lambda i: (0, 0))],
        out_specs=[pl.BlockSpec((tm, A_Q_RANK), lambda i: (i, 0)),
                   pl.BlockSpec((tm, A_KV_RANK), lambda i: (i, 0)),
                   pl.BlockSpec((tm, LANES), lambda i: (i, 0))],
        out_shape=[jax.ShapeDtypeStruct((t, A_Q_RANK), BF16),
                   jax.ShapeDtypeStruct((t, A_KV_RANK), BF16),
                   jax.ShapeDtypeStruct((t, LANES), BF16)],
        compiler_params=_params(("arbitrary",)),
        name="a_norm",
    )(z, z, hn, w_in, tab_kc, tab_ks, q_norm.reshape(1, -1), kv_norm.reshape(1, -1))


def _q_proj_kernel(x_ref, w_ref, tab_ref, o_ref, *, scale):
    acc = _dot(x_ref[...], w_ref[0])
    nope = acc[:, :A_NOPE] * scale
    pe = _rope64(acc[:, A_NOPE:], tab_ref[...]) * scale
    o_ref[0, 0] = jnp.concatenate([nope, pe], axis=1).astype(o_ref.dtype)


def q_proj(cqn, wq_r, tab64, batch, seq, tm=1024):
    nsb = seq // tm
    scale = float((A_NOPE + A_ROPE) ** -0.5)
    return pl.pallas_call(
        functools.partial(_q_proj_kernel, scale=scale),
        grid=(A_HEADS, batch * nsb),
        in_specs=[pl.BlockSpec((tm, A_Q_RANK), lambda h, i: (i, 0)),
                  pl.BlockSpec((1, A_Q_RANK, 2 * LANES), lambda h, i: (h, 0, 0)),
                  pl.BlockSpec((tm, LANES), lambda h, i: (i, 0))],
        out_specs=pl.BlockSpec((1, 1, tm, 2 * LANES), lambda h, i: (i // nsb, h, i % nsb, 0)),
        out_shape=jax.ShapeDtypeStruct((batch, A_HEADS, seq, 2 * LANES), BF16),
        compiler_params=_params(("arbitrary", "arbitrary")),
        name="q_proj",
    )(cqn, wq_r, tab64)


def _kv_proj_kernel(x_ref, w_ref, kpe_ref, k_ref, v_ref):
    acc = _dot(x_ref[...], _bf(w_ref[...]))
    k_ref[0, 0] = jnp.concatenate([acc[:, :A_NOPE].astype(BF16), kpe_ref[...]], axis=1)
    v_ref[0, 0] = acc[:, A_NOPE:].astype(BF16)


def kv_proj(ckvn, w_ukv, kpe, batch, seq, tm=1024):
    nsb = seq // tm
    hw = A_NOPE + A_V
    return pl.pallas_call(
        _kv_proj_kernel,
        grid=(A_HEADS, batch * nsb),
        in_specs=[pl.BlockSpec((tm, A_KV_RANK), lambda h, i: (i, 0)),
                  pl.BlockSpec((A_KV_RANK, hw), lambda h, i: (0, h)),
                  pl.BlockSpec((tm, LANES), lambda h, i: (i, 0))],
        out_specs=[pl.BlockSpec((1, 1, tm, 2 * LANES), lambda h, i: (i // nsb, h, i % nsb, 0)),
                   pl.BlockSpec((1, 1, tm, A_V), lambda h, i: (i // nsb, h, i % nsb, 0))],
        out_shape=[jax.ShapeDtypeStruct((batch, A_HEADS, seq, 2 * LANES), BF16),
                   jax.ShapeDtypeStruct((batch, A_HEADS, seq, A_V), BF16)],
        compiler_params=_params(("arbitrary", "arbitrary")),
        name="kv_proj",
    )(ckvn, w_ukv, kpe)


def _flash_kernel(q_ref, k_ref, v_ref, o_ref, *, tq, tk):
    qi = pl.program_id(2)
    q = q_ref[0, 0]
    dv = v_ref.shape[-1]

    def step(off, width, carry, masked):
        m, l, acc = carry
        s = _dot_nt(q, k_ref[0, 0, pl.ds(off, width), :])
        if masked:
            row = lax.broadcasted_iota(jnp.int32, s.shape, 0)
            col = lax.broadcasted_iota(jnp.int32, s.shape, 1)
            s = jnp.where(col <= row, s, NEG_BIG)
        m_new = jnp.maximum(m, jnp.max(s, axis=-1, keepdims=True))
        alpha = jnp.exp(m - m_new)
        p = jnp.exp(s - m_new)
        l = alpha * l + jnp.sum(p, axis=-1, keepdims=True)
        acc = alpha * acc + _dot(p.astype(BF16), v_ref[0, 0, pl.ds(off, width), :])
        return m_new, l, acc

    def body(j, carry):
        return step(pl.multiple_of(j * tk, tk), tk, carry, False)

    carry = (jnp.full((tq, 1), NEG_BIG, F32), jnp.zeros((tq, 1), F32), jnp.zeros((tq, dv), F32))
    n_wide = (qi * tq) // tk
    carry = lax.fori_loop(0, n_wide, body, carry)
    if tk > tq:
        carry = lax.cond(qi * tq > n_wide * tk,
                         lambda c: step(pl.multiple_of(n_wide * tk, tq), tq, c, False),
                         lambda c: c, carry)
    _, l, acc = step(pl.multiple_of(qi * tq, tq), tq, carry, True)
    o_ref[...] = (acc / l).astype(o_ref.dtype)


def causal_attention(q, k, v, tq=512, tk=1024):
    b, h, s, dk = q.shape
    dv = v.shape[-1]
    nq = s // tq
    assert tk in (tq, 2 * tq) and s % tk == 0
    return pl.pallas_call(
        functools.partial(_flash_kernel, tq=tq, tk=tk),
        grid=(b, h, nq),
        in_specs=[pl.BlockSpec((1, 1, tq, dk), lambda bi, hi, qi: (bi, hi, qi, 0)),
                  pl.BlockSpec((1, 1, s, dk), lambda bi, hi, qi: (bi, hi, 0, 0)),
                  pl.BlockSpec((1, 1, s, dv), lambda bi, hi, qi: (bi, hi, 0, 0))],
        out_specs=pl.BlockSpec((tq, dv), lambda bi, hi, qi: (bi * nq + qi, hi)),
        out_shape=jax.ShapeDtypeStruct((b * s, h * dv), BF16),
        compiler_params=_params(("arbitrary", "arbitrary", "arbitrary")),
        name="causal_attention",
    )(q, k, v)


def _rope_half(x, cos, sin):
    half = x.shape[1] // 2
    x1, x2 = x[:, :half], x[:, half:]
    return jnp.concatenate([x1 * cos - x2 * sin, x1 * sin + x2 * cos], axis=1)


def _retention_kernel(cdec_ref, q_ref, k_ref, v_ref, g_ref, cos_ref, sin_ref, intra_ref, qdec_ref,
                      kdec_ref, rn_ref, o_ref, state_sc, *, n_chunks):
    hi = pl.program_id(1)

    @pl.when(pl.program_id(2) == 0)
    def _():
        state_sc[...] = jnp.zeros_like(state_sc)

    chunk_decay = cdec_ref[hi]
    intra = intra_ref[0]
    q_decay = qdec_ref[0]
    k_decay = kdec_ref[0]
    rn = rn_ref[...]
    k_scale = float(B_QK ** -0.5)
    for c in range(n_chunks):
        sl = slice(c * RET_CHUNK, (c + 1) * RET_CHUNK)
        cos = cos_ref[sl, :]
        sin = sin_ref[sl, :]
        qr = _rope_half(q_ref[sl, :].astype(F32), cos, sin)
        kr = _rope_half(k_ref[sl, :].astype(F32), cos, sin) * k_scale
        v = v_ref[sl, :]
        qb = qr.astype(BF16)
        s = _dot_nt(qb, kr.astype(BF16)) * intra
        inner = _dot(s.astype(BF16), v)
        state = state_sc[...]
        cross = _dot(qb, state.astype(BF16)) * q_decay
        state_sc[...] = state * chunk_decay + _dot_tn((kr * k_decay).astype(BF16), v)
        y = inner + cross
        mu = jnp.mean(y, axis=-1, keepdims=True)
        yc = y - mu
        var = jnp.mean(yc * yc, axis=-1, keepdims=True)
        yn = yc * lax.rsqrt(var + EPS) * rn
        o_ref[sl, :] = (yn * jax.nn.silu(g_ref[sl, :].astype(F32))).astype(o_ref.dtype)


def retention(z, cos, sin, ret_norm, batch, seq, rows=1024):
    t = z.shape[0]
    nr = seq // rows
    hh = jnp.arange(B_HEADS, dtype=F32)
    log_g = jnp.log1p(-jnp.exp2(-5.0 - hh))
    idx = jnp.arange(RET_CHUNK, dtype=F32)
    diff = idx[:, None] - idx[None, :]
    intra = jnp.where(diff >= 0, jnp.exp(log_g[:, None, None] * jnp.maximum(diff, 0.0)), 0.0)
    ones = jnp.ones((1, 1, B_QK), F32)
    q_decay = jnp.exp(log_g[:, None] * (idx + 1.0))[:, :, None] * ones
    k_decay = jnp.exp(log_g[:, None] * (RET_CHUNK - 1.0 - idx))[:, :, None] * ones
    chunk_decay = jnp.exp(log_g * RET_CHUNK)

    def zspec(off):
        return pl.BlockSpec((rows, B_QK), lambda b, h, r, cd: (b * nr + r, off // B_QK + h))

    tab = pl.BlockSpec((rows, B_QK // 2), lambda b, h, r, cd: (b * nr + r, 0))
    per_head = pl.BlockSpec((1, RET_CHUNK, B_QK), lambda b, h, r, cd: (h, 0, 0))
    grid_spec = pltpu.PrefetchScalarGridSpec(
        num_scalar_prefetch=1,
        grid=(batch, B_HEADS, nr),
        in_specs=[zspec(Z_RQ), zspec(Z_RK), zspec(Z_RV), zspec(Z_RG), tab, tab,
                  pl.BlockSpec((1, RET_CHUNK, RET_CHUNK), lambda b, h, r, cd: (h, 0, 0)),
                  per_head, per_head,
                  pl.BlockSpec((1, B_V), lambda b, h, r, cd: (0, h))],
        out_specs=pl.BlockSpec((rows, B_V), lambda b, h, r, cd: (b * nr + r, h)),
        scratch_shapes=[pltpu.VMEM((B_QK, B_V), F32)],
    )
    return pl.pallas_call(
        functools.partial(_retention_kernel, n_chunks=rows // RET_CHUNK),
        grid_spec=grid_spec,
        out_shape=jax.ShapeDtypeStruct((t, B_HEADS * B_V), BF16),
        compiler_params=_params(("arbitrary", "arbitrary", "arbitrary")),
        name="retention",
    )(chunk_decay, z, z, z, z, cos, sin, intra, q_decay, k_decay, ret_norm.reshape(1, -1))


def _sgu_kernel(u_ref, v_ref, vn_ref, ws_ref, bs_ref, o_ref, *, n_chunks):
    gw = C_WIDTH // C_GROUPS
    v = v_ref[...].astype(F32)
    mu = jnp.mean(v, axis=-1, keepdims=True)
    vc = v - mu
    var = jnp.mean(vc * vc, axis=-1, keepdims=True)
    vn = (vc * lax.rsqrt(var + EPS) * vn_ref[...]).astype(BF16)
    r = lax.broadcasted_iota(jnp.int32, (C_CHUNK, C_CHUNK), 0)
    c = lax.broadcasted_iota(jnp.int32, (C_CHUNK, C_CHUNK), 1)
    bs = bs_ref[...]
    for g in range(C_GROUPS):
        ws = jnp.where(c <= r, ws_ref[g], 0.0).astype(BF16)
        bias = bs[:, g:g + 1]
        cols = slice(g * gw, (g + 1) * gw)
        for ch in range(n_chunks):
            rows = slice(ch * C_CHUNK, (ch + 1) * C_CHUNK)
            mixed = _dot(ws, vn[rows, cols]) + bias
            o_ref[rows, cols] = (u_ref[rows, cols].astype(F32) * mixed).astype(o_ref.dtype)


def spatial_gate(z, v_norm, w_s, b_s, rows=512):
    t = z.shape[0]
    return pl.pallas_call(
        functools.partial(_sgu_kernel, n_chunks=rows // C_CHUNK),
        grid=(t // rows,),
        in_specs=[pl.BlockSpec((rows, C_WIDTH), lambda i: (i, 0)),
                  pl.BlockSpec((rows, C_WIDTH), lambda i: (i, 1)),
                  pl.BlockSpec((1, C_WIDTH), lambda i: (0, 0)),
                  pl.BlockSpec((C_GROUPS, C_CHUNK, C_CHUNK), lambda i: (0, 0, 0)),
                  pl.BlockSpec((C_CHUNK, C_GROUPS), lambda i: (0, 0))],
        out_specs=pl.BlockSpec((rows, C_WIDTH), lambda i: (i, 0)),
        out_shape=jax.ShapeDtypeStruct((t, C_WIDTH), BF16),
        compiler_params=_params(("arbitrary",)),
        name="spatial_gate",
    )(z, z, v_norm.reshape(1, -1), w_s, b_s.T)


def _pack_bf16_pair(lo, hi):
    lo_bits = lax.bitcast_convert_type(lo.astype(BF16).astype(F32), jnp.uint32)
    hi_bits = lax.bitcast_convert_type(hi.astype(BF16).astype(F32), jnp.uint32)
    return (lo_bits >> 16) | (hi_bits & jnp.uint32(0xFFFF0000))


def _unpack_bf16_pair(words):
    lo = lax.bitcast_convert_type(words << 16, F32).astype(BF16)
    hi = lax.bitcast_convert_type(words & jnp.uint32(0xFFFF0000), F32).astype(BF16)
    return lo, hi


def _xattn_kernel(x_ref, gpre_ref, wq_ref, k_ref, v_ref, wo_ref, gpost_ref, gnext_ref, xo_ref, hn_ref,
                  *, pack_next):
    x = x_ref[...]
    hn = _rms(x, gpre_ref[...]).astype(BF16)
    q = (_dot(hn, wq_ref[...]) * float(X_HEAD_DIM ** -0.5)).astype(BF16)
    k = k_ref[...]
    v = v_ref[...]
    outs = []
    for h in range(X_HEADS):
        cols = slice(h * X_HEAD_DIM, (h + 1) * X_HEAD_DIM)
        s = _dot_nt(q[:, cols], k[:, cols])
        p = jnp.exp(s - jnp.max(s, axis=-1, keepdims=True))
        o = _dot(p.astype(BF16), v[:, cols]) / jnp.sum(p, axis=-1, keepdims=True)
        outs.append(o.astype(BF16))
    hh = _dot(jnp.concatenate(outs, axis=1), wo_ref[...])
    xn = x + _rms(hh, gpost_ref[...])
    xo_ref[...] = xn
    hn = _rms(xn, gnext_ref[...])
    if pack_next:
        half = hn.shape[1] // 2
        hn_ref[...] = _pack_bf16_pair(hn[:, :half], hn[:, half:])
    else:
        hn_ref[...] = hn.astype(hn_ref.dtype)


def cross_attention(x, kv, wq, wo, g_pre, g_post, g_next, seq, pack_next=False, tm=256):
    t, d = x.shape
    hd = X_HEADS * X_HEAD_DIM
    nsb = seq // tm
    row = pl.BlockSpec((tm, d), lambda i: (i, 0))
    vec = pl.BlockSpec((1, d), lambda i: (0, 0))
    if pack_next:
        hn_spec = pl.BlockSpec((tm, d // 2), lambda i: (i, 0))
        hn_shape = jax.ShapeDtypeStruct((t, d // 2), jnp.uint32)
    else:
        hn_spec = row
        hn_shape = jax.ShapeDtypeStruct((t, d), BF16)
    return pl.pallas_call(
        functools.partial(_xattn_kernel, pack_next=pack_next),
        grid=(t // tm,),
        in_specs=[row, vec,
                  pl.BlockSpec((d, hd), lambda i: (0, 0)),
                  pl.BlockSpec((N_MEM, hd), lambda i: (i // nsb, 0)),
                  pl.BlockSpec((N_MEM, hd), lambda i: (i // nsb, 1)),
                  pl.BlockSpec((hd, d), lambda i: (0, 0)),
                  vec, vec],
        out_specs=[row, hn_spec],
        out_shape=[jax.ShapeDtypeStruct((t, d), F32), hn_shape],
        compiler_params=_params(("arbitrary",)),
        name="cross_attention",
    )(x, g_pre.reshape(1, d), wq, kv, kv, wo, g_post.reshape(1, d), g_next.reshape(1, d))


def _router_kernel(x_ref, g_ref, wt_ref, idx_ref, gate_ref):
    hn = _rms(x_ref[...], g_ref[...])
    wt = wt_ref[...]
    h_hi = hn.astype(BF16)
    h_lo = (hn - h_hi.astype(F32)).astype(BF16)
    w_hi = wt.astype(BF16)
    w_lo = (wt - w_hi.astype(F32)).astype(BF16)
    logits = _dot_nt(w_hi, h_hi) + (_dot_nt(w_hi, h_lo) + _dot_nt(w_lo, h_hi))
    e = lax.broadcasted_iota(jnp.int32, logits.shape, 0)
    m1 = jnp.max(logits, axis=0, keepdims=True)
    i1 = jnp.min(jnp.where(logits == m1, e, N_EXPERTS), axis=0, keepdims=True)
    rest = jnp.where(e == i1, -jnp.inf, logits)
    m2 = jnp.max(rest, axis=0, keepdims=True)
    i2 = jnp.min(jnp.where(rest == m2, e, N_EXPERTS), axis=0, keepdims=True)
    ex = jnp.exp(m2 - m1)
    g1 = 1.0 / (1.0 + ex)
    idx_ref[...] = jnp.concatenate([i1, i2], axis=0)
    gate_ref[...] = jnp.concatenate([g1, ex * g1], axis=0)


def router(x, g, w_router, tm=512):
    t, d = x.shape
    return pl.pallas_call(
        _router_kernel,
        grid=(t // tm,),
        in_specs=[pl.BlockSpec((tm, d), lambda i: (i, 0)),
                  pl.BlockSpec((1, d), lambda i: (0, 0)),
                  pl.BlockSpec((N_EXPERTS, d), lambda i: (0, 0))],
        out_specs=[pl.BlockSpec((2, tm), lambda i: (0, i)),
                   pl.BlockSpec((2, tm), lambda i: (0, i))],
        out_shape=[jax.ShapeDtypeStruct((2, t), jnp.int32), jax.ShapeDtypeStruct((2, t), F32)],
        compiler_params=_params(("arbitrary",)),
        name="router",
    )(x, g.reshape(1, d), w_router.T)


def _live_col(tile_sub, t, j, n_cols):
    return jnp.where(tile_sub[t] > 0, j, n_cols - 1)


def _moe_gu_kernel(te_ref, tx_ref, tq_ref, x_ref, wg_ref, wu_ref, o_ref):
    n_sub = tq_ref[pl.program_id(0)]
    tm = o_ref.shape[0]
    half = wg_ref.shape[1] // 2

    for q in range(1, tm // MOE_SUB + 1):
        rows = q * MOE_SUB

        @pl.when(n_sub == q)
        def _(rows=rows):
            x_lo, x_hi = _unpack_bf16_pair(x_ref[:rows, :])
            g = _dot(x_lo, _bf(wg_ref[0, :half, :])) + _dot(x_hi, _bf(wg_ref[0, half:, :]))
            u = _dot(x_lo, _bf(wu_ref[0, :half, :])) + _dot(x_hi, _bf(wu_ref[0, half:, :]))
            o_ref[:rows, :] = (jax.nn.silu(g) * u).astype(o_ref.dtype)
            if rows < tm:
                o_ref[rows:, :] = jnp.zeros((tm - rows, o_ref.shape[1]), o_ref.dtype)

    @pl.when(n_sub == 0)
    def _():
        o_ref[...] = jnp.zeros_like(o_ref)


def moe_gate_up(xs, w_gu, tile_e, tile_x, tile_sub, tn=256):
    p = xs.shape[0]
    d = w_gu.shape[1]
    f = w_gu.shape[2] // 2
    nf = f // tn
    nt = p // MOE_TM
    grid_spec = pltpu.PrefetchScalarGridSpec(
        num_scalar_prefetch=3,
        grid=(nt, nf),
        in_specs=[pl.BlockSpec((MOE_TM, d // 2), lambda t, j, te, tx, ts: (tx[t], 0)),
                  pl.BlockSpec((1, d, tn), lambda t, j, te, tx, ts: (te[t], 0, _live_col(ts, t, j, nf))),
                  pl.BlockSpec((1, d, tn), lambda t, j, te, tx, ts: (te[t], 0, _live_col(ts, t, j, nf) + nf))],
        out_specs=pl.BlockSpec((MOE_TM, tn), lambda t, j, te, tx, ts: (t, j)),
    )
    return pl.pallas_call(
        _moe_gu_kernel,
        grid_spec=grid_spec,
        out_shape=jax.ShapeDtypeStruct((p, f), BF16),
        compiler_params=_params(("arbitrary", "arbitrary")),
        name="moe_gate_up",
    )(tile_e, tile_x, tile_sub, xs, w_gu, w_gu)


def _moe_down_kernel(te_ref, tx_ref, tq_ref, a_ref, w_ref, o_ref):
    n_sub = tq_ref[pl.program_id(0)]
    tm = o_ref.shape[0]

    for q in range(1, tm // MOE_SUB + 1):
        rows = q * MOE_SUB

        @pl.when(n_sub == q)
        def _(rows=rows):
            o_ref[:rows, :] = _dot(a_ref[:rows, :], _bf(w_ref[0]))
            if rows < tm:
                o_ref[rows:, :] = jnp.zeros((tm - rows, o_ref.shape[1]), o_ref.dtype)

    @pl.when(n_sub == 0)
    def _():
        o_ref[...] = jnp.zeros_like(o_ref)


def moe_down(act, w_down, tile_e, tile_x, tile_sub, tn=512):
    p, f = act.shape
    d = w_down.shape[2]
    nt = p // MOE_TM
    grid_spec = pltpu.PrefetchScalarGridSpec(
        num_scalar_prefetch=3,
        grid=(nt, d // tn),
        in_specs=[pl.BlockSpec((MOE_TM, f), lambda t, j, te, tx, ts: (tx[t], 0)),
                  pl.BlockSpec((1, f, tn), lambda t, j, te, tx, ts: (te[t], 0, _live_col(ts, t, j, d // tn)))],
        out_specs=pl.BlockSpec((MOE_TM, tn), lambda t, j, te, tx, ts: (t, j)),
    )
    return pl.pallas_call(
        _moe_down_kernel,
        grid_spec=grid_spec,
        out_shape=jax.ShapeDtypeStruct((p, d), F32),
        compiler_params=_params(("arbitrary", "arbitrary")),
        name="moe_down",
    )(tile_e, tile_x, tile_sub, act, w_down)


def moe_layer(x, g_pre, hn_packed, w_router, w_gu, w_down):
    t = x.shape[0]
    idx, gates = router(x, g_pre, w_router)
    experts = jnp.arange(N_EXPERTS, dtype=jnp.int32)
    e_flat = idx.reshape(-1)
    csum = jnp.cumsum((e_flat[:, None] == experts[None, :]).astype(jnp.int32), axis=0)
    rank = jnp.take_along_axis(csum, e_flat[:, None], axis=1)[:, 0] - 1
    count = csum[-1]
    n_tiles = (count + MOE_TM - 1) // MOE_TM
    tile_end = jnp.cumsum(n_tiles)
    tile_start = tile_end - n_tiles
    pos = tile_start[e_flat] * MOE_TM + rank
    nt = 2 * t // MOE_TM + N_EXPERTS
    p = nt * MOE_TM
    row_tok = (jnp.arange(p, dtype=jnp.int32) % t).at[pos].set(jnp.tile(jnp.arange(t, dtype=jnp.int32), 2))
    n_used = tile_end[-1]
    tile_id = jnp.arange(nt, dtype=jnp.int32)
    tile_x = jnp.minimum(tile_id, n_used - 1)
    tile_e = jnp.sum((tile_end[None, :] <= tile_x[:, None]).astype(jnp.int32), axis=1)
    tile_e = jnp.minimum(tile_e, N_EXPERTS - 1)
    rows_left = count[tile_e] - (tile_x - tile_start[tile_e]) * MOE_TM
    tile_sub = (jnp.clip(rows_left, 0, MOE_TM) + MOE_SUB - 1) // MOE_SUB
    tile_sub = jnp.where(tile_id < n_used, tile_sub, 0).astype(jnp.int32)
    xs = hn_packed.at[row_tok].get(mode="promise_in_bounds")
    act = moe_gate_up(xs, w_gu, tile_e, tile_x, tile_sub)
    y = moe_down(act, w_down, tile_e, tile_x, tile_sub)
    y0 = y.at[pos[:t]].get(mode="promise_in_bounds")
    y1 = y.at[pos[t:]].get(mode="promise_in_bounds")
    return y0, y1, gates.T


def _rope_tables(positions):
    pos = positions.reshape(-1).astype(F32)
    inv_a = ROPE_BASE ** (-jnp.arange(0, A_ROPE, 2, dtype=F32) / A_ROPE)
    ang_a = pos[:, None] * inv_a
    cos_a, sin_a = jnp.cos(ang_a), jnp.sin(ang_a)
    zero = jnp.zeros_like(cos_a)
    tab64 = jnp.concatenate([cos_a, cos_a, -sin_a, sin_a], axis=1)
    tab_kc = jnp.concatenate([cos_a, cos_a, zero, zero], axis=1)
    tab_ks = jnp.concatenate([-sin_a, sin_a, zero, zero], axis=1)
    inv_b = ROPE_BASE ** (-jnp.arange(0, B_QK, 2, dtype=F32) / B_QK)
    ang_b = pos[:, None] * inv_b
    return tab64, tab_kc, tab_ks, jnp.cos(ang_b), jnp.sin(ang_b)


def _swap_halves(w):
    half = w.shape[-1] // 2
    return jnp.concatenate([w[..., half:], w[..., :half]], axis=-1)


def kernel(x, mem, positions, norm_mix_pre, norm_mix_post, norm_x_pre, norm_x_post, norm_ffn_pre, norm_ffn_post, norm_mem, x_w_q, x_w_kv, x_w_o, ab_w_in, ab_q_norm, ab_w_uq, ab_kv_norm, ab_w_ukv, ab_ret_norm, ab_w_out, ffn_w_gu, ffn_w_down, c_w_in, c_v_norm, c_w_s, c_b_s, c_w_out, moe_router, moe_w_gu, moe_w_down):
    batch, seq, d = x.shape
    t = batch * seq
    x = x.reshape(t, d)
    mem2 = mem.reshape(batch * N_MEM, d)
    tab64, tab_kc, tab_ks, cos_b, sin_b = _rope_tables(positions)

    w_uq = ab_w_uq[0].reshape(A_Q_RANK, A_HEADS, A_NOPE + A_ROPE)
    w_pe = w_uq[:, :, A_NOPE:]
    wq_r = jnp.concatenate([w_uq[:, :, :A_NOPE], w_pe, _swap_halves(w_pe)], axis=-1)
    wq_r = wq_r.transpose(1, 0, 2).astype(BF16)

    hn = rms_norm_bf16(x, norm_mix_pre[0])
    z = ab_in_proj(hn, ab_w_in)
    cqn, ckvn, kpe = a_norm(z, hn, ab_w_in, tab_kc, tab_ks, ab_q_norm[0], ab_kv_norm[0])
    q = q_proj(cqn, wq_r, tab64, batch, seq)
    k, v = kv_proj(ckvn, ab_w_ukv[0], kpe, batch, seq)
    ya = causal_attention(q, k, v)
    yb = retention(z, cos_b, sin_b, ab_ret_norm[0], batch, seq)
    h = matmul_cat(ya, yb, ab_w_out[0], F32, tm=1024, tn=512, name="ab_out_proj")
    x = resid_norm([h], x, norm_mix_post[0])
    memn = rms_norm_bf16(mem2, norm_mem[0])
    kv = matmul(memn, x_w_kv[0], BF16, tm=512, tn=512, name="mem_kv_proj")
    x, hn = cross_attention(x, kv, x_w_q[0].astype(BF16), x_w_o[0].astype(BF16),
                            norm_x_pre[0], norm_x_post[0], norm_ffn_pre[0], seq)
    act = matmul_swiglu(hn, ffn_w_gu[0], tm=1024, tn=256)
    h = matmul_ksplit(act, ffn_w_down[0].astype(BF16), tm=1024, tn=2048, tk=512)
    x, hn = resid_norm([h], x, norm_ffn_post[0], norm_mix_pre[1])

    zc = matmul(hn, c_w_in[0], BF16, tm=1024, tn=512, act="gelu", name="c_in_proj")
    yc = spatial_gate(zc, c_v_norm[0], c_w_s[0], c_b_s[0])
    h = matmul(yc, c_w_out[0], F32, tm=1024, tn=512, name="c_out_proj")
    x = resid_norm([h], x, norm_mix_post[1])
    memn = rms_norm_bf16(mem2, norm_mem[1])
    kv = matmul(memn, x_w_kv[1], BF16, tm=512, tn=512, name="mem_kv_proj")
    x, hn_packed = cross_attention(x, kv, x_w_q[1].astype(BF16), x_w_o[1].astype(BF16),
                                   norm_x_pre[1], norm_x_post[1], norm_ffn_pre[1], seq, pack_next=True)
    y0, y1, gates = moe_layer(x, norm_ffn_pre[1], hn_packed, moe_router[0], moe_w_gu[0], moe_w_down[0])
    x = resid_norm([y0, y1], x, norm_ffn_post[1], weights=gates)
    return x.reshape(batch, seq, d)
```

```python
import functools

import jax
import jax.numpy as jnp
from jax import lax
from jax.experimental import pallas as pl
from jax.experimental.pallas import tpu as pltpu

F32 = jnp.float32
BF16 = jnp.bfloat16

D_MODEL = 4096
N_MEM = 256
EPS = 1e-6
ROPE_BASE = 10000.0
A_HEADS = 16
A_Q_RANK = 1024
A_KV_RANK = 512
A_NOPE = 128
A_ROPE = 64
A_V = 128
B_HEADS = 8
B_QK = 256
B_V = 256
RET_CHUNK = 128
C_GROUPS = 8
C_WIDTH = 4096
C_CHUNK = 128
X_HEADS = 4
X_HEAD_DIM = 128
D_FF = 14336
N_EXPERTS = 8
D_EXPERT = 3584

LANES = 128
VMEM_LIMIT = 56 * 1024 * 1024
NEG_BIG = -1e30

Z_CQ = 0
Z_CKV = A_Q_RANK
Z_RQ = A_Q_RANK + A_KV_RANK
Z_RK = Z_RQ + B_HEADS * B_QK
Z_RV = Z_RK + B_HEADS * B_QK
Z_RG = Z_RV + B_HEADS * B_V
Z_W = Z_RG + B_HEADS * B_V
A_IN = A_Q_RANK + A_KV_RANK + A_ROPE

MOE_TM = 1024
MOE_SUB = 256


def _params(sem, vmem=VMEM_LIMIT):
    return pltpu.CompilerParams(dimension_semantics=sem, vmem_limit_bytes=vmem)


def _rms(xf, g):
    ms = jnp.mean(xf * xf, axis=-1, keepdims=True)
    return xf * lax.rsqrt(ms + EPS) * g


def _bf(w):
    return w if w.dtype == BF16 else w.astype(BF16)


def _dot(a, b):
    return jnp.dot(a, b, preferred_element_type=F32)


def _dot_nt(a, b):
    return lax.dot_general(a, b, (((1,), (1,)), ((), ())), preferred_element_type=F32)


def _dot_tn(a, b):
    return lax.dot_general(a, b, (((0,), (0,)), ((), ())), preferred_element_type=F32)


def _norm_kernel(x_ref, g_ref, o_ref):
    o_ref[...] = _rms(x_ref[...].astype(F32), g_ref[...]).astype(o_ref.dtype)


def rms_norm_bf16(x, g, tm=512):
    m, d = x.shape
    return pl.pallas_call(
        _norm_kernel,
        grid=(m // tm,),
        in_specs=[pl.BlockSpec((tm, d), lambda i: (i, 0)),
                  pl.BlockSpec((1, d), lambda i: (0, 0))],
        out_specs=pl.BlockSpec((tm, d), lambda i: (i, 0)),
        out_shape=jax.ShapeDtypeStruct((m, d), BF16),
        compiler_params=_params(("arbitrary",)),
        name="rms_norm",
    )(x, g.reshape(1, d))


def _resid_kernel(*refs, n_h, gated, with_next):
    h_refs, rest = refs[:n_h], refs[n_h:]
    if gated:
        w = rest[0][...]
        rest = rest[1:]
        h = h_refs[0][...] * w[:, 0:1]
        for c, r in enumerate(h_refs[1:], start=1):
            h = h + r[...] * w[:, c:c + 1]
    else:
        h = h_refs[0][...]
        for r in h_refs[1:]:
            h = h + r[...]
    x_ref, g_ref = rest[0], rest[1]
    xn = x_ref[...] + _rms(h, g_ref[...])
    if with_next:
        gn_ref, xo_ref, hn_ref = rest[2:]
        xo_ref[...] = xn
        hn_ref[...] = _rms(xn, gn_ref[...]).astype(hn_ref.dtype)
    else:
        xo_ref = rest[2]
        xo_ref[...] = xn


def resid_norm(hs, x, g_post, g_next=None, weights=None, tm=256):
    m, d = x.shape
    row = pl.BlockSpec((tm, d), lambda i: (i, 0))
    vec = pl.BlockSpec((1, d), lambda i: (0, 0))
    with_next = g_next is not None
    gated = weights is not None
    in_specs = [row] * len(hs) + ([pl.BlockSpec((tm, len(hs)), lambda i: (i, 0))] if gated else [])
    in_specs += [row, vec] + ([vec] if with_next else [])
    args = list(hs) + ([weights] if gated else [])
    args += [x, g_post.reshape(1, d)] + ([g_next.reshape(1, d)] if with_next else [])
    out_shape = [jax.ShapeDtypeStruct((m, d), F32)]
    out_specs = [row]
    if with_next:
        out_shape.append(jax.ShapeDtypeStruct((m, d), BF16))
        out_specs.append(row)
    res = pl.pallas_call(
        functools.partial(_resid_kernel, n_h=len(hs), gated=gated, with_next=with_next),
        grid=(m // tm,),
        in_specs=in_specs,
        out_specs=out_specs,
        out_shape=out_shape,
        compiler_params=_params(("arbitrary",)),
        name="resid_norm",
    )(*args)
    return res if with_next else res[0]


def _mm_kernel(x_ref, w_ref, o_ref, *, act):
    acc = _dot(x_ref[...], _bf(w_ref[...]))
    if act == "gelu":
        acc = jax.nn.gelu(acc)
    o_ref[...] = acc.astype(o_ref.dtype)


def matmul(x, w, out_dtype, tm, tn, act=None, name="matmul"):
    m, k = x.shape
    n = w.shape[1]
    return pl.pallas_call(
        functools.partial(_mm_kernel, act=act),
        grid=(n // tn, m // tm),
        in_specs=[pl.BlockSpec((tm, k), lambda j, i: (i, 0)),
                  pl.BlockSpec((k, tn), lambda j, i: (0, j))],
        out_specs=pl.BlockSpec((tm, tn), lambda j, i: (i, j)),
        out_shape=jax.ShapeDtypeStruct((m, n), out_dtype),
        compiler_params=_params(("arbitrary", "arbitrary")),
        name=name,
    )(x, w)


def _mm2_kernel(xa_ref, xb_ref, w_ref, o_ref):
    ka = xa_ref.shape[1]
    acc = _dot(xa_ref[...], _bf(w_ref[:ka, :]))
    acc = acc + _dot(xb_ref[...], _bf(w_ref[ka:, :]))
    o_ref[...] = acc.astype(o_ref.dtype)


def matmul_cat(xa, xb, w, out_dtype, tm, tn, name="matmul_cat"):
    m, ka = xa.shape
    kb = xb.shape[1]
    n = w.shape[1]
    return pl.pallas_call(
        _mm2_kernel,
        grid=(n // tn, m // tm),
        in_specs=[pl.BlockSpec((tm, ka), lambda j, i: (i, 0)),
                  pl.BlockSpec((tm, kb), lambda j, i: (i, 0)),
                  pl.BlockSpec((ka + kb, tn), lambda j, i: (0, j))],
        out_specs=pl.BlockSpec((tm, tn), lambda j, i: (i, j)),
        out_shape=jax.ShapeDtypeStruct((m, n), out_dtype),
        compiler_params=_params(("arbitrary", "arbitrary")),
        name=name,
    )(xa, xb, w)


def _swiglu_kernel(x_ref, wg_ref, wu_ref, o_ref):
    x = x_ref[...]
    g = _dot(x, _bf(wg_ref[...]))
    u = _dot(x, _bf(wu_ref[...]))
    o_ref[...] = (jax.nn.silu(g) * u).astype(o_ref.dtype)


def matmul_swiglu(x, w_gu, tm, tn, name="ffn_gate_up"):
    m, k = x.shape
    f = w_gu.shape[1] // 2
    nf = f // tn
    return pl.pallas_call(
        _swiglu_kernel,
        grid=(nf, m // tm),
        in_specs=[pl.BlockSpec((tm, k), lambda j, i: (i, 0)),
                  pl.BlockSpec((k, tn), lambda j, i: (0, j)),
                  pl.BlockSpec((k, tn), lambda j, i: (0, j + nf))],
        out_specs=pl.BlockSpec((tm, tn), lambda j, i: (i, j)),
        out_shape=jax.ShapeDtypeStruct((m, f), BF16),
        compiler_params=_params(("arbitrary", "arbitrary")),
        name=name,
    )(x, w_gu, w_gu)


def _mm_acc_kernel(x_ref, w_ref, o_ref):
    @pl.when(pl.program_id(2) == 0)
    def _():
        o_ref[...] = jnp.zeros_like(o_ref)

    o_ref[...] += _dot(x_ref[...], _bf(w_ref[...]))


def matmul_ksplit(x, w, tm, tn, tk, name="ffn_down"):
    m, k = x.shape
    n = w.shape[1]
    return pl.pallas_call(
        _mm_acc_kernel,
        grid=(m // tm, n // tn, k // tk),
        in_specs=[pl.BlockSpec((tm, tk), lambda i, j, kk: (i, kk)),
                  pl.BlockSpec((tk, tn), lambda i, j, kk: (kk, j))],
        out_specs=pl.BlockSpec((tm, tn), lambda i, j, kk: (i, j)),
        out_shape=jax.ShapeDtypeStruct((m, n), F32),
        compiler_params=_params(("arbitrary", "arbitrary", "arbitrary")),
        name=name,
    )(x, w)


def _rope64(pe_pair, tab):
    p = pe_pair * tab
    return p + pltpu.roll(p, A_ROPE, axis=1)


def _ab_in_kernel(x_ref, wt_ref, o_ref, w_sc):
    @pl.when(pl.program_id(1) == 0)
    def _():
        w_sc[...] = wt_ref[...].astype(BF16)

    o_ref[...] = _dot_nt(x_ref[...], w_sc[...]).astype(o_ref.dtype)


def ab_in_proj(x, w_in_t, tm=1024, tn=512):
    m, k = x.shape
    plain = A_Q_RANK + A_KV_RANK
    assert plain % tn == 0 and Z_W % tn == 0 and A_ROPE % 8 == 0

    def feature_row(j):
        return pl.multiple_of(j * tn + jnp.where(j * tn >= plain, A_ROPE, 0), 8)

    return pl.pallas_call(
        _ab_in_kernel,
        grid=(Z_W // tn, m // tm),
        in_specs=[pl.BlockSpec((tm, k), lambda j, i: (i, 0)),
                  pl.BlockSpec((pl.Element(tn), pl.Element(k)), lambda j, i: (feature_row(j), 0))],
        out_specs=pl.BlockSpec((tm, tn), lambda j, i: (i, j)),
        out_shape=jax.ShapeDtypeStruct((m, Z_W), BF16),
        scratch_shapes=[pltpu.VMEM((tn, k), BF16)],
        compiler_params=_params(("arbitrary", "arbitrary")),
        name="ab_in_proj",
    )(x, w_in_t)


def _a_norm_kernel(cq_ref, ckv_ref, x_ref, wkr_ref, tabc_ref, tabs_ref, qn_ref, kvn_ref,
                   cqn_ref, ckvn_ref, kpe_ref):
    cqn_ref[...] = _rms(cq_ref[...].astype(F32), qn_ref[...]).astype(BF16)
    ckvn_ref[...] = _rms(ckv_ref[...].astype(F32), kvn_ref[...]).astype(BF16)
    kr = _dot_nt(x_ref[...], _bf(wkr_ref[...]))
    lane = lax.broadcasted_iota(jnp.int32, kr.shape, 1)
    half = A_ROPE // 2
    swapped = jnp.where(lane < half, pltpu.roll(kr, LANES - half, axis=1), pltpu.roll(kr, half, axis=1))
    kpe_ref[...] = (kr * tabc_ref[...] + swapped * tabs_ref[...]).astype(BF16)


def a_norm(z, hn, w_in_t, tab_kc, tab_ks, q_norm, kv_norm, tm=512):
    t, d = hn.shape
    kr_block = (A_Q_RANK + A_KV_RANK) // LANES
    return pl.pallas_call(
        _a_norm_kernel,
        grid=(t // tm,),
        in_specs=[pl.BlockSpec((tm, A_Q_RANK), lambda i: (i, Z_CQ // A_Q_RANK)),
                  pl.BlockSpec((tm, A_KV_RANK), lambda i: (i, Z_CKV // A_KV_RANK)),
                  pl.BlockSpec((tm, d), lambda i: (i, 0)),
                  pl.BlockSpec((LANES, d), lambda i: (kr_block, 0)),
                  pl.BlockSpec((tm, LANES), lambda i: (i, 0)),
                  pl.BlockSpec((tm, LANES), lambda i: (i, 0)),
                  pl.BlockSpec((1, A_Q_RANK), lambda i: (0, 0)),
                  pl.BlockSpec((1, A_KV_RANK), lambda i: (0, 0))],
        out_specs=[pl.BlockSpec((tm, A_Q_RANK), lambda i: (i, 0)),
                   pl.BlockSpec((tm, A_KV_RANK), lambda i: (i, 0)),
                   pl.BlockSpec((tm, LANES), lambda i: (i, 0))],
        out_shape=[jax.ShapeDtypeStruct((t, A_Q_RANK), BF16),
                   jax.ShapeDtypeStruct((t, A_KV_RANK), BF16),
                   jax.ShapeDtypeStruct((t, LANES), BF16)],
        compiler_params=_params(("arbitrary",)),
        name="a_norm",
    )(z, z, hn, w_in_t, tab_kc, tab_ks, q_norm.reshape(1, -1), kv_norm.reshape(1, -1))


def _q_proj_kernel(x_ref, w_ref, tab_ref, o_ref, *, scale):
    acc = _dot(x_ref[...], w_ref[0])
    nope = acc[:, :A_NOPE] * scale
    pe = _rope64(acc[:, A_NOPE:], tab_ref[...]) * scale
    o_ref[0, 0] = jnp.concatenate([nope, pe], axis=1).astype(o_ref.dtype)


def q_proj(cqn, wq_r, tab64, batch, seq, tm=1024):
    nsb = seq // tm
    scale = float((A_NOPE + A_ROPE) ** -0.5)
    return pl.pallas_call(
        functools.partial(_q_proj_kernel, scale=scale),
        grid=(batch * nsb, A_HEADS),
        in_specs=[pl.BlockSpec((tm, A_Q_RANK), lambda i, h: (i, 0)),
                  pl.BlockSpec((1, A_Q_RANK, 2 * LANES), lambda i, h: (h, 0, 0)),
                  pl.BlockSpec((tm, LANES), lambda i, h: (i, 0))],
        out_specs=pl.BlockSpec((1, 1, tm, 2 * LANES), lambda i, h: (i // nsb, h, i % nsb, 0)),
        out_shape=jax.ShapeDtypeStruct((batch, A_HEADS, seq, 2 * LANES), BF16),
        compiler_params=_params(("arbitrary", "arbitrary")),
        name="q_proj",
    )(cqn, wq_r, tab64)


def _kv_proj_kernel(x_ref, w_ref, kpe_ref, k_ref, v_ref):
    acc = _dot(x_ref[...], _bf(w_ref[...]))
    k_ref[0, 0] = jnp.concatenate([acc[:, :A_NOPE].astype(BF16), kpe_ref[...]], axis=1)
    v_ref[0, 0] = acc[:, A_NOPE:].astype(BF16)


def kv_proj(ckvn, w_ukv, kpe, batch, seq, tm=1024):
    nsb = seq // tm
    hw = A_NOPE + A_V
    return pl.pallas_call(
        _kv_proj_kernel,
        grid=(batch * nsb, A_HEADS),
        in_specs=[pl.BlockSpec((tm, A_KV_RANK), lambda i, h: (i, 0)),
                  pl.BlockSpec((A_KV_RANK, hw), lambda i, h: (0, h)),
                  pl.BlockSpec((tm, LANES), lambda i, h: (i, 0))],
        out_specs=[pl.BlockSpec((1, 1, tm, 2 * LANES), lambda i, h: (i // nsb, h, i % nsb, 0)),
                   pl.BlockSpec((1, 1, tm, A_V), lambda i, h: (i // nsb, h, i % nsb, 0))],
        out_shape=[jax.ShapeDtypeStruct((batch, A_HEADS, seq, 2 * LANES), BF16),
                   jax.ShapeDtypeStruct((batch, A_HEADS, seq, A_V), BF16)],
        compiler_params=_params(("arbitrary", "arbitrary")),
        name="kv_proj",
    )(ckvn, w_ukv, kpe)


def _flash_kernel(q_ref, k_ref, v_ref, o_ref, *, tq, tk, hp):
    qi = pl.program_id(2)
    dv = v_ref.shape[-1]

    def step_one(h, off, width, carry, masked):
        m, l, acc = carry
        s = _dot_nt(q_ref[0, h], k_ref[0, h, pl.ds(off, width), :])
        if masked:
            row = lax.broadcasted_iota(jnp.int32, s.shape, 0)
            col = lax.broadcasted_iota(jnp.int32, s.shape, 1)
            s = jnp.where(col <= row, s, NEG_BIG)
        m_new = jnp.maximum(m, jnp.max(s, axis=-1, keepdims=True))
        alpha = jnp.exp(m - m_new)
        p = jnp.exp(s - m_new)
        l = alpha * l + jnp.sum(p, axis=-1, keepdims=True)
        acc = alpha * acc + _dot(p.astype(BF16), v_ref[0, h, pl.ds(off, width), :])
        return m_new, l, acc

    def step(off, width, carry, masked):
        return tuple(step_one(h, off, width, carry[h], masked) for h in range(hp))

    def body(j, carry):
        return step(pl.multiple_of(j * tk, tk), tk, carry, False)

    init = (jnp.full((tq, 1), NEG_BIG, F32), jnp.zeros((tq, 1), F32), jnp.zeros((tq, dv), F32))
    carry = (init,) * hp
    n_wide = (qi * tq) // tk
    carry = lax.fori_loop(0, n_wide, body, carry)
    if tk > tq:
        carry = lax.cond(qi * tq > n_wide * tk,
                         lambda c: step(pl.multiple_of(n_wide * tk, tq), tq, c, False),
                         lambda c: c, carry)
    carry = step(pl.multiple_of(qi * tq, tq), tq, carry, True)
    o_ref[...] = jnp.concatenate([acc / l for _, l, acc in carry], axis=1).astype(o_ref.dtype)


def causal_attention(q, k, v, tq=512, tk=1024, hp=4):
    b, h, s, dk = q.shape
    dv = v.shape[-1]
    nq = s // tq
    assert tk in (tq, 2 * tq) and s % tk == 0 and h % hp == 0
    return pl.pallas_call(
        functools.partial(_flash_kernel, tq=tq, tk=tk, hp=hp),
        grid=(b, h // hp, nq),
        in_specs=[pl.BlockSpec((1, hp, tq, dk), lambda bi, hi, qi: (bi, hi, qi, 0)),
                  pl.BlockSpec((1, hp, s, dk), lambda bi, hi, qi: (bi, hi, 0, 0)),
                  pl.BlockSpec((1, hp, s, dv), lambda bi, hi, qi: (bi, hi, 0, 0))],
        out_specs=pl.BlockSpec((tq, hp * dv), lambda bi, hi, qi: (bi * nq + qi, hi)),
        out_shape=jax.ShapeDtypeStruct((b * s, h * dv), BF16),
        compiler_params=_params(("arbitrary", "arbitrary", "arbitrary")),
        name="causal_attention",
    )(q, k, v)


def _rope_half(x, cos, sin):
    half = x.shape[1] // 2
    x1, x2 = x[:, :half], x[:, half:]
    return jnp.concatenate([x1 * cos - x2 * sin, x1 * sin + x2 * cos], axis=1)


def _retention_kernel(cdec_ref, q_ref, k_ref, v_ref, g_ref, cos_ref, sin_ref, intra_ref, qdec_ref,
                      kdec_ref, rn_ref, o_ref, state_sc, *, n_chunks):
    hi = pl.program_id(1)

    @pl.when(pl.program_id(2) == 0)
    def _():
        state_sc[...] = jnp.zeros_like(state_sc)

    chunk_decay = cdec_ref[hi]
    intra = intra_ref[0]
    q_decay = qdec_ref[0]
    k_decay = kdec_ref[0]
    rn = rn_ref[...]
    k_scale = float(B_QK ** -0.5)
    for c in range(n_chunks):
        sl = slice(c * RET_CHUNK, (c + 1) * RET_CHUNK)
        cos = cos_ref[sl, :]
        sin = sin_ref[sl, :]
        qr = _rope_half(q_ref[sl, :].astype(F32), cos, sin)
        kr = _rope_half(k_ref[sl, :].astype(F32), cos, sin) * k_scale
        v = v_ref[sl, :]
        qb = qr.astype(BF16)
        s = _dot_nt(qb, kr.astype(BF16)) * intra
        inner = _dot(s.astype(BF16), v)
        state = state_sc[...]
        cross = _dot(qb, state.astype(BF16)) * q_decay
        state_sc[...] = state * chunk_decay + _dot_tn((kr * k_decay).astype(BF16), v)
        y = inner + cross
        mu = jnp.mean(y, axis=-1, keepdims=True)
        yc = y - mu
        var = jnp.mean(yc * yc, axis=-1, keepdims=True)
        yn = yc * lax.rsqrt(var + EPS) * rn
        o_ref[sl, :] = (yn * jax.nn.silu(g_ref[sl, :].astype(F32))).astype(o_ref.dtype)


def retention(z, cos, sin, ret_norm, batch, seq, rows=1024):
    t = z.shape[0]
    nr = seq // rows
    hh = jnp.arange(B_HEADS, dtype=F32)
    log_g = jnp.log1p(-jnp.exp2(-5.0 - hh))
    idx = jnp.arange(RET_CHUNK, dtype=F32)
    diff = idx[:, None] - idx[None, :]
    intra = jnp.where(diff >= 0, jnp.exp(log_g[:, None, None] * jnp.maximum(diff, 0.0)), 0.0)
    ones = jnp.ones((1, 1, B_QK), F32)
    q_decay = jnp.exp(log_g[:, None] * (idx + 1.0))[:, :, None] * ones
    k_decay = jnp.exp(log_g[:, None] * (RET_CHUNK - 1.0 - idx))[:, :, None] * ones
    chunk_decay = jnp.exp(log_g * RET_CHUNK)

    def zspec(off):
        return pl.BlockSpec((rows, B_QK), lambda b, h, r, cd: (b * nr + r, off // B_QK + h))

    tab = pl.BlockSpec((rows, B_QK // 2), lambda b, h, r, cd: (b * nr + r, 0))
    per_head = pl.BlockSpec((1, RET_CHUNK, B_QK), lambda b, h, r, cd: (h, 0, 0))
    grid_spec = pltpu.PrefetchScalarGridSpec(
        num_scalar_prefetch=1,
        grid=(batch, B_HEADS, nr),
        in_specs=[zspec(Z_RQ), zspec(Z_RK), zspec(Z_RV), zspec(Z_RG), tab, tab,
                  pl.BlockSpec((1, RET_CHUNK, RET_CHUNK), lambda b, h, r, cd: (h, 0, 0)),
                  per_head, per_head,
                  pl.BlockSpec((1, B_V), lambda b, h, r, cd: (0, h))],
        out_specs=pl.BlockSpec((rows, B_V), lambda b, h, r, cd: (b * nr + r, h)),
        scratch_shapes=[pltpu.VMEM((B_QK, B_V), F32)],
    )
    return pl.pallas_call(
        functools.partial(_retention_kernel, n_chunks=rows // RET_CHUNK),
        grid_spec=grid_spec,
        out_shape=jax.ShapeDtypeStruct((t, B_HEADS * B_V), BF16),
        compiler_params=_params(("arbitrary", "arbitrary", "arbitrary")),
        name="retention",
    )(chunk_decay, z, z, z, z, cos, sin, intra, q_decay, k_decay, ret_norm.reshape(1, -1))


def _sgu_kernel(u_ref, v_ref, vn_ref, ws_ref, bs_ref, o_ref, *, n_chunks):
    gw = C_WIDTH // C_GROUPS
    v = v_ref[...].astype(F32)
    mu = jnp.mean(v, axis=-1, keepdims=True)
    vc = v - mu
    var = jnp.mean(vc * vc, axis=-1, keepdims=True)
    vn = (vc * lax.rsqrt(var + EPS) * vn_ref[...]).astype(BF16)
    r = lax.broadcasted_iota(jnp.int32, (C_CHUNK, C_CHUNK), 0)
    c = lax.broadcasted_iota(jnp.int32, (C_CHUNK, C_CHUNK), 1)
    bs = bs_ref[...]
    for g in range(C_GROUPS):
        ws = jnp.where(c <= r, ws_ref[g], 0.0).astype(BF16)
        bias = bs[:, g:g + 1]
        cols = slice(g * gw, (g + 1) * gw)
        for ch in range(n_chunks):
            rows = slice(ch * C_CHUNK, (ch + 1) * C_CHUNK)
            mixed = _dot(ws, vn[rows, cols]) + bias
            o_ref[rows, cols] = (u_ref[rows, cols].astype(F32) * mixed).astype(o_ref.dtype)


def spatial_gate(z, v_norm, w_s, b_s, rows=512):
    t = z.shape[0]
    return pl.pallas_call(
        functools.partial(_sgu_kernel, n_chunks=rows // C_CHUNK),
        grid=(t // rows,),
        in_specs=[pl.BlockSpec((rows, C_WIDTH), lambda i: (i, 0)),
                  pl.BlockSpec((rows, C_WIDTH), lambda i: (i, 1)),
                  pl.BlockSpec((1, C_WIDTH), lambda i: (0, 0)),
                  pl.BlockSpec((C_GROUPS, C_CHUNK, C_CHUNK), lambda i: (0, 0, 0)),
                  pl.BlockSpec((C_CHUNK, C_GROUPS), lambda i: (0, 0))],
        out_specs=pl.BlockSpec((rows, C_WIDTH), lambda i: (i, 0)),
        out_shape=jax.ShapeDtypeStruct((t, C_WIDTH), BF16),
        compiler_params=_params(("arbitrary",)),
        name="spatial_gate",
    )(z, z, v_norm.reshape(1, -1), w_s, b_s.T)


def _pack_bf16_pair(lo, hi):
    lo_bits = lax.bitcast_convert_type(lo.astype(BF16).astype(F32), jnp.uint32)
    hi_bits = lax.bitcast_convert_type(hi.astype(BF16).astype(F32), jnp.uint32)
    return (lo_bits >> 16) | (hi_bits & jnp.uint32(0xFFFF0000))


def _unpack_bf16_pair(words):
    lo = lax.bitcast_convert_type(words << 16, F32).astype(BF16)
    hi = lax.bitcast_convert_type(words & jnp.uint32(0xFFFF0000), F32).astype(BF16)
    return lo, hi


def _xattn_kernel(x_ref, gpre_ref, wq_ref, k_ref, v_ref, wo_ref, gpost_ref, gnext_ref, xo_ref, hn_ref,
                  *, pack_next):
    x = x_ref[...]
    hn = _rms(x, gpre_ref[...]).astype(BF16)
    q = (_dot(hn, wq_ref[...]) * float(X_HEAD_DIM ** -0.5)).astype(BF16)
    k = k_ref[...]
    v = v_ref[...]
    outs = []
    for h in range(X_HEADS):
        cols = slice(h * X_HEAD_DIM, (h + 1) * X_HEAD_DIM)
        s = _dot_nt(q[:, cols], k[:, cols])
        p = jnp.exp(s - jnp.max(s, axis=-1, keepdims=True))
        o = _dot(p.astype(BF16), v[:, cols]) / jnp.sum(p, axis=-1, keepdims=True)
        outs.append(o.astype(BF16))
    hh = _dot(jnp.concatenate(outs, axis=1), wo_ref[...])
    xn = x + _rms(hh, gpost_ref[...])
    xo_ref[...] = xn
    hn = _rms(xn, gnext_ref[...])
    if pack_next:
        half = hn.shape[1] // 2
        hn_ref[...] = _pack_bf16_pair(hn[:, :half], hn[:, half:])
    else:
        hn_ref[...] = hn.astype(hn_ref.dtype)


def cross_attention(x, kv, wq, wo, g_pre, g_post, g_next, seq, pack_next=False, tm=256):
    t, d = x.shape
    hd = X_HEADS * X_HEAD_DIM
    nsb = seq // tm
    row = pl.BlockSpec((tm, d), lambda i: (i, 0))
    vec = pl.BlockSpec((1, d), lambda i: (0, 0))
    if pack_next:
        hn_spec = pl.BlockSpec((tm, d // 2), lambda i: (i, 0))
        hn_shape = jax.ShapeDtypeStruct((t, d // 2), jnp.uint32)
    else:
        hn_spec = row
        hn_shape = jax.ShapeDtypeStruct((t, d), BF16)
    return pl.pallas_call(
        functools.partial(_xattn_kernel, pack_next=pack_next),
        grid=(t // tm,),
        in_specs=[row, vec,
                  pl.BlockSpec((d, hd), lambda i: (0, 0)),
                  pl.BlockSpec((N_MEM, hd), lambda i: (i // nsb, 0)),
                  pl.BlockSpec((N_MEM, hd), lambda i: (i // nsb, 1)),
                  pl.BlockSpec((hd, d), lambda i: (0, 0)),
                  vec, vec],
        out_specs=[row, hn_spec],
        out_shape=[jax.ShapeDtypeStruct((t, d), F32), hn_shape],
        compiler_params=_params(("arbitrary",)),
        name="cross_attention",
    )(x, g_pre.reshape(1, d), wq, kv, kv, wo, g_post.reshape(1, d), g_next.reshape(1, d))


def _router_kernel(x_ref, g_ref, wt_ref, idx_ref, gate_ref):
    hn = _rms(x_ref[...], g_ref[...])
    wt = wt_ref[...]
    h_hi = hn.astype(BF16)
    h_lo = (hn - h_hi.astype(F32)).astype(BF16)
    w_hi = wt.astype(BF16)
    w_lo = (wt - w_hi.astype(F32)).astype(BF16)
    logits = _dot_nt(w_hi, h_hi) + (_dot_nt(w_hi, h_lo) + _dot_nt(w_lo, h_hi))
    e = lax.broadcasted_iota(jnp.int32, logits.shape, 0)
    m1 = jnp.max(logits, axis=0, keepdims=True)
    i1 = jnp.min(jnp.where(logits == m1, e, N_EXPERTS), axis=0, keepdims=True)
    rest = jnp.where(e == i1, -jnp.inf, logits)
    m2 = jnp.max(rest, axis=0, keepdims=True)
    i2 = jnp.min(jnp.where(rest == m2, e, N_EXPERTS), axis=0, keepdims=True)
    ex = jnp.exp(m2 - m1)
    g1 = 1.0 / (1.0 + ex)
    idx_ref[...] = jnp.concatenate([i1, i2], axis=0)
    gate_ref[...] = jnp.concatenate([g1, ex * g1], axis=0)


def router(x, g, w_router, tm=512):
    t, d = x.shape
    return pl.pallas_call(
        _router_kernel,
        grid=(t // tm,),
        in_specs=[pl.BlockSpec((tm, d), lambda i: (i, 0)),
                  pl.BlockSpec((1, d), lambda i: (0, 0)),
                  pl.BlockSpec((N_EXPERTS, d), lambda i: (0, 0))],
        out_specs=[pl.BlockSpec((2, tm), lambda i: (0, i)),
                   pl.BlockSpec((2, tm), lambda i: (0, i))],
        out_shape=[jax.ShapeDtypeStruct((2, t), jnp.int32), jax.ShapeDtypeStruct((2, t), F32)],
        compiler_params=_params(("arbitrary",)),
        name="router",
    )(x, g.reshape(1, d), w_router.T)


def _live_col(tile_sub, t, j, n_cols):
    return jnp.where(tile_sub[t] > 0, j, n_cols - 1)


def _moe_gu_kernel(te_ref, tx_ref, tq_ref, x_ref, wg_ref, wu_ref, o_ref):
    n_sub = tq_ref[pl.program_id(0)]
    tm = o_ref.shape[0]
    half = wg_ref.shape[1] // 2

    for q in range(1, tm // MOE_SUB + 1):
        rows = q * MOE_SUB

        @pl.when(n_sub == q)
        def _(rows=rows):
            x_lo, x_hi = _unpack_bf16_pair(x_ref[:rows, :])
            g = _dot(x_lo, _bf(wg_ref[0, :half, :])) + _dot(x_hi, _bf(wg_ref[0, half:, :]))
            u = _dot(x_lo, _bf(wu_ref[0, :half, :])) + _dot(x_hi, _bf(wu_ref[0, half:, :]))
            o_ref[:rows, :] = (jax.nn.silu(g) * u).astype(o_ref.dtype)
            if rows < tm:
                o_ref[rows:, :] = jnp.zeros((tm - rows, o_ref.shape[1]), o_ref.dtype)

    @pl.when(n_sub == 0)
    def _():
        o_ref[...] = jnp.zeros_like(o_ref)


def moe_gate_up(xs, w_gu, tile_e, tile_x, tile_sub, tn=256):
    p = xs.shape[0]
    d = w_gu.shape[1]
    f = w_gu.shape[2] // 2
    nf = f // tn
    nt = p // MOE_TM
    grid_spec = pltpu.PrefetchScalarGridSpec(
        num_scalar_prefetch=3,
        grid=(nt, nf),
        in_specs=[pl.BlockSpec((MOE_TM, d // 2), lambda t, j, te, tx, ts: (tx[t], 0)),
                  pl.BlockSpec((1, d, tn), lambda t, j, te, tx, ts: (te[t], 0, _live_col(ts, t, j, nf))),
                  pl.BlockSpec((1, d, tn), lambda t, j, te, tx, ts: (te[t], 0, _live_col(ts, t, j, nf) + nf))],
        out_specs=pl.BlockSpec((MOE_TM, tn), lambda t, j, te, tx, ts: (t, j)),
    )
    return pl.pallas_call(
        _moe_gu_kernel,
        grid_spec=grid_spec,
        out_shape=jax.ShapeDtypeStruct((p, f), BF16),
        compiler_params=_params(("arbitrary", "arbitrary")),
        name="moe_gate_up",
    )(tile_e, tile_x, tile_sub, xs, w_gu, w_gu)


def _moe_down_kernel(te_ref, tx_ref, tq_ref, a_ref, w_ref, o_ref):
    n_sub = tq_ref[pl.program_id(0)]
    tm = o_ref.shape[0]

    for q in range(1, tm // MOE_SUB + 1):
        rows = q * MOE_SUB

        @pl.when(n_sub == q)
        def _(rows=rows):
            o_ref[:rows, :] = _dot(a_ref[:rows, :], _bf(w_ref[0]))
            if rows < tm:
                o_ref[rows:, :] = jnp.zeros((tm - rows, o_ref.shape[1]), o_ref.dtype)

    @pl.when(n_sub == 0)
    def _():
        o_ref[...] = jnp.zeros_like(o_ref)


def moe_down(act, w_down, tile_e, tile_x, tile_sub, tn=512):
    p, f = act.shape
    d = w_down.shape[2]
    nt = p // MOE_TM
    grid_spec = pltpu.PrefetchScalarGridSpec(
        num_scalar_prefetch=3,
        grid=(nt, d // tn),
        in_specs=[pl.BlockSpec((MOE_TM, f), lambda t, j, te, tx, ts: (tx[t], 0)),
                  pl.BlockSpec((1, f, tn), lambda t, j, te, tx, ts: (te[t], 0, _live_col(ts, t, j, d // tn)))],
        out_specs=pl.BlockSpec((MOE_TM, tn), lambda t, j, te, tx, ts: (t, j)),
    )
    return pl.pallas_call(
        _moe_down_kernel,
        grid_spec=grid_spec,
        out_shape=jax.ShapeDtypeStruct((p, d), F32),
        compiler_params=_params(("arbitrary", "arbitrary")),
        name="moe_down",
    )(tile_e, tile_x, tile_sub, act, w_down)


def moe_layer(x, g_pre, hn_packed, w_router, w_gu, w_down):
    t = x.shape[0]
    idx, gates = router(x, g_pre, w_router)
    experts = jnp.arange(N_EXPERTS, dtype=jnp.int32)
    e_flat = idx.reshape(-1)
    csum = jnp.cumsum((e_flat[:, None] == experts[None, :]).astype(jnp.int32), axis=0)
    rank = jnp.take_along_axis(csum, e_flat[:, None], axis=1)[:, 0] - 1
    count = csum[-1]
    n_tiles = (count + MOE_TM - 1) // MOE_TM
    tile_end = jnp.cumsum(n_tiles)
    tile_start = tile_end - n_tiles
    pos = tile_start[e_flat] * MOE_TM + rank
    nt = 2 * t // MOE_TM + N_EXPERTS
    p = nt * MOE_TM
    row_tok = (jnp.arange(p, dtype=jnp.int32) % t).at[pos].set(jnp.tile(jnp.arange(t, dtype=jnp.int32), 2))
    n_used = tile_end[-1]
    tile_id = jnp.arange(nt, dtype=jnp.int32)
    tile_x = jnp.minimum(tile_id, n_used - 1)
    tile_e = jnp.sum((tile_end[None, :] <= tile_x[:, None]).astype(jnp.int32), axis=1)
    tile_e = jnp.minimum(tile_e, N_EXPERTS - 1)
    rows_left = count[tile_e] - (tile_x - tile_start[tile_e]) * MOE_TM
    tile_sub = (jnp.clip(rows_left, 0, MOE_TM) + MOE_SUB - 1) // MOE_SUB
    tile_sub = jnp.where(tile_id < n_used, tile_sub, 0).astype(jnp.int32)
    xs = hn_packed.at[row_tok].get(mode="promise_in_bounds")
    act = moe_gate_up(xs, w_gu, tile_e, tile_x, tile_sub)
    y = moe_down(act, w_down, tile_e, tile_x, tile_sub)
    y0 = y.at[pos[:t]].get(mode="promise_in_bounds")
    y1 = y.at[pos[t:]].get(mode="promise_in_bounds")
    return y0, y1, gates.T


def _rope_tables(positions):
    pos = positions.reshape(-1).astype(F32)
    inv_a = ROPE_BASE ** (-jnp.arange(0, A_ROPE, 2, dtype=F32) / A_ROPE)
    ang_a = pos[:, None] * inv_a
    cos_a, sin_a = jnp.cos(ang_a), jnp.sin(ang_a)
    zero = jnp.zeros_like(cos_a)
    tab64 = jnp.concatenate([cos_a, cos_a, -sin_a, sin_a], axis=1)
    tab_kc = jnp.concatenate([cos_a, cos_a, zero, zero], axis=1)
    tab_ks = jnp.concatenate([-sin_a, sin_a, zero, zero], axis=1)
    inv_b = ROPE_BASE ** (-jnp.arange(0, B_QK, 2, dtype=F32) / B_QK)
    ang_b = pos[:, None] * inv_b
    return tab64, tab_kc, tab_ks, jnp.cos(ang_b), jnp.sin(ang_b)


def _swap_halves(w):
    half = w.shape[-1] // 2
    return jnp.concatenate([w[..., half:], w[..., :half]], axis=-1)


def kernel(x, mem, positions, norm_mix_pre, norm_mix_post, norm_x_pre, norm_x_post, norm_ffn_pre, norm_ffn_post, norm_mem, x_w_q, x_w_kv, x_w_o, ab_w_in, ab_q_norm, ab_w_uq, ab_kv_norm, ab_w_ukv, ab_ret_norm, ab_w_out, ffn_w_gu, ffn_w_down, c_w_in, c_v_norm, c_w_s, c_b_s, c_w_out, moe_router, moe_w_gu, moe_w_down):
    batch, seq, d = x.shape
    t = batch * seq
    x = x.reshape(t, d)
    mem2 = mem.reshape(batch * N_MEM, d)
    tab64, tab_kc, tab_ks, cos_b, sin_b = _rope_tables(positions)

    w_uq = ab_w_uq[0].reshape(A_Q_RANK, A_HEADS, A_NOPE + A_ROPE)
    w_pe = w_uq[:, :, A_NOPE:]
    wq_r = jnp.concatenate([w_uq[:, :, :A_NOPE], w_pe, _swap_halves(w_pe)], axis=-1)
    wq_r = wq_r.transpose(1, 0, 2).astype(BF16)

    hn = rms_norm_bf16(x, norm_mix_pre[0])
    w_in_t = ab_w_in[0].T
    z = ab_in_proj(hn, w_in_t)
    cqn, ckvn, kpe = a_norm(z, hn, w_in_t, tab_kc, tab_ks, ab_q_norm[0], ab_kv_norm[0])
    q = q_proj(cqn, wq_r, tab64, batch, seq)
    k, v = kv_proj(ckvn, ab_w_ukv[0], kpe, batch, seq)
    ya = causal_attention(q, k, v)
    yb = retention(z, cos_b, sin_b, ab_ret_norm[0], batch, seq)
    h = matmul_cat(ya, yb, ab_w_out[0], F32, tm=1024, tn=512, name="ab_out_proj")
    x = resid_norm([h], x, norm_mix_post[0])
    memn = rms_norm_bf16(mem2, norm_mem[0])
    kv = matmul(memn, x_w_kv[0], BF16, tm=512, tn=512, name="mem_kv_proj")
    x, hn = cross_attention(x, kv, x_w_q[0].astype(BF16), x_w_o[0].astype(BF16),
                            norm_x_pre[0], norm_x_post[0], norm_ffn_pre[0], seq)
    act = matmul_swiglu(hn, ffn_w_gu[0], tm=1024, tn=256)
    h = matmul_ksplit(act, ffn_w_down[0], tm=1024, tn=2048, tk=512)
    x, hn = resid_norm([h], x, norm_ffn_post[0], norm_mix_pre[1])

    zc = matmul(hn, c_w_in[0], BF16, tm=1024, tn=512, act="gelu", name="c_in_proj")
    yc = spatial_gate(zc, c_v_norm[0], c_w_s[0], c_b_s[0])
    h = matmul(yc, c_w_out[0], F32, tm=1024, tn=512, name="c_out_proj")
    x = resid_norm([h], x, norm_mix_post[1])
    memn = rms_norm_bf16(mem2, norm_mem[1])
    kv = matmul(memn, x_w_kv[1], BF16, tm=512, tn=512, name="mem_kv_proj")
    x, hn_packed = cross_attention(x, kv, x_w_q[1].astype(BF16), x_w_o[1].astype(BF16),
                                   norm_x_pre[1], norm_x_post[1], norm_ffn_pre[1], seq, pack_next=True)
    y0, y1, gates = moe_layer(x, norm_ffn_pre[1], hn_packed, moe_router[0], moe_w_gu[0], moe_w_down[0])
    x = resid_norm([y0, y1], x, norm_ffn_post[1], weights=gates)
    return x.reshape(batch, seq, d)
```

```python
import functools

import jax
import jax.numpy as jnp
from jax import lax
from jax.experimental import pallas as pl
from jax.experimental.pallas import tpu as pltpu

F32 = jnp.float32
BF16 = jnp.bfloat16

D_MODEL = 4096
N_MEM = 256
EPS = 1e-6
ROPE_BASE = 10000.0
A_HEADS = 16
A_Q_RANK = 1024
A_KV_RANK = 512
A_NOPE = 128
A_ROPE = 64
A_V = 128
B_HEADS = 8
B_QK = 256
B_V = 256
RET_CHUNK = 128
C_GROUPS = 8
C_WIDTH = 4096
C_CHUNK = 128
X_HEADS = 4
X_HEAD_DIM = 128
D_FF = 14336
N_EXPERTS = 8
D_EXPERT = 3584

LANES = 128
VMEM_LIMIT = 56 * 1024 * 1024
NEG_BIG = -1e30

Z_CQ = 0
Z_CKV = A_Q_RANK
Z_RQ = A_Q_RANK + A_KV_RANK
Z_RK = Z_RQ + B_HEADS * B_QK
Z_RV = Z_RK + B_HEADS * B_QK
Z_RG = Z_RV + B_HEADS * B_V
Z_W = Z_RG + B_HEADS * B_V
A_IN = A_Q_RANK + A_KV_RANK + A_ROPE

MOE_TM = 1024
MOE_SUB = 256


def _params(sem, vmem=VMEM_LIMIT):
    return pltpu.CompilerParams(dimension_semantics=sem, vmem_limit_bytes=vmem)


def _rms(xf, g):
    ms = jnp.mean(xf * xf, axis=-1, keepdims=True)
    return xf * lax.rsqrt(ms + EPS) * g


def _bf(w):
    return w if w.dtype == BF16 else w.astype(BF16)


def _dot(a, b):
    return jnp.dot(a, b, preferred_element_type=F32)


def _dot_nt(a, b):
    return lax.dot_general(a, b, (((1,), (1,)), ((), ())), preferred_element_type=F32)


def _dot_tn(a, b):
    return lax.dot_general(a, b, (((0,), (0,)), ((), ())), preferred_element_type=F32)


def _norm_kernel(x_ref, g_ref, o_ref):
    o_ref[...] = _rms(x_ref[...].astype(F32), g_ref[...]).astype(o_ref.dtype)


def rms_norm_bf16(x, g, tm=512):
    m, d = x.shape
    return pl.pallas_call(
        _norm_kernel,
        grid=(m // tm,),
        in_specs=[pl.BlockSpec((tm, d), lambda i: (i, 0)),
                  pl.BlockSpec((1, d), lambda i: (0, 0))],
        out_specs=pl.BlockSpec((tm, d), lambda i: (i, 0)),
        out_shape=jax.ShapeDtypeStruct((m, d), BF16),
        compiler_params=_params(("arbitrary",)),
        name="rms_norm",
    )(x, g.reshape(1, d))


def _resid_kernel(*refs, n_h, gated, with_next):
    h_refs, rest = refs[:n_h], refs[n_h:]
    if gated:
        w = rest[0][...]
        rest = rest[1:]
        h = h_refs[0][...] * w[:, 0:1]
        for c, r in enumerate(h_refs[1:], start=1):
            h = h + r[...] * w[:, c:c + 1]
    else:
        h = h_refs[0][...]
        for r in h_refs[1:]:
            h = h + r[...]
    x_ref, g_ref = rest[0], rest[1]
    xn = x_ref[...] + _rms(h, g_ref[...])
    if with_next:
        gn_ref, xo_ref, hn_ref = rest[2:]
        xo_ref[...] = xn
        hn_ref[...] = _rms(xn, gn_ref[...]).astype(hn_ref.dtype)
    else:
        xo_ref = rest[2]
        xo_ref[...] = xn


def resid_norm(hs, x, g_post, g_next=None, weights=None, tm=256):
    m, d = x.shape
    row = pl.BlockSpec((tm, d), lambda i: (i, 0))
    vec = pl.BlockSpec((1, d), lambda i: (0, 0))
    with_next = g_next is not None
    gated = weights is not None
    in_specs = [row] * len(hs) + ([pl.BlockSpec((tm, len(hs)), lambda i: (i, 0))] if gated else [])
    in_specs += [row, vec] + ([vec] if with_next else [])
    args = list(hs) + ([weights] if gated else [])
    args += [x, g_post.reshape(1, d)] + ([g_next.reshape(1, d)] if with_next else [])
    out_shape = [jax.ShapeDtypeStruct((m, d), F32)]
    out_specs = [row]
    if with_next:
        out_shape.append(jax.ShapeDtypeStruct((m, d), BF16))
        out_specs.append(row)
    res = pl.pallas_call(
        functools.partial(_resid_kernel, n_h=len(hs), gated=gated, with_next=with_next),
        grid=(m // tm,),
        in_specs=in_specs,
        out_specs=out_specs,
        out_shape=out_shape,
        compiler_params=_params(("arbitrary",)),
        name="resid_norm",
    )(*args)
    return res if with_next else res[0]


def _mm_kernel(x_ref, w_ref, o_ref, *, act):
    acc = _dot(x_ref[...], _bf(w_ref[...]))
    if act == "gelu":
        acc = jax.nn.gelu(acc)
    o_ref[...] = acc.astype(o_ref.dtype)


def matmul(x, w, out_dtype, tm, tn, act=None, name="matmul"):
    m, k = x.shape
    n = w.shape[1]
    return pl.pallas_call(
        functools.partial(_mm_kernel, act=act),
        grid=(n // tn, m // tm),
        in_specs=[pl.BlockSpec((tm, k), lambda j, i: (i, 0)),
                  pl.BlockSpec((k, tn), lambda j, i: (0, j))],
        out_specs=pl.BlockSpec((tm, tn), lambda j, i: (i, j)),
        out_shape=jax.ShapeDtypeStruct((m, n), out_dtype),
        compiler_params=_params(("arbitrary", "arbitrary")),
        name=name,
    )(x, w)


def _mm2_kernel(xa_ref, xb_ref, w_ref, o_ref):
    ka = xa_ref.shape[1]
    acc = _dot(xa_ref[...], _bf(w_ref[:ka, :]))
    acc = acc + _dot(xb_ref[...], _bf(w_ref[ka:, :]))
    o_ref[...] = acc.astype(o_ref.dtype)


def matmul_cat(xa, xb, w, out_dtype, tm, tn, name="matmul_cat"):
    m, ka = xa.shape
    kb = xb.shape[1]
    n = w.shape[1]
    return pl.pallas_call(
        _mm2_kernel,
        grid=(n // tn, m // tm),
        in_specs=[pl.BlockSpec((tm, ka), lambda j, i: (i, 0)),
                  pl.BlockSpec((tm, kb), lambda j, i: (i, 0)),
                  pl.BlockSpec((ka + kb, tn), lambda j, i: (0, j))],
        out_specs=pl.BlockSpec((tm, tn), lambda j, i: (i, j)),
        out_shape=jax.ShapeDtypeStruct((m, n), out_dtype),
        compiler_params=_params(("arbitrary", "arbitrary")),
        name=name,
    )(xa, xb, w)


def _swiglu_kernel(x_ref, wg_ref, wu_ref, o_ref):
    x = x_ref[...]
    g = _dot(x, _bf(wg_ref[...]))
    u = _dot(x, _bf(wu_ref[...]))
    o_ref[...] = (jax.nn.silu(g) * u).astype(o_ref.dtype)


def matmul_swiglu(x, w_gu, tm, tn, name="ffn_gate_up"):
    m, k = x.shape
    f = w_gu.shape[1] // 2
    nf = f // tn
    return pl.pallas_call(
        _swiglu_kernel,
        grid=(nf, m // tm),
        in_specs=[pl.BlockSpec((tm, k), lambda j, i: (i, 0)),
                  pl.BlockSpec((k, tn), lambda j, i: (0, j)),
                  pl.BlockSpec((k, tn), lambda j, i: (0, j + nf))],
        out_specs=pl.BlockSpec((tm, tn), lambda j, i: (i, j)),
        out_shape=jax.ShapeDtypeStruct((m, f), BF16),
        compiler_params=_params(("arbitrary", "arbitrary")),
        name=name,
    )(x, w_gu, w_gu)


def _mm_acc_kernel(x_ref, w_ref, o_ref):
    @pl.when(pl.program_id(2) == 0)
    def _():
        o_ref[...] = jnp.zeros_like(o_ref)

    o_ref[...] += _dot(x_ref[...], _bf(w_ref[...]))


def matmul_ksplit(x, w, tm, tn, tk, name="ffn_down"):
    m, k = x.shape
    n = w.shape[1]
    return pl.pallas_call(
        _mm_acc_kernel,
        grid=(m // tm, n // tn, k // tk),
        in_specs=[pl.BlockSpec((tm, tk), lambda i, j, kk: (i, kk)),
                  pl.BlockSpec((tk, tn), lambda i, j, kk: (kk, j))],
        out_specs=pl.BlockSpec((tm, tn), lambda i, j, kk: (i, j)),
        out_shape=jax.ShapeDtypeStruct((m, n), F32),
        compiler_params=_params(("arbitrary", "arbitrary", "arbitrary")),
        name=name,
    )(x, w)


def _rope64(pe_pair, tab):
    p = pe_pair * tab
    return p + pltpu.roll(p, A_ROPE, axis=1)


def _ab_in_kernel(x_ref, wt_ref, o_ref, w_sc):
    @pl.when(pl.program_id(1) == 0)
    def _():
        w_sc[...] = wt_ref[...].astype(BF16)

    o_ref[...] = _dot_nt(x_ref[...], w_sc[...]).astype(o_ref.dtype)


def ab_in_proj(x, w_in_t, tm=1024, tn=512):
    m, k = x.shape
    plain = A_Q_RANK + A_KV_RANK
    assert plain % tn == 0 and Z_W % tn == 0 and A_ROPE % 8 == 0

    def feature_row(j):
        return pl.multiple_of(j * tn + jnp.where(j * tn >= plain, A_ROPE, 0), 8)

    return pl.pallas_call(
        _ab_in_kernel,
        grid=(Z_W // tn, m // tm),
        in_specs=[pl.BlockSpec((tm, k), lambda j, i: (i, 0)),
                  pl.BlockSpec((pl.Element(tn), pl.Element(k)), lambda j, i: (feature_row(j), 0))],
        out_specs=pl.BlockSpec((tm, tn), lambda j, i: (i, j)),
        out_shape=jax.ShapeDtypeStruct((m, Z_W), BF16),
        scratch_shapes=[pltpu.VMEM((tn, k), BF16)],
        compiler_params=_params(("arbitrary", "arbitrary")),
        name="ab_in_proj",
    )(x, w_in_t)


def _a_norm_kernel(cq_ref, ckv_ref, x_ref, wkr_ref, tabc_ref, tabs_ref, qn_ref, kvn_ref,
                   cqn_ref, ckvn_ref, kpe_ref):
    cqn_ref[...] = _rms(cq_ref[...].astype(F32), qn_ref[...]).astype(BF16)
    ckvn_ref[...] = _rms(ckv_ref[...].astype(F32), kvn_ref[...]).astype(BF16)
    kr = _dot_nt(x_ref[...], _bf(wkr_ref[...]))
    lane = lax.broadcasted_iota(jnp.int32, kr.shape, 1)
    half = A_ROPE // 2
    swapped = jnp.where(lane < half, pltpu.roll(kr, LANES - half, axis=1), pltpu.roll(kr, half, axis=1))
    kpe_ref[...] = (kr * tabc_ref[...] + swapped * tabs_ref[...]).astype(BF16)


def a_norm(z, hn, w_in_t, tab_kc, tab_ks, q_norm, kv_norm, tm=512):
    t, d = hn.shape
    kr_block = (A_Q_RANK + A_KV_RANK) // LANES
    return pl.pallas_call(
        _a_norm_kernel,
        grid=(t // tm,),
        in_specs=[pl.BlockSpec((tm, A_Q_RANK), lambda i: (i, Z_CQ // A_Q_RANK)),
                  pl.BlockSpec((tm, A_KV_RANK), lambda i: (i, Z_CKV // A_KV_RANK)),
                  pl.BlockSpec((tm, d), lambda i: (i, 0)),
                  pl.BlockSpec((LANES, d), lambda i: (kr_block, 0)),
                  pl.BlockSpec((tm, LANES), lambda i: (i, 0)),
                  pl.BlockSpec((tm, LANES), lambda i: (i, 0)),
                  pl.BlockSpec((1, A_Q_RANK), lambda i: (0, 0)),
                  pl.BlockSpec((1, A_KV_RANK), lambda i: (0, 0))],
        out_specs=[pl.BlockSpec((tm, A_Q_RANK), lambda i: (i, 0)),
                   pl.BlockSpec((tm, A_KV_RANK), lambda i: (i, 0)),
                   pl.BlockSpec((tm, LANES), lambda i: (i, 0))],
        out_shape=[jax.ShapeDtypeStruct((t, A_Q_RANK), BF16),
                   jax.ShapeDtypeStruct((t, A_KV_RANK), BF16),
                   jax.ShapeDtypeStruct((t, LANES), BF16)],
        compiler_params=_params(("arbitrary",)),
        name="a_norm",
    )(z, z, hn, w_in_t, tab_kc, tab_ks, q_norm.reshape(1, -1), kv_norm.reshape(1, -1))


def _q_proj_kernel(x_ref, w_ref, tab_ref, o_ref, *, scale):
    acc = _dot(x_ref[...], w_ref[0])
    nope = acc[:, :A_NOPE] * scale
    pe = _rope64(acc[:, A_NOPE:], tab_ref[...]) * scale
    o_ref[0, 0] = jnp.concatenate([nope, pe], axis=1).astype(o_ref.dtype)


def q_proj(cqn, wq_r, tab64, batch, seq, tm=1024):
    nsb = seq // tm
    scale = float((A_NOPE + A_ROPE) ** -0.5)
    return pl.pallas_call(
        functools.partial(_q_proj_kernel, scale=scale),
        grid=(batch * nsb, A_HEADS),
        in_specs=[pl.BlockSpec((tm, A_Q_RANK), lambda i, h: (i, 0)),
                  pl.BlockSpec((1, A_Q_RANK, 2 * LANES), lambda i, h: (h, 0, 0)),
                  pl.BlockSpec((tm, LANES), lambda i, h: (i, 0))],
        out_specs=pl.BlockSpec((1, 1, tm, 2 * LANES), lambda i, h: (i // nsb, h, i % nsb, 0)),
        out_shape=jax.ShapeDtypeStruct((batch, A_HEADS, seq, 2 * LANES), BF16),
        compiler_params=_params(("arbitrary", "arbitrary")),
        name="q_proj",
    )(cqn, wq_r, tab64)


def _kv_proj_kernel(x_ref, w_ref, kpe_ref, k_ref, v_ref):
    acc = _dot(x_ref[...], _bf(w_ref[...]))
    k_ref[0, 0] = jnp.concatenate([acc[:, :A_NOPE].astype(BF16), kpe_ref[...]], axis=1)
    v_ref[0, 0] = acc[:, A_NOPE:].astype(BF16)


def kv_proj(ckvn, w_ukv, kpe, batch, seq, tm=1024):
    nsb = seq // tm
    hw = A_NOPE + A_V
    return pl.pallas_call(
        _kv_proj_kernel,
        grid=(batch * nsb, A_HEADS),
        in_specs=[pl.BlockSpec((tm, A_KV_RANK), lambda i, h: (i, 0)),
                  pl.BlockSpec((A_KV_RANK, hw), lambda i, h: (0, h)),
                  pl.BlockSpec((tm, LANES), lambda i, h: (i, 0))],
        out_specs=[pl.BlockSpec((1, 1, tm, 2 * LANES), lambda i, h: (i // nsb, h, i % nsb, 0)),
                   pl.BlockSpec((1, 1, tm, A_V), lambda i, h: (i // nsb, h, i % nsb, 0))],
        out_shape=[jax.ShapeDtypeStruct((batch, A_HEADS, seq, 2 * LANES), BF16),
                   jax.ShapeDtypeStruct((batch, A_HEADS, seq, A_V), BF16)],
        compiler_params=_params(("arbitrary", "arbitrary")),
        name="kv_proj",
    )(ckvn, w_ukv, kpe)


def _flash_kernel(q_ref, k_ref, v_ref, o_ref, *, tq, tk, hp):
    qi = pl.program_id(2)
    dv = v_ref.shape[-1]

    def step_one(h, off, width, carry, masked):
        m, l, acc = carry
        s = _dot_nt(q_ref[0, h], k_ref[0, h, pl.ds(off, width), :])
        if masked:
            row = lax.broadcasted_iota(jnp.int32, s.shape, 0)
            col = lax.broadcasted_iota(jnp.int32, s.shape, 1)
            s = jnp.where(col <= row, s, NEG_BIG)
        m_new = jnp.maximum(m, jnp.max(s, axis=-1, keepdims=True))
        alpha = jnp.exp(m - m_new)
        p = jnp.exp(s - m_new)
        l = alpha * l + jnp.sum(p, axis=-1, keepdims=True)
        acc = alpha * acc + _dot(p.astype(BF16), v_ref[0, h, pl.ds(off, width), :])
        return m_new, l, acc

    def step(off, width, carry, masked):
        return tuple(step_one(h, off, width, carry[h], masked) for h in range(hp))

    def body(j, carry):
        return step(pl.multiple_of(j * tk, tk), tk, carry, False)

    init = (jnp.full((tq, 1), NEG_BIG, F32), jnp.zeros((tq, 1), F32), jnp.zeros((tq, dv), F32))
    carry = (init,) * hp
    n_wide = (qi * tq) // tk
    carry = lax.fori_loop(0, n_wide, body, carry)
    if tk > tq:
        carry = lax.cond(qi * tq > n_wide * tk,
                         lambda c: step(pl.multiple_of(n_wide * tk, tq), tq, c, False),
                         lambda c: c, carry)
    carry = step(pl.multiple_of(qi * tq, tq), tq, carry, True)
    o_ref[...] = jnp.concatenate([acc / l for _, l, acc in carry], axis=1).astype(o_ref.dtype)


def causal_attention(q, k, v, tq=512, tk=1024, hp=4):
    b, h, s, dk = q.shape
    dv = v.shape[-1]
    nq = s // tq
    assert tk in (tq, 2 * tq) and s % tk == 0 and h % hp == 0
    return pl.pallas_call(
        functools.partial(_flash_kernel, tq=tq, tk=tk, hp=hp),
        grid=(b, h // hp, nq),
        in_specs=[pl.BlockSpec((1, hp, tq, dk), lambda bi, hi, qi: (bi, hi, qi, 0)),
                  pl.BlockSpec((1, hp, s, dk), lambda bi, hi, qi: (bi, hi, 0, 0)),
                  pl.BlockSpec((1, hp, s, dv), lambda bi, hi, qi: (bi, hi, 0, 0))],
        out_specs=pl.BlockSpec((tq, hp * dv), lambda bi, hi, qi: (bi * nq + qi, hi)),
        out_shape=jax.ShapeDtypeStruct((b * s, h * dv), BF16),
        compiler_params=_params(("arbitrary", "arbitrary", "arbitrary")),
        name="causal_attention",
    )(q, k, v)


def _rope_half(x, cos, sin):
    half = x.shape[1] // 2
    x1, x2 = x[:, :half], x[:, half:]
    return jnp.concatenate([x1 * cos - x2 * sin, x1 * sin + x2 * cos], axis=1)


def _retention_kernel(cdec_ref, q_ref, k_ref, v_ref, g_ref, cos_ref, sin_ref, intra_ref, qdec_ref,
                      kdec_ref, rn_ref, o_ref, state_sc, *, n_chunks):
    hi = pl.program_id(1)

    @pl.when(pl.program_id(2) == 0)
    def _():
        state_sc[...] = jnp.zeros_like(state_sc)

    chunk_decay = cdec_ref[hi]
    intra = intra_ref[0]
    q_decay = qdec_ref[0]
    k_decay = kdec_ref[0]
    rn = rn_ref[...]
    k_scale = float(B_QK ** -0.5)
    for c in range(n_chunks):
        sl = slice(c * RET_CHUNK, (c + 1) * RET_CHUNK)
        cos = cos_ref[sl, :]
        sin = sin_ref[sl, :]
        qr = _rope_half(q_ref[sl, :].astype(F32), cos, sin)
        kr = _rope_half(k_ref[sl, :].astype(F32), cos, sin) * k_scale
        v = v_ref[sl, :]
        qb = qr.astype(BF16)
        s = _dot_nt(qb, kr.astype(BF16)) * intra
        inner = _dot(s.astype(BF16), v)
        state = state_sc[...]
        cross = _dot(qb, state.astype(BF16)) * q_decay
        state_sc[...] = state * chunk_decay + _dot_tn((kr * k_decay).astype(BF16), v)
        y = inner + cross
        mu = jnp.mean(y, axis=-1, keepdims=True)
        yc = y - mu
        var = jnp.mean(yc * yc, axis=-1, keepdims=True)
        yn = yc * lax.rsqrt(var + EPS) * rn
        o_ref[sl, :] = (yn * jax.nn.silu(g_ref[sl, :].astype(F32))).astype(o_ref.dtype)


def retention(z, cos, sin, ret_norm, batch, seq, rows=1024):
    t = z.shape[0]
    nr = seq // rows
    hh = jnp.arange(B_HEADS, dtype=F32)
    log_g = jnp.log1p(-jnp.exp2(-5.0 - hh))
    idx = jnp.arange(RET_CHUNK, dtype=F32)
    diff = idx[:, None] - idx[None, :]
    intra = jnp.where(diff >= 0, jnp.exp(log_g[:, None, None] * jnp.maximum(diff, 0.0)), 0.0)
    ones = jnp.ones((1, 1, B_QK), F32)
    q_decay = jnp.exp(log_g[:, None] * (idx + 1.0))[:, :, None] * ones
    k_decay = jnp.exp(log_g[:, None] * (RET_CHUNK - 1.0 - idx))[:, :, None] * ones
    chunk_decay = jnp.exp(log_g * RET_CHUNK)

    def zspec(off):
        return pl.BlockSpec((rows, B_QK), lambda b, h, r, cd: (b * nr + r, off // B_QK + h))

    tab = pl.BlockSpec((rows, B_QK // 2), lambda b, h, r, cd: (b * nr + r, 0))
    per_head = pl.BlockSpec((1, RET_CHUNK, B_QK), lambda b, h, r, cd: (h, 0, 0))
    grid_spec = pltpu.PrefetchScalarGridSpec(
        num_scalar_prefetch=1,
        grid=(batch, B_HEADS, nr),
        in_specs=[zspec(Z_RQ), zspec(Z_RK), zspec(Z_RV), zspec(Z_RG), tab, tab,
                  pl.BlockSpec((1, RET_CHUNK, RET_CHUNK), lambda b, h, r, cd: (h, 0, 0)),
                  per_head, per_head,
                  pl.BlockSpec((1, B_V), lambda b, h, r, cd: (0, h))],
        out_specs=pl.BlockSpec((rows, B_V), lambda b, h, r, cd: (b * nr + r, h)),
        scratch_shapes=[pltpu.VMEM((B_QK, B_V), F32)],
    )
    return pl.pallas_call(
        functools.partial(_retention_kernel, n_chunks=rows // RET_CHUNK),
        grid_spec=grid_spec,
        out_shape=jax.ShapeDtypeStruct((t, B_HEADS * B_V), BF16),
        compiler_params=_params(("arbitrary", "arbitrary", "arbitrary")),
        name="retention",
    )(chunk_decay, z, z, z, z, cos, sin, intra, q_decay, k_decay, ret_norm.reshape(1, -1))


def _sgu_kernel(u_ref, v_ref, vn_ref, ws_ref, bs_ref, o_ref, *, n_chunks):
    gw = C_WIDTH // C_GROUPS
    v = v_ref[...].astype(F32)
    mu = jnp.mean(v, axis=-1, keepdims=True)
    vc = v - mu
    var = jnp.mean(vc * vc, axis=-1, keepdims=True)
    vn = (vc * lax.rsqrt(var + EPS) * vn_ref[...]).astype(BF16)
    r = lax.broadcasted_iota(jnp.int32, (C_CHUNK, C_CHUNK), 0)
    c = lax.broadcasted_iota(jnp.int32, (C_CHUNK, C_CHUNK), 1)
    bs = bs_ref[...]
    for g in range(C_GROUPS):
        ws = jnp.where(c <= r, ws_ref[g], 0.0).astype(BF16)
        bias = bs[:, g:g + 1]
        cols = slice(g * gw, (g + 1) * gw)
        for ch in range(n_chunks):
            rows = slice(ch * C_CHUNK, (ch + 1) * C_CHUNK)
            mixed = _dot(ws, vn[rows, cols]) + bias
            o_ref[rows, cols] = (u_ref[rows, cols].astype(F32) * mixed).astype(o_ref.dtype)


def spatial_gate(z, v_norm, w_s, b_s, rows=512):
    t = z.shape[0]
    return pl.pallas_call(
        functools.partial(_sgu_kernel, n_chunks=rows // C_CHUNK),
        grid=(t // rows,),
        in_specs=[pl.BlockSpec((rows, C_WIDTH), lambda i: (i, 0)),
                  pl.BlockSpec((rows, C_WIDTH), lambda i: (i, 1)),
                  pl.BlockSpec((1, C_WIDTH), lambda i: (0, 0)),
                  pl.BlockSpec((C_GROUPS, C_CHUNK, C_CHUNK), lambda i: (0, 0, 0)),
                  pl.BlockSpec((C_CHUNK, C_GROUPS), lambda i: (0, 0))],
        out_specs=pl.BlockSpec((rows, C_WIDTH), lambda i: (i, 0)),
        out_shape=jax.ShapeDtypeStruct((t, C_WIDTH), BF16),
        compiler_params=_params(("arbitrary",)),
        name="spatial_gate",
    )(z, z, v_norm.reshape(1, -1), w_s, b_s.T)


def _pack_bf16_pair(lo, hi):
    lo_bits = lax.bitcast_convert_type(lo.astype(BF16).astype(F32), jnp.uint32)
    hi_bits = lax.bitcast_convert_type(hi.astype(BF16).astype(F32), jnp.uint32)
    return (lo_bits >> 16) | (hi_bits & jnp.uint32(0xFFFF0000))


def _unpack_bf16_pair(words):
    lo = lax.bitcast_convert_type(words << 16, F32).astype(BF16)
    hi = lax.bitcast_convert_type(words & jnp.uint32(0xFFFF0000), F32).astype(BF16)
    return lo, hi


def _xattn_kernel(hmix_ref, gmix_ref, x_ref, gpre_ref, wq_ref, k_ref, v_ref, wo_ref, gpost_ref, gnext_ref,
                  xo_ref, hn_ref, *, pack_next):
    x = x_ref[...] + _rms(hmix_ref[...], gmix_ref[...])
    hn = _rms(x, gpre_ref[...]).astype(BF16)
    q = (_dot(hn, wq_ref[...]) * float(X_HEAD_DIM ** -0.5)).astype(BF16)
    k = k_ref[...]
    v = v_ref[...]
    outs = []
    for h in range(X_HEADS):
        cols = slice(h * X_HEAD_DIM, (h + 1) * X_HEAD_DIM)
        s = _dot_nt(q[:, cols], k[:, cols])
        p = jnp.exp(s - jnp.max(s, axis=-1, keepdims=True))
        o = _dot(p.astype(BF16), v[:, cols]) / jnp.sum(p, axis=-1, keepdims=True)
        outs.append(o.astype(BF16))
    hh = _dot(jnp.concatenate(outs, axis=1), wo_ref[...])
    xn = x + _rms(hh, gpost_ref[...])
    xo_ref[...] = xn
    hn = _rms(xn, gnext_ref[...])
    if pack_next:
        half = hn.shape[1] // 2
        hn_ref[...] = _pack_bf16_pair(hn[:, :half], hn[:, half:])
    else:
        hn_ref[...] = hn.astype(hn_ref.dtype)


def cross_attention(h_mix, g_mix, x, kv, wq, wo, g_pre, g_post, g_next, seq, pack_next=False, tm=256):
    t, d = x.shape
    hd = X_HEADS * X_HEAD_DIM
    nsb = seq // tm
    row = pl.BlockSpec((tm, d), lambda i: (i, 0))
    vec = pl.BlockSpec((1, d), lambda i: (0, 0))
    if pack_next:
        hn_spec = pl.BlockSpec((tm, d // 2), lambda i: (i, 0))
        hn_shape = jax.ShapeDtypeStruct((t, d // 2), jnp.uint32)
    else:
        hn_spec = row
        hn_shape = jax.ShapeDtypeStruct((t, d), BF16)
    return pl.pallas_call(
        functools.partial(_xattn_kernel, pack_next=pack_next),
        grid=(t // tm,),
        in_specs=[row, vec, row, vec,
                  pl.BlockSpec((d, hd), lambda i: (0, 0)),
                  pl.BlockSpec((N_MEM, hd), lambda i: (i // nsb, 0)),
                  pl.BlockSpec((N_MEM, hd), lambda i: (i // nsb, 1)),
                  pl.BlockSpec((hd, d), lambda i: (0, 0)),
                  vec, vec],
        out_specs=[row, hn_spec],
        out_shape=[jax.ShapeDtypeStruct((t, d), F32), hn_shape],
        compiler_params=_params(("arbitrary",)),
        name="cross_attention",
    )(h_mix, g_mix.reshape(1, d), x, g_pre.reshape(1, d), wq, kv, kv, wo, g_post.reshape(1, d),
      g_next.reshape(1, d))


def _router_kernel(x_ref, g_ref, wt_ref, idx_ref, gate_ref):
    hn = _rms(x_ref[...], g_ref[...])
    wt = wt_ref[...]
    h_hi = hn.astype(BF16)
    h_lo = (hn - h_hi.astype(F32)).astype(BF16)
    w_hi = wt.astype(BF16)
    w_lo = (wt - w_hi.astype(F32)).astype(BF16)
    logits = _dot_nt(w_hi, h_hi) + (_dot_nt(w_hi, h_lo) + _dot_nt(w_lo, h_hi))
    e = lax.broadcasted_iota(jnp.int32, logits.shape, 0)
    m1 = jnp.max(logits, axis=0, keepdims=True)
    i1 = jnp.min(jnp.where(logits == m1, e, N_EXPERTS), axis=0, keepdims=True)
    rest = jnp.where(e == i1, -jnp.inf, logits)
    m2 = jnp.max(rest, axis=0, keepdims=True)
    i2 = jnp.min(jnp.where(rest == m2, e, N_EXPERTS), axis=0, keepdims=True)
    ex = jnp.exp(m2 - m1)
    g1 = 1.0 / (1.0 + ex)
    idx_ref[...] = jnp.concatenate([i1, i2], axis=0)
    gate_ref[...] = jnp.concatenate([g1, ex * g1], axis=0)


def router(x, g, w_router, tm=512):
    t, d = x.shape
    return pl.pallas_call(
        _router_kernel,
        grid=(t // tm,),
        in_specs=[pl.BlockSpec((tm, d), lambda i: (i, 0)),
                  pl.BlockSpec((1, d), lambda i: (0, 0)),
                  pl.BlockSpec((N_EXPERTS, d), lambda i: (0, 0))],
        out_specs=[pl.BlockSpec((2, tm), lambda i: (0, i)),
                   pl.BlockSpec((2, tm), lambda i: (0, i))],
        out_shape=[jax.ShapeDtypeStruct((2, t), jnp.int32), jax.ShapeDtypeStruct((2, t), F32)],
        compiler_params=_params(("arbitrary",)),
        name="router",
    )(x, g.reshape(1, d), w_router.T)


def _live_col(tile_sub, t, j, n_cols):
    return jnp.where(tile_sub[t] > 0, j, n_cols - 1)


def _moe_gu_kernel(te_ref, tx_ref, tq_ref, x_ref, wg_ref, wu_ref, o_ref):
    n_sub = tq_ref[pl.program_id(0)]
    tm = o_ref.shape[0]
    half = wg_ref.shape[1] // 2

    for q in range(1, tm // MOE_SUB + 1):
        rows = q * MOE_SUB

        @pl.when(n_sub == q)
        def _(rows=rows):
            x_lo, x_hi = _unpack_bf16_pair(x_ref[:rows, :])
            g = _dot(x_lo, _bf(wg_ref[0, :half, :])) + _dot(x_hi, _bf(wg_ref[0, half:, :]))
            u = _dot(x_lo, _bf(wu_ref[0, :half, :])) + _dot(x_hi, _bf(wu_ref[0, half:, :]))
            o_ref[:rows, :] = (jax.nn.silu(g) * u).astype(o_ref.dtype)
            if rows < tm:
                o_ref[rows:, :] = jnp.zeros((tm - rows, o_ref.shape[1]), o_ref.dtype)

    @pl.when(n_sub == 0)
    def _():
        o_ref[...] = jnp.zeros_like(o_ref)


def moe_gate_up(xs, w_gu, tile_e, tile_x, tile_sub, tn=256):
    p = xs.shape[0]
    d = w_gu.shape[1]
    f = w_gu.shape[2] // 2
    nf = f // tn
    nt = p // MOE_TM
    grid_spec = pltpu.PrefetchScalarGridSpec(
        num_scalar_prefetch=3,
        grid=(nt, nf),
        in_specs=[pl.BlockSpec((MOE_TM, d // 2), lambda t, j, te, tx, ts: (tx[t], 0)),
                  pl.BlockSpec((1, d, tn), lambda t, j, te, tx, ts: (te[t], 0, _live_col(ts, t, j, nf))),
                  pl.BlockSpec((1, d, tn), lambda t, j, te, tx, ts: (te[t], 0, _live_col(ts, t, j, nf) + nf))],
        out_specs=pl.BlockSpec((MOE_TM, tn), lambda t, j, te, tx, ts: (t, j)),
    )
    return pl.pallas_call(
        _moe_gu_kernel,
        grid_spec=grid_spec,
        out_shape=jax.ShapeDtypeStruct((p, f), BF16),
        compiler_params=_params(("arbitrary", "arbitrary")),
        name="moe_gate_up",
    )(tile_e, tile_x, tile_sub, xs, w_gu, w_gu)


def _moe_down_kernel(te_ref, tx_ref, tq_ref, a_ref, w_ref, o_ref):
    n_sub = tq_ref[pl.program_id(0)]
    tm = o_ref.shape[0]

    for q in range(1, tm // MOE_SUB + 1):
        rows = q * MOE_SUB

        @pl.when(n_sub == q)
        def _(rows=rows):
            o_ref[:rows, :] = _dot(a_ref[:rows, :], _bf(w_ref[0]))
            if rows < tm:
                o_ref[rows:, :] = jnp.zeros((tm - rows, o_ref.shape[1]), o_ref.dtype)

    @pl.when(n_sub == 0)
    def _():
        o_ref[...] = jnp.zeros_like(o_ref)


def moe_down(act, w_down, tile_e, tile_x, tile_sub, tn=512):
    p, f = act.shape
    d = w_down.shape[2]
    nt = p // MOE_TM
    grid_spec = pltpu.PrefetchScalarGridSpec(
        num_scalar_prefetch=3,
        grid=(nt, d // tn),
        in_specs=[pl.BlockSpec((MOE_TM, f), lambda t, j, te, tx, ts: (tx[t], 0)),
                  pl.BlockSpec((1, f, tn), lambda t, j, te, tx, ts: (te[t], 0, _live_col(ts, t, j, d // tn)))],
        out_specs=pl.BlockSpec((MOE_TM, tn), lambda t, j, te, tx, ts: (t, j)),
    )
    return pl.pallas_call(
        _moe_down_kernel,
        grid_spec=grid_spec,
        out_shape=jax.ShapeDtypeStruct((p, d), F32),
        compiler_params=_params(("arbitrary", "arbitrary")),
        name="moe_down",
    )(tile_e, tile_x, tile_sub, act, w_down)


def moe_layer(x, g_pre, hn_packed, w_router, w_gu, w_down):
    t = x.shape[0]
    idx, gates = router(x, g_pre, w_router)
    experts = jnp.arange(N_EXPERTS, dtype=jnp.int32)
    e_flat = idx.reshape(-1)
    csum = jnp.cumsum((e_flat[:, None] == experts[None, :]).astype(jnp.int32), axis=0)
    rank = jnp.take_along_axis(csum, e_flat[:, None], axis=1)[:, 0] - 1
    count = csum[-1]
    n_tiles = (count + MOE_TM - 1) // MOE_TM
    tile_end = jnp.cumsum(n_tiles)
    tile_start = tile_end - n_tiles
    pos = tile_start[e_flat] * MOE_TM + rank
    nt = 2 * t // MOE_TM + N_EXPERTS
    p = nt * MOE_TM
    row_tok = (jnp.arange(p, dtype=jnp.int32) % t).at[pos].set(jnp.tile(jnp.arange(t, dtype=jnp.int32), 2))
    n_used = tile_end[-1]
    tile_id = jnp.arange(nt, dtype=jnp.int32)
    tile_x = jnp.minimum(tile_id, n_used - 1)
    tile_e = jnp.sum((tile_end[None, :] <= tile_x[:, None]).astype(jnp.int32), axis=1)
    tile_e = jnp.minimum(tile_e, N_EXPERTS - 1)
    rows_left = count[tile_e] - (tile_x - tile_start[tile_e]) * MOE_TM
    tile_sub = (jnp.clip(rows_left, 0, MOE_TM) + MOE_SUB - 1) // MOE_SUB
    tile_sub = jnp.where(tile_id < n_used, tile_sub, 0).astype(jnp.int32)
    xs = hn_packed.at[row_tok].get(mode="promise_in_bounds")
    act = moe_gate_up(xs, w_gu, tile_e, tile_x, tile_sub)
    y = moe_down(act, w_down, tile_e, tile_x, tile_sub)
    y0 = y.at[pos[:t]].get(mode="promise_in_bounds")
    y1 = y.at[pos[t:]].get(mode="promise_in_bounds")
    return y0, y1, gates.T


def _rope_tables(positions):
    pos = positions.reshape(-1).astype(F32)
    inv_a = ROPE_BASE ** (-jnp.arange(0, A_ROPE, 2, dtype=F32) / A_ROPE)
    ang_a = pos[:, None] * inv_a
    cos_a, sin_a = jnp.cos(ang_a), jnp.sin(ang_a)
    zero = jnp.zeros_like(cos_a)
    tab64 = jnp.concatenate([cos_a, cos_a, -sin_a, sin_a], axis=1)
    tab_kc = jnp.concatenate([cos_a, cos_a, zero, zero], axis=1)
    tab_ks = jnp.concatenate([-sin_a, sin_a, zero, zero], axis=1)
    inv_b = ROPE_BASE ** (-jnp.arange(0, B_QK, 2, dtype=F32) / B_QK)
    ang_b = pos[:, None] * inv_b
    return tab64, tab_kc, tab_ks, jnp.cos(ang_b), jnp.sin(ang_b)


def _swap_halves(w):
    half = w.shape[-1] // 2
    return jnp.concatenate([w[..., half:], w[..., :half]], axis=-1)


def kernel(x, mem, positions, norm_mix_pre, norm_mix_post, norm_x_pre, norm_x_post, norm_ffn_pre, norm_ffn_post, norm_mem, x_w_q, x_w_kv, x_w_o, ab_w_in, ab_q_norm, ab_w_uq, ab_kv_norm, ab_w_ukv, ab_ret_norm, ab_w_out, ffn_w_gu, ffn_w_down, c_w_in, c_v_norm, c_w_s, c_b_s, c_w_out, moe_router, moe_w_gu, moe_w_down):
    batch, seq, d = x.shape
    t = batch * seq
    x = x.reshape(t, d)
    mem2 = mem.reshape(batch * N_MEM, d)
    tab64, tab_kc, tab_ks, cos_b, sin_b = _rope_tables(positions)

    w_uq = ab_w_uq[0].reshape(A_Q_RANK, A_HEADS, A_NOPE + A_ROPE)
    w_pe = w_uq[:, :, A_NOPE:]
    wq_r = jnp.concatenate([w_uq[:, :, :A_NOPE], w_pe, _swap_halves(w_pe)], axis=-1)
    wq_r = wq_r.transpose(1, 0, 2).astype(BF16)

    hn = rms_norm_bf16(x, norm_mix_pre[0])
    w_in_t = ab_w_in[0].T
    z = ab_in_proj(hn, w_in_t)
    cqn, ckvn, kpe = a_norm(z, hn, w_in_t, tab_kc, tab_ks, ab_q_norm[0], ab_kv_norm[0])
    q = q_proj(cqn, wq_r, tab64, batch, seq)
    k, v = kv_proj(ckvn, ab_w_ukv[0], kpe, batch, seq)
    ya = causal_attention(q, k, v)
    yb = retention(z, cos_b, sin_b, ab_ret_norm[0], batch, seq)
    h = matmul_cat(ya, yb, ab_w_out[0], F32, tm=1024, tn=512, name="ab_out_proj")
    memn = rms_norm_bf16(mem2, norm_mem[0])
    kv = matmul(memn, x_w_kv[0], BF16, tm=512, tn=512, name="mem_kv_proj")
    x, hn = cross_attention(h, norm_mix_post[0], x, kv, x_w_q[0].astype(BF16), x_w_o[0].astype(BF16),
                            norm_x_pre[0], norm_x_post[0], norm_ffn_pre[0], seq)
    act = matmul_swiglu(hn, ffn_w_gu[0], tm=1024, tn=256)
    h = matmul_ksplit(act, ffn_w_down[0], tm=1024, tn=2048, tk=1024)
    x, hn = resid_norm([h], x, norm_ffn_post[0], norm_mix_pre[1])

    zc = matmul(hn, c_w_in[0], BF16, tm=1024, tn=512, act="gelu", name="c_in_proj")
    yc = spatial_gate(zc, c_v_norm[0], c_w_s[0], c_b_s[0])
    h = matmul(yc, c_w_out[0], F32, tm=1024, tn=512, name="c_out_proj")
    memn = rms_norm_bf16(mem2, norm_mem[1])
    kv = matmul(memn, x_w_kv[1], BF16, tm=512, tn=512, name="mem_kv_proj")
    x, hn_packed = cross_attention(h, norm_mix_post[1], x, kv, x_w_q[1].astype(BF16), x_w_o[1].astype(BF16),
                                   norm_x_pre[1], norm_x_post[1], norm_ffn_pre[1], seq, pack_next=True)
    y0, y1, gates = moe_layer(x, norm_ffn_pre[1], hn_packed, moe_router[0], moe_w_gu[0], moe_w_down[0])
    x = resid_norm([y0, y1], x, norm_ffn_post[1], weights=gates)
    return x.reshape(batch, seq, d)
```

```python
import functools

import jax
import jax.numpy as jnp
from jax import lax
from jax.experimental import pallas as pl
from jax.experimental.pallas import tpu as pltpu

F32 = jnp.float32
BF16 = jnp.bfloat16

D_MODEL = 4096
N_MEM = 256
EPS = 1e-6
ROPE_BASE = 10000.0
A_HEADS = 16
A_Q_RANK = 1024
A_KV_RANK = 512
A_NOPE = 128
A_ROPE = 64
A_V = 128
B_HEADS = 8
B_QK = 256
B_V = 256
RET_CHUNK = 128
C_GROUPS = 8
C_WIDTH = 4096
C_CHUNK = 128
X_HEADS = 4
X_HEAD_DIM = 128
D_FF = 14336
N_EXPERTS = 8
D_EXPERT = 3584

LANES = 128
VMEM_LIMIT = 56 * 1024 * 1024
NEG_BIG = -1e30

Z_TILE = 1024
Z_CQ = 0
Z_CKV = A_Q_RANK
Z_RQ = -(-(A_Q_RANK + A_KV_RANK) // Z_TILE) * Z_TILE
Z_RK = Z_RQ + B_HEADS * B_QK
Z_RV = Z_RK + B_HEADS * B_QK
Z_RG = Z_RV + B_HEADS * B_V
Z_W = Z_RG + B_HEADS * B_V
A_IN = A_Q_RANK + A_KV_RANK + A_ROPE

MOE_TM = 1024
MOE_SUB = 256


def _params(sem, vmem=VMEM_LIMIT):
    return pltpu.CompilerParams(dimension_semantics=sem, vmem_limit_bytes=vmem)


def _rms(xf, g):
    ms = jnp.mean(xf * xf, axis=-1, keepdims=True)
    return xf * lax.rsqrt(ms + EPS) * g


def _bf(w):
    return w if w.dtype == BF16 else w.astype(BF16)


def _dot(a, b):
    return jnp.dot(a, b, preferred_element_type=F32)


def _dot_nt(a, b):
    return lax.dot_general(a, b, (((1,), (1,)), ((), ())), preferred_element_type=F32)


def _dot_tn(a, b):
    return lax.dot_general(a, b, (((0,), (0,)), ((), ())), preferred_element_type=F32)


def _norm_kernel(x_ref, g_ref, o_ref):
    o_ref[...] = _rms(x_ref[...].astype(F32), g_ref[...]).astype(o_ref.dtype)


def rms_norm_bf16(x, g, tm=512):
    m, d = x.shape
    return pl.pallas_call(
        _norm_kernel,
        grid=(m // tm,),
        in_specs=[pl.BlockSpec((tm, d), lambda i: (i, 0)),
                  pl.BlockSpec((1, d), lambda i: (0, 0))],
        out_specs=pl.BlockSpec((tm, d), lambda i: (i, 0)),
        out_shape=jax.ShapeDtypeStruct((m, d), BF16),
        compiler_params=_params(("arbitrary",)),
        name="rms_norm",
    )(x, g.reshape(1, d))


def _resid_kernel(*refs, n_h, gated, with_next):
    h_refs, rest = refs[:n_h], refs[n_h:]
    if gated:
        w = rest[0][...]
        rest = rest[1:]
        h = h_refs[0][...] * w[:, 0:1]
        for c, r in enumerate(h_refs[1:], start=1):
            h = h + r[...] * w[:, c:c + 1]
    else:
        h = h_refs[0][...]
        for r in h_refs[1:]:
            h = h + r[...]
    x_ref, g_ref = rest[0], rest[1]
    xn = x_ref[...] + _rms(h, g_ref[...])
    if with_next:
        gn_ref, xo_ref, hn_ref = rest[2:]
        xo_ref[...] = xn
        hn_ref[...] = _rms(xn, gn_ref[...]).astype(hn_ref.dtype)
    else:
        xo_ref = rest[2]
        xo_ref[...] = xn


def resid_norm(hs, x, g_post, g_next=None, weights=None, tm=256):
    m, d = x.shape
    row = pl.BlockSpec((tm, d), lambda i: (i, 0))
    vec = pl.BlockSpec((1, d), lambda i: (0, 0))
    with_next = g_next is not None
    gated = weights is not None
    in_specs = [row] * len(hs) + ([pl.BlockSpec((tm, len(hs)), lambda i: (i, 0))] if gated else [])
    in_specs += [row, vec] + ([vec] if with_next else [])
    args = list(hs) + ([weights] if gated else [])
    args += [x, g_post.reshape(1, d)] + ([g_next.reshape(1, d)] if with_next else [])
    out_shape = [jax.ShapeDtypeStruct((m, d), F32)]
    out_specs = [row]
    if with_next:
        out_shape.append(jax.ShapeDtypeStruct((m, d), BF16))
        out_specs.append(row)
    res = pl.pallas_call(
        functools.partial(_resid_kernel, n_h=len(hs), gated=gated, with_next=with_next),
        grid=(m // tm,),
        in_specs=in_specs,
        out_specs=out_specs,
        out_shape=out_shape,
        compiler_params=_params(("arbitrary",)),
        name="resid_norm",
    )(*args)
    return res if with_next else res[0]


def _mm_kernel(x_ref, w_ref, o_ref, *, act):
    acc = _dot(x_ref[...], _bf(w_ref[...]))
    if act == "gelu":
        acc = jax.nn.gelu(acc)
    o_ref[...] = acc.astype(o_ref.dtype)


def matmul(x, w, out_dtype, tm, tn, act=None, name="matmul"):
    m, k = x.shape
    n = w.shape[1]
    return pl.pallas_call(
        functools.partial(_mm_kernel, act=act),
        grid=(n // tn, m // tm),
        in_specs=[pl.BlockSpec((tm, k), lambda j, i: (i, 0)),
                  pl.BlockSpec((k, tn), lambda j, i: (0, j))],
        out_specs=pl.BlockSpec((tm, tn), lambda j, i: (i, j)),
        out_shape=jax.ShapeDtypeStruct((m, n), out_dtype),
        compiler_params=_params(("arbitrary", "arbitrary")),
        name=name,
    )(x, w)


def _mm2_kernel(xa_ref, xb_ref, w_ref, o_ref):
    ka = xa_ref.shape[1]
    acc = _dot(xa_ref[...], _bf(w_ref[:ka, :]))
    acc = acc + _dot(xb_ref[...], _bf(w_ref[ka:, :]))
    o_ref[...] = acc.astype(o_ref.dtype)


def matmul_cat(xa, xb, w, out_dtype, tm, tn, name="matmul_cat"):
    m, ka = xa.shape
    kb = xb.shape[1]
    n = w.shape[1]
    return pl.pallas_call(
        _mm2_kernel,
        grid=(n // tn, m // tm),
        in_specs=[pl.BlockSpec((tm, ka), lambda j, i: (i, 0)),
                  pl.BlockSpec((tm, kb), lambda j, i: (i, 0)),
                  pl.BlockSpec((ka + kb, tn), lambda j, i: (0, j))],
        out_specs=pl.BlockSpec((tm, tn), lambda j, i: (i, j)),
        out_shape=jax.ShapeDtypeStruct((m, n), out_dtype),
        compiler_params=_params(("arbitrary", "arbitrary")),
        name=name,
    )(xa, xb, w)


def _swiglu_kernel(x_ref, wg_ref, wu_ref, o_ref):
    x = x_ref[...]
    g = _dot(x, _bf(wg_ref[...]))
    u = _dot(x, _bf(wu_ref[...]))
    o_ref[...] = (jax.nn.silu(g) * u).astype(o_ref.dtype)


def matmul_swiglu(x, w_gu, tm, tn, name="ffn_gate_up"):
    m, k = x.shape
    f = w_gu.shape[1] // 2
    nf = f // tn
    return pl.pallas_call(
        _swiglu_kernel,
        grid=(nf, m // tm),
        in_specs=[pl.BlockSpec((tm, k), lambda j, i: (i, 0)),
                  pl.BlockSpec((k, tn), lambda j, i: (0, j)),
                  pl.BlockSpec((k, tn), lambda j, i: (0, j + nf))],
        out_specs=pl.BlockSpec((tm, tn), lambda j, i: (i, j)),
        out_shape=jax.ShapeDtypeStruct((m, f), BF16),
        compiler_params=_params(("arbitrary", "arbitrary")),
        name=name,
    )(x, w_gu, w_gu)


def _mm_acc_kernel(x_ref, w_ref, o_ref):
    @pl.when(pl.program_id(2) == 0)
    def _():
        o_ref[...] = jnp.zeros_like(o_ref)

    o_ref[...] += _dot(x_ref[...], _bf(w_ref[...]))


def matmul_ksplit(x, w, tm, tn, tk, name="ffn_down"):
    m, k = x.shape
    n = w.shape[1]
    return pl.pallas_call(
        _mm_acc_kernel,
        grid=(m // tm, n // tn, k // tk),
        in_specs=[pl.BlockSpec((tm, tk), lambda i, j, kk: (i, kk)),
                  pl.BlockSpec((tk, tn), lambda i, j, kk: (kk, j))],
        out_specs=pl.BlockSpec((tm, tn), lambda i, j, kk: (i, j)),
        out_shape=jax.ShapeDtypeStruct((m, n), F32),
        compiler_params=_params(("arbitrary", "arbitrary", "arbitrary")),
        name=name,
    )(x, w)


def _rope64(pe_pair, tab):
    p = pe_pair * tab
    return p + pltpu.roll(p, A_ROPE, axis=1)


def _ab_in_kernel(x_ref, wt_ref, o_ref, w_sc):
    @pl.when(pl.program_id(1) == 0)
    def _():
        w_sc[...] = wt_ref[...].astype(BF16)

    o_ref[...] = _dot_nt(x_ref[...], w_sc[...]).astype(o_ref.dtype)


def ab_in_proj(x, w_in_t, tm=512, tn=Z_TILE):
    m, k = x.shape
    assert Z_RQ % tn == 0 and Z_W % tn == 0 and A_IN % 8 == 0

    def feature_row(j):
        return pl.multiple_of(jnp.where(j * tn >= Z_RQ, j * tn - Z_RQ + A_IN, j * tn), 8)

    return pl.pallas_call(
        _ab_in_kernel,
        grid=(Z_W // tn, m // tm),
        in_specs=[pl.BlockSpec((tm, k), lambda j, i: (i, 0)),
                  pl.BlockSpec((pl.Element(tn), pl.Element(k)), lambda j, i: (feature_row(j), 0))],
        out_specs=pl.BlockSpec((tm, tn), lambda j, i: (i, j)),
        out_shape=jax.ShapeDtypeStruct((m, Z_W), BF16),
        scratch_shapes=[pltpu.VMEM((tn, k), BF16)],
        compiler_params=_params(("arbitrary", "arbitrary")),
        name="ab_in_proj",
    )(x, w_in_t)


def _a_norm_kernel(cq_ref, ckv_ref, x_ref, wkr_ref, tabc_ref, tabs_ref, qn_ref, kvn_ref,
                   cqn_ref, ckvn_ref, kpe_ref):
    cqn_ref[...] = _rms(cq_ref[...].astype(F32), qn_ref[...]).astype(BF16)
    ckvn_ref[...] = _rms(ckv_ref[...].astype(F32), kvn_ref[...]).astype(BF16)
    kr = _dot_nt(x_ref[...], _bf(wkr_ref[...]))
    lane = lax.broadcasted_iota(jnp.int32, kr.shape, 1)
    half = A_ROPE // 2
    swapped = jnp.where(lane < half, pltpu.roll(kr, LANES - half, axis=1), pltpu.roll(kr, half, axis=1))
    kpe_ref[...] = (kr * tabc_ref[...] + swapped * tabs_ref[...]).astype(BF16)


def a_norm(z, hn, w_in_t, tab_kc, tab_ks, q_norm, kv_norm, tm=512):
    t, d = hn.shape
    kr_block = (A_Q_RANK + A_KV_RANK) // LANES
    return pl.pallas_call(
        _a_norm_kernel,
        grid=(t // tm,),
        in_specs=[pl.BlockSpec((tm, A_Q_RANK), lambda i: (i, Z_CQ // A_Q_RANK)),
                  pl.BlockSpec((tm, A_KV_RANK), lambda i: (i, Z_CKV // A_KV_RANK)),
                  pl.BlockSpec((tm, d), lambda i: (i, 0)),
                  pl.BlockSpec((LANES, d), lambda i: (kr_block, 0)),
                  pl.BlockSpec((tm, LANES), lambda i: (i, 0)),
                  pl.BlockSpec((tm, LANES), lambda i: (i, 0)),
                  pl.BlockSpec((1, A_Q_RANK), lambda i: (0, 0)),
                  pl.BlockSpec((1, A_KV_RANK), lambda i: (0, 0))],
        out_specs=[pl.BlockSpec((tm, A_Q_RANK), lambda i: (i, 0)),
                   pl.BlockSpec((tm, A_KV_RANK), lambda i: (i, 0)),
                   pl.BlockSpec((tm, LANES), lambda i: (i, 0))],
        out_shape=[jax.ShapeDtypeStruct((t, A_Q_RANK), BF16),
                   jax.ShapeDtypeStruct((t, A_KV_RANK), BF16),
                   jax.ShapeDtypeStruct((t, LANES), BF16)],
        compiler_params=_params(("arbitrary",)),
        name="a_norm",
    )(z, z, hn, w_in_t, tab_kc, tab_ks, q_norm.reshape(1, -1), kv_norm.reshape(1, -1))


def _q_proj_kernel(x_ref, w_ref, tab_ref, o_ref, *, scale):
    acc = _dot(x_ref[...], w_ref[0])
    nope = acc[:, :A_NOPE] * scale
    pe = _rope64(acc[:, A_NOPE:], tab_ref[...]) * scale
    o_ref[0, 0] = jnp.concatenate([nope, pe], axis=1).astype(o_ref.dtype)


def q_proj(cqn, wq_r, tab64, batch, seq, tm=1024):
    nsb = seq // tm
    scale = float((A_NOPE + A_ROPE) ** -0.5)
    return pl.pallas_call(
        functools.partial(_q_proj_kernel, scale=scale),
        grid=(batch * nsb, A_HEADS),
        in_specs=[pl.BlockSpec((tm, A_Q_RANK), lambda i, h: (i, 0)),
                  pl.BlockSpec((1, A_Q_RANK, 2 * LANES), lambda i, h: (h, 0, 0)),
                  pl.BlockSpec((tm, LANES), lambda i, h: (i, 0))],
        out_specs=pl.BlockSpec((1, 1, tm, 2 * LANES), lambda i, h: (i // nsb, h, i % nsb, 0)),
        out_shape=jax.ShapeDtypeStruct((batch, A_HEADS, seq, 2 * LANES), BF16),
        compiler_params=_params(("arbitrary", "arbitrary")),
        name="q_proj",
    )(cqn, wq_r, tab64)


def _kv_proj_kernel(x_ref, w_ref, kpe_ref, k_ref, v_ref):
    acc = _dot(x_ref[...], _bf(w_ref[...]))
    k_ref[0, 0] = jnp.concatenate([acc[:, :A_NOPE].astype(BF16), kpe_ref[...]], axis=1)
    v_ref[0, 0] = acc[:, A_NOPE:].astype(BF16)


def kv_proj(ckvn, w_ukv, kpe, batch, seq, tm=1024):
    nsb = seq // tm
    hw = A_NOPE + A_V
    return pl.pallas_call(
        _kv_proj_kernel,
        grid=(batch * nsb, A_HEADS),
        in_specs=[pl.BlockSpec((tm, A_KV_RANK), lambda i, h: (i, 0)),
                  pl.BlockSpec((A_KV_RANK, hw), lambda i, h: (0, h)),
                  pl.BlockSpec((tm, LANES), lambda i, h: (i, 0))],
        out_specs=[pl.BlockSpec((1, 1, tm, 2 * LANES), lambda i, h: (i // nsb, h, i % nsb, 0)),
                   pl.BlockSpec((1, 1, tm, A_V), lambda i, h: (i // nsb, h, i % nsb, 0))],
        out_shape=[jax.ShapeDtypeStruct((batch, A_HEADS, seq, 2 * LANES), BF16),
                   jax.ShapeDtypeStruct((batch, A_HEADS, seq, A_V), BF16)],
        compiler_params=_params(("arbitrary", "arbitrary")),
        name="kv_proj",
    )(ckvn, w_ukv, kpe)


def _flash_kernel(q_ref, k_ref, v_ref, o_ref, *, tq, tk, hp):
    qi = pl.program_id(2)
    dv = v_ref.shape[-1]

    def step_one(h, off, width, carry, masked):
        m, l, acc = carry
        s = _dot_nt(q_ref[0, h], k_ref[0, h, pl.ds(off, width), :])
        if masked:
            row = lax.broadcasted_iota(jnp.int32, s.shape, 0)
            col = lax.broadcasted_iota(jnp.int32, s.shape, 1)
            s = jnp.where(col <= row, s, NEG_BIG)
        m_new = jnp.maximum(m, jnp.max(s, axis=-1, keepdims=True))
        alpha = jnp.exp(m - m_new)
        p = jnp.exp(s - m_new)
        l = alpha * l + jnp.sum(p, axis=-1, keepdims=True)
        acc = alpha * acc + _dot(p.astype(BF16), v_ref[0, h, pl.ds(off, width), :])
        return m_new, l, acc

    def step(off, width, carry, masked):
        return tuple(step_one(h, off, width, carry[h], masked) for h in range(hp))

    def body(j, carry):
        return step(pl.multiple_of(j * tk, tk), tk, carry, False)

    init = (jnp.full((tq, 1), NEG_BIG, F32), jnp.zeros((tq, 1), F32), jnp.zeros((tq, dv), F32))
    carry = (init,) * hp
    n_wide = (qi * tq) // tk
    carry = lax.fori_loop(0, n_wide, body, carry)
    if tk > tq:
        carry = lax.cond(qi * tq > n_wide * tk,
                         lambda c: step(pl.multiple_of(n_wide * tk, tq), tq, c, False),
                         lambda c: c, carry)
    carry = step(pl.multiple_of(qi * tq, tq), tq, carry, True)
    o_ref[...] = jnp.concatenate([acc / l for _, l, acc in carry], axis=1).astype(o_ref.dtype)


def causal_attention(q, k, v, tq=512, tk=1024, hp=4):
    b, h, s, dk = q.shape
    dv = v.shape[-1]
    nq = s // tq
    assert tk in (tq, 2 * tq) and s % tk == 0 and h % hp == 0
    return pl.pallas_call(
        functools.partial(_flash_kernel, tq=tq, tk=tk, hp=hp),
        grid=(b, h // hp, nq),
        in_specs=[pl.BlockSpec((1, hp, tq, dk), lambda bi, hi, qi: (bi, hi, qi, 0)),
                  pl.BlockSpec((1, hp, s, dk), lambda bi, hi, qi: (bi, hi, 0, 0)),
                  pl.BlockSpec((1, hp, s, dv), lambda bi, hi, qi: (bi, hi, 0, 0))],
        out_specs=pl.BlockSpec((tq, hp * dv), lambda bi, hi, qi: (bi * nq + qi, hi)),
        out_shape=jax.ShapeDtypeStruct((b * s, h * dv), BF16),
        compiler_params=_params(("arbitrary", "arbitrary", "arbitrary")),
        name="causal_attention",
    )(q, k, v)


def _rope_half(x, cos, sin):
    half = x.shape[1] // 2
    x1, x2 = x[:, :half], x[:, half:]
    return jnp.concatenate([x1 * cos - x2 * sin, x1 * sin + x2 * cos], axis=1)


def _retention_kernel(cdec_ref, q_ref, k_ref, v_ref, g_ref, cos_ref, sin_ref, intra_ref, qdec_ref,
                      kdec_ref, rn_ref, o_ref, state_sc, *, n_chunks):
    hi = pl.program_id(1)

    @pl.when(pl.program_id(2) == 0)
    def _():
        state_sc[...] = jnp.zeros_like(state_sc)

    chunk_decay = cdec_ref[hi]
    intra = intra_ref[0]
    q_decay = qdec_ref[0]
    k_decay = kdec_ref[0]
    rn = rn_ref[...]
    k_scale = float(B_QK ** -0.5)
    for c in range(n_chunks):
        sl = slice(c * RET_CHUNK, (c + 1) * RET_CHUNK)
        cos = cos_ref[sl, :]
        sin = sin_ref[sl, :]
        qr = _rope_half(q_ref[sl, :].astype(F32), cos, sin)
        kr = _rope_half(k_ref[sl, :].astype(F32), cos, sin) * k_scale
        v = v_ref[sl, :]
        qb = qr.astype(BF16)
        s = _dot_nt(qb, kr.astype(BF16)) * intra
        inner = _dot(s.astype(BF16), v)
        state = state_sc[...]
        cross = _dot(qb, state.astype(BF16)) * q_decay
        state_sc[...] = state * chunk_decay + _dot_tn((kr * k_decay).astype(BF16), v)
        y = inner + cross
        mu = jnp.mean(y, axis=-1, keepdims=True)
        yc = y - mu
        var = jnp.mean(yc * yc, axis=-1, keepdims=True)
        yn = yc * lax.rsqrt(var + EPS) * rn
        o_ref[sl, :] = (yn * jax.nn.silu(g_ref[sl, :].astype(F32))).astype(o_ref.dtype)


def retention(z, cos, sin, ret_norm, batch, seq, rows=1024):
    t = z.shape[0]
    nr = seq // rows
    hh = jnp.arange(B_HEADS, dtype=F32)
    log_g = jnp.log1p(-jnp.exp2(-5.0 - hh))
    idx = jnp.arange(RET_CHUNK, dtype=F32)
    diff = idx[:, None] - idx[None, :]
    intra = jnp.where(diff >= 0, jnp.exp(log_g[:, None, None] * jnp.maximum(diff, 0.0)), 0.0)
    ones = jnp.ones((1, 1, B_QK), F32)
    q_decay = jnp.exp(log_g[:, None] * (idx + 1.0))[:, :, None] * ones
    k_decay = jnp.exp(log_g[:, None] * (RET_CHUNK - 1.0 - idx))[:, :, None] * ones
    chunk_decay = jnp.exp(log_g * RET_CHUNK)

    def zspec(off):
        return pl.BlockSpec((rows, B_QK), lambda b, h, r, cd: (b * nr + r, off // B_QK + h))

    tab = pl.BlockSpec((rows, B_QK // 2), lambda b, h, r, cd: (b * nr + r, 0))
    per_head = pl.BlockSpec((1, RET_CHUNK, B_QK), lambda b, h, r, cd: (h, 0, 0))
    grid_spec = pltpu.PrefetchScalarGridSpec(
        num_scalar_prefetch=1,
        grid=(batch, B_HEADS, nr),
        in_specs=[zspec(Z_RQ), zspec(Z_RK), zspec(Z_RV), zspec(Z_RG), tab, tab,
                  pl.BlockSpec((1, RET_CHUNK, RET_CHUNK), lambda b, h, r, cd: (h, 0, 0)),
                  per_head, per_head,
                  pl.BlockSpec((1, B_V), lambda b, h, r, cd: (0, h))],
        out_specs=pl.BlockSpec((rows, B_V), lambda b, h, r, cd: (b * nr + r, h)),
        scratch_shapes=[pltpu.VMEM((B_QK, B_V), F32)],
    )
    return pl.pallas_call(
        functools.partial(_retention_kernel, n_chunks=rows // RET_CHUNK),
        grid_spec=grid_spec,
        out_shape=jax.ShapeDtypeStruct((t, B_HEADS * B_V), BF16),
        compiler_params=_params(("arbitrary", "arbitrary", "arbitrary")),
        name="retention",
    )(chunk_decay, z, z, z, z, cos, sin, intra, q_decay, k_decay, ret_norm.reshape(1, -1))


def _sgu_kernel(u_ref, v_ref, vn_ref, ws_ref, bs_ref, o_ref, *, n_chunks):
    gw = C_WIDTH // C_GROUPS
    v = v_ref[...].astype(F32)
    mu = jnp.mean(v, axis=-1, keepdims=True)
    vc = v - mu
    var = jnp.mean(vc * vc, axis=-1, keepdims=True)
    vn = (vc * lax.rsqrt(var + EPS) * vn_ref[...]).astype(BF16)
    r = lax.broadcasted_iota(jnp.int32, (C_CHUNK, C_CHUNK), 0)
    c = lax.broadcasted_iota(jnp.int32, (C_CHUNK, C_CHUNK), 1)
    bs = bs_ref[...]
    for g in range(C_GROUPS):
        ws = jnp.where(c <= r, ws_ref[g], 0.0).astype(BF16)
        bias = bs[:, g:g + 1]
        cols = slice(g * gw, (g + 1) * gw)
        for ch in range(n_chunks):
            rows = slice(ch * C_CHUNK, (ch + 1) * C_CHUNK)
            mixed = _dot(ws, vn[rows, cols]) + bias
            o_ref[rows, cols] = (u_ref[rows, cols].astype(F32) * mixed).astype(o_ref.dtype)


def spatial_gate(z, v_norm, w_s, b_s, rows=512):
    t = z.shape[0]
    return pl.pallas_call(
        functools.partial(_sgu_kernel, n_chunks=rows // C_CHUNK),
        grid=(t // rows,),
        in_specs=[pl.BlockSpec((rows, C_WIDTH), lambda i: (i, 0)),
                  pl.BlockSpec((rows, C_WIDTH), lambda i: (i, 1)),
                  pl.BlockSpec((1, C_WIDTH), lambda i: (0, 0)),
                  pl.BlockSpec((C_GROUPS, C_CHUNK, C_CHUNK), lambda i: (0, 0, 0)),
                  pl.BlockSpec((C_CHUNK, C_GROUPS), lambda i: (0, 0))],
        out_specs=pl.BlockSpec((rows, C_WIDTH), lambda i: (i, 0)),
        out_shape=jax.ShapeDtypeStruct((t, C_WIDTH), BF16),
        compiler_params=_params(("arbitrary",)),
        name="spatial_gate",
    )(z, z, v_norm.reshape(1, -1), w_s, b_s.T)


def _pack_bf16_pair(lo, hi):
    lo_bits = lax.bitcast_convert_type(lo.astype(BF16).astype(F32), jnp.uint32)
    hi_bits = lax.bitcast_convert_type(hi.astype(BF16).astype(F32), jnp.uint32)
    return (lo_bits >> 16) | (hi_bits & jnp.uint32(0xFFFF0000))


def _unpack_bf16_pair(words):
    lo = lax.bitcast_convert_type(words << 16, F32).astype(BF16)
    hi = lax.bitcast_convert_type(words & jnp.uint32(0xFFFF0000), F32).astype(BF16)
    return lo, hi


def _xattn_kernel(hmix_ref, gmix_ref, x_ref, gpre_ref, wq_ref, k_ref, v_ref, wo_ref, gpost_ref, gnext_ref,
                  xo_ref, hn_ref, *, pack_next):
    x = x_ref[...] + _rms(hmix_ref[...], gmix_ref[...])
    hn = _rms(x, gpre_ref[...]).astype(BF16)
    q = (_dot(hn, wq_ref[...]) * float(X_HEAD_DIM ** -0.5)).astype(BF16)
    k = k_ref[...]
    v = v_ref[...]
    outs = []
    for h in range(X_HEADS):
        cols = slice(h * X_HEAD_DIM, (h + 1) * X_HEAD_DIM)
        s = _dot_nt(q[:, cols], k[:, cols])
        p = jnp.exp(s - jnp.max(s, axis=-1, keepdims=True))
        o = _dot(p.astype(BF16), v[:, cols]) / jnp.sum(p, axis=-1, keepdims=True)
        outs.append(o.astype(BF16))
    hh = _dot(jnp.concatenate(outs, axis=1), wo_ref[...])
    xn = x + _rms(hh, gpost_ref[...])
    xo_ref[...] = xn
    hn = _rms(xn, gnext_ref[...])
    if pack_next:
        half = hn.shape[1] // 2
        hn_ref[...] = _pack_bf16_pair(hn[:, :half], hn[:, half:])
    else:
        hn_ref[...] = hn.astype(hn_ref.dtype)


def cross_attention(h_mix, g_mix, x, kv, wq, wo, g_pre, g_post, g_next, seq, pack_next=False, tm=256):
    t, d = x.shape
    hd = X_HEADS * X_HEAD_DIM
    nsb = seq // tm
    row = pl.BlockSpec((tm, d), lambda i: (i, 0))
    vec = pl.BlockSpec((1, d), lambda i: (0, 0))
    if pack_next:
        hn_spec = pl.BlockSpec((tm, d // 2), lambda i: (i, 0))
        hn_shape = jax.ShapeDtypeStruct((t, d // 2), jnp.uint32)
    else:
        hn_spec = row
        hn_shape = jax.ShapeDtypeStruct((t, d), BF16)
    return pl.pallas_call(
        functools.partial(_xattn_kernel, pack_next=pack_next),
        grid=(t // tm,),
        in_specs=[row, vec, row, vec,
                  pl.BlockSpec((d, hd), lambda i: (0, 0)),
                  pl.BlockSpec((N_MEM, hd), lambda i: (i // nsb, 0)),
                  pl.BlockSpec((N_MEM, hd), lambda i: (i // nsb, 1)),
                  pl.BlockSpec((hd, d), lambda i: (0, 0)),
                  vec, vec],
        out_specs=[row, hn_spec],
        out_shape=[jax.ShapeDtypeStruct((t, d), F32), hn_shape],
        compiler_params=_params(("arbitrary",)),
        name="cross_attention",
    )(h_mix, g_mix.reshape(1, d), x, g_pre.reshape(1, d), wq, kv, kv, wo, g_post.reshape(1, d),
      g_next.reshape(1, d))


def _router_kernel(x_ref, g_ref, wt_ref, idx_ref, gate_ref):
    hn = _rms(x_ref[...], g_ref[...])
    wt = wt_ref[...]
    h_hi = hn.astype(BF16)
    h_lo = (hn - h_hi.astype(F32)).astype(BF16)
    w_hi = wt.astype(BF16)
    w_lo = (wt - w_hi.astype(F32)).astype(BF16)
    logits = _dot_nt(w_hi, h_hi) + (_dot_nt(w_hi, h_lo) + _dot_nt(w_lo, h_hi))
    e = lax.broadcasted_iota(jnp.int32, logits.shape, 0)
    m1 = jnp.max(logits, axis=0, keepdims=True)
    i1 = jnp.min(jnp.where(logits == m1, e, N_EXPERTS), axis=0, keepdims=True)
    rest = jnp.where(e == i1, -jnp.inf, logits)
    m2 = jnp.max(rest, axis=0, keepdims=True)
    i2 = jnp.min(jnp.where(rest == m2, e, N_EXPERTS), axis=0, keepdims=True)
    ex = jnp.exp(m2 - m1)
    g1 = 1.0 / (1.0 + ex)
    idx_ref[...] = jnp.concatenate([i1, i2], axis=0)
    gate_ref[...] = jnp.concatenate([g1, ex * g1], axis=0)


def router(x, g, w_router, tm=512):
    t, d = x.shape
    return pl.pallas_call(
        _router_kernel,
        grid=(t // tm,),
        in_specs=[pl.BlockSpec((tm, d), lambda i: (i, 0)),
                  pl.BlockSpec((1, d), lambda i: (0, 0)),
                  pl.BlockSpec((N_EXPERTS, d), lambda i: (0, 0))],
        out_specs=[pl.BlockSpec((2, tm), lambda i: (0, i)),
                   pl.BlockSpec((2, tm), lambda i: (0, i))],
        out_shape=[jax.ShapeDtypeStruct((2, t), jnp.int32), jax.ShapeDtypeStruct((2, t), F32)],
        compiler_params=_params(("arbitrary",)),
        name="router",
    )(x, g.reshape(1, d), w_router.T)


def _live_col(tile_sub, t, j, n_cols):
    return jnp.where(tile_sub[t] > 0, j, n_cols - 1)


def _moe_gu_kernel(te_ref, tx_ref, tq_ref, x_ref, wg_ref, wu_ref, o_ref):
    n_sub = tq_ref[pl.program_id(0)]
    tm = o_ref.shape[0]
    half = wg_ref.shape[1] // 2

    for q in range(1, tm // MOE_SUB + 1):
        rows = q * MOE_SUB

        @pl.when(n_sub == q)
        def _(rows=rows):
            x_lo, x_hi = _unpack_bf16_pair(x_ref[:rows, :])
            g = _dot(x_lo, _bf(wg_ref[0, :half, :])) + _dot(x_hi, _bf(wg_ref[0, half:, :]))
            u = _dot(x_lo, _bf(wu_ref[0, :half, :])) + _dot(x_hi, _bf(wu_ref[0, half:, :]))
            o_ref[:rows, :] = (jax.nn.silu(g) * u).astype(o_ref.dtype)
            if rows < tm:
                o_ref[rows:, :] = jnp.zeros((tm - rows, o_ref.shape[1]), o_ref.dtype)

    @pl.when(n_sub == 0)
    def _():
        o_ref[...] = jnp.zeros_like(o_ref)


def moe_gate_up(xs, w_gu, tile_e, tile_x, tile_sub, tn=256):
    p = xs.shape[0]
    d = w_gu.shape[1]
    f = w_gu.shape[2] // 2
    nf = f // tn
    nt = p // MOE_TM
    grid_spec = pltpu.PrefetchScalarGridSpec(
        num_scalar_prefetch=3,
        grid=(nt, nf),
        in_specs=[pl.BlockSpec((MOE_TM, d // 2), lambda t, j, te, tx, ts: (tx[t], 0)),
                  pl.BlockSpec((1, d, tn), lambda t, j, te, tx, ts: (te[t], 0, _live_col(ts, t, j, nf))),
                  pl.BlockSpec((1, d, tn), lambda t, j, te, tx, ts: (te[t], 0, _live_col(ts, t, j, nf) + nf))],
        out_specs=pl.BlockSpec((MOE_TM, tn), lambda t, j, te, tx, ts: (t, j)),
    )
    return pl.pallas_call(
        _moe_gu_kernel,
        grid_spec=grid_spec,
        out_shape=jax.ShapeDtypeStruct((p, f), BF16),
        compiler_params=_params(("arbitrary", "arbitrary")),
        name="moe_gate_up",
    )(tile_e, tile_x, tile_sub, xs, w_gu, w_gu)


def _moe_down_kernel(te_ref, tx_ref, tq_ref, a_ref, w_ref, o_ref):
    n_sub = tq_ref[pl.program_id(0)]
    tm = o_ref.shape[0]

    for q in range(1, tm // MOE_SUB + 1):
        rows = q * MOE_SUB

        @pl.when(n_sub == q)
        def _(rows=rows):
            o_ref[:rows, :] = _dot(a_ref[:rows, :], _bf(w_ref[0]))
            if rows < tm:
                o_ref[rows:, :] = jnp.zeros((tm - rows, o_ref.shape[1]), o_ref.dtype)

    @pl.when(n_sub == 0)
    def _():
        o_ref[...] = jnp.zeros_like(o_ref)


def moe_down(act, w_down, tile_e, tile_x, tile_sub, tn=512):
    p, f = act.shape
    d = w_down.shape[2]
    nt = p // MOE_TM
    grid_spec = pltpu.PrefetchScalarGridSpec(
        num_scalar_prefetch=3,
        grid=(nt, d // tn),
        in_specs=[pl.BlockSpec((MOE_TM, f), lambda t, j, te, tx, ts: (tx[t], 0)),
                  pl.BlockSpec((1, f, tn), lambda t, j, te, tx, ts: (te[t], 0, _live_col(ts, t, j, d // tn)))],
        out_specs=pl.BlockSpec((MOE_TM, tn), lambda t, j, te, tx, ts: (t, j)),
    )
    return pl.pallas_call(
        _moe_down_kernel,
        grid_spec=grid_spec,
        out_shape=jax.ShapeDtypeStruct((p, d), F32),
        compiler_params=_params(("arbitrary", "arbitrary")),
        name="moe_down",
    )(tile_e, tile_x, tile_sub, act, w_down)


def moe_layer(x, g_pre, hn_packed, w_router, w_gu, w_down):
    t = x.shape[0]
    idx, gates = router(x, g_pre, w_router)
    experts = jnp.arange(N_EXPERTS, dtype=jnp.int32)
    e_flat = idx.reshape(-1)
    csum = jnp.cumsum((e_flat[:, None] == experts[None, :]).astype(jnp.int32), axis=0)
    rank = jnp.take_along_axis(csum, e_flat[:, None], axis=1)[:, 0] - 1
    count = csum[-1]
    n_tiles = (count + MOE_TM - 1) // MOE_TM
    tile_end = jnp.cumsum(n_tiles)
    tile_start = tile_end - n_tiles
    pos = tile_start[e_flat] * MOE_TM + rank
    nt = 2 * t // MOE_TM + N_EXPERTS
    p = nt * MOE_TM
    row_tok = (jnp.arange(p, dtype=jnp.int32) % t).at[pos].set(jnp.tile(jnp.arange(t, dtype=jnp.int32), 2))
    n_used = tile_end[-1]
    tile_id = jnp.arange(nt, dtype=jnp.int32)
    tile_x = jnp.minimum(tile_id, n_used - 1)
    tile_e = jnp.sum((tile_end[None, :] <= tile_x[:, None]).astype(jnp.int32), axis=1)
    tile_e = jnp.minimum(tile_e, N_EXPERTS - 1)
    rows_left = count[tile_e] - (tile_x - tile_start[tile_e]) * MOE_TM
    tile_sub = (jnp.clip(rows_left, 0, MOE_TM) + MOE_SUB - 1) // MOE_SUB
    tile_sub = jnp.where(tile_id < n_used, tile_sub, 0).astype(jnp.int32)
    xs = hn_packed.at[row_tok].get(mode="promise_in_bounds")
    act = moe_gate_up(xs, w_gu, tile_e, tile_x, tile_sub)
    y = moe_down(act, w_down, tile_e, tile_x, tile_sub)
    y0 = y.at[pos[:t]].get(mode="promise_in_bounds")
    y1 = y.at[pos[t:]].get(mode="promise_in_bounds")
    return y0, y1, gates.T


def _rope_tables(positions):
    pos = positions.reshape(-1).astype(F32)
    inv_a = ROPE_BASE ** (-jnp.arange(0, A_ROPE, 2, dtype=F32) / A_ROPE)
    ang_a = pos[:, None] * inv_a
    cos_a, sin_a = jnp.cos(ang_a), jnp.sin(ang_a)
    zero = jnp.zeros_like(cos_a)
    tab64 = jnp.concatenate([cos_a, cos_a, -sin_a, sin_a], axis=1)
    tab_kc = jnp.concatenate([cos_a, cos_a, zero, zero], axis=1)
    tab_ks = jnp.concatenate([-sin_a, sin_a, zero, zero], axis=1)
    inv_b = ROPE_BASE ** (-jnp.arange(0, B_QK, 2, dtype=F32) / B_QK)
    ang_b = pos[:, None] * inv_b
    return tab64, tab_kc, tab_ks, jnp.cos(ang_b), jnp.sin(ang_b)


def _swap_halves(w):
    half = w.shape[-1] // 2
    return jnp.concatenate([w[..., half:], w[..., :half]], axis=-1)


def kernel(x, mem, positions, norm_mix_pre, norm_mix_post, norm_x_pre, norm_x_post, norm_ffn_pre, norm_ffn_post, norm_mem, x_w_q, x_w_kv, x_w_o, ab_w_in, ab_q_norm, ab_w_uq, ab_kv_norm, ab_w_ukv, ab_ret_norm, ab_w_out, ffn_w_gu, ffn_w_down, c_w_in, c_v_norm, c_w_s, c_b_s, c_w_out, moe_router, moe_w_gu, moe_w_down):
    batch, seq, d = x.shape
    t = batch * seq
    x = x.reshape(t, d)
    mem2 = mem.reshape(batch * N_MEM, d)
    tab64, tab_kc, tab_ks, cos_b, sin_b = _rope_tables(positions)

    w_uq = ab_w_uq[0].reshape(A_Q_RANK, A_HEADS, A_NOPE + A_ROPE)
    w_pe = w_uq[:, :, A_NOPE:]
    wq_r = jnp.concatenate([w_uq[:, :, :A_NOPE], w_pe, _swap_halves(w_pe)], axis=-1)
    wq_r = wq_r.transpose(1, 0, 2).astype(BF16)

    hn = rms_norm_bf16(x, norm_mix_pre[0])
    w_in_t = ab_w_in[0].T
    z = ab_in_proj(hn, w_in_t)
    cqn, ckvn, kpe = a_norm(z, hn, w_in_t, tab_kc, tab_ks, ab_q_norm[0], ab_kv_norm[0])
    q = q_proj(cqn, wq_r, tab64, batch, seq)
    k, v = kv_proj(ckvn, ab_w_ukv[0], kpe, batch, seq)
    ya = causal_attention(q, k, v)
    yb = retention(z, cos_b, sin_b, ab_ret_norm[0], batch, seq)
    h = matmul_cat(ya, yb, ab_w_out[0], F32, tm=512, tn=1024, name="ab_out_proj")
    memn = rms_norm_bf16(mem2, norm_mem[0])
    kv = matmul(memn, x_w_kv[0], BF16, tm=512, tn=512, name="mem_kv_proj")
    x, hn = cross_attention(h, norm_mix_post[0], x, kv, x_w_q[0].astype(BF16), x_w_o[0].astype(BF16),
                            norm_x_pre[0], norm_x_post[0], norm_ffn_pre[0], seq)
    act = matmul_swiglu(hn, ffn_w_gu[0], tm=512, tn=512)
    h = matmul_ksplit(act, ffn_w_down[0], tm=1024, tn=2048, tk=1024)
    x, hn = resid_norm([h], x, norm_ffn_post[0], norm_mix_pre[1])

    zc = matmul(hn, c_w_in[0], BF16, tm=512, tn=1024, act="gelu", name="c_in_proj")
    yc = spatial_gate(zc, c_v_norm[0], c_w_s[0], c_b_s[0])
    h = matmul(yc, c_w_out[0], F32, tm=512, tn=1024, name="c_out_proj")
    memn = rms_norm_bf16(mem2, norm_mem[1])
    kv = matmul(memn, x_w_kv[1], BF16, tm=512, tn=512, name="mem_kv_proj")
    x, hn_packed = cross_attention(h, norm_mix_post[1], x, kv, x_w_q[1].astype(BF16), x_w_o[1].astype(BF16),
                                   norm_x_pre[1], norm_x_post[1], norm_ffn_pre[1], seq, pack_next=True)
    y0, y1, gates = moe_layer(x, norm_ffn_pre[1], hn_packed, moe_router[0], moe_w_gu[0], moe_w_down[0])
    x = resid_norm([y0, y1], x, norm_ffn_post[1], weights=gates)
    return x.reshape(batch, seq, d)
```

```python
import functools

import jax
import jax.numpy as jnp
from jax import lax
from jax.experimental import pallas as pl
from jax.experimental.pallas import tpu as pltpu

F32 = jnp.float32
BF16 = jnp.bfloat16

D_MODEL = 4096
N_MEM = 256
EPS = 1e-6
ROPE_BASE = 10000.0
A_HEADS = 16
A_Q_RANK = 1024
A_KV_RANK = 512
A_NOPE = 128
A_ROPE = 64
A_V = 128
B_HEADS = 8
B_QK = 256
B_V = 256
RET_CHUNK = 128
C_GROUPS = 8
C_WIDTH = 4096
C_CHUNK = 128
X_HEADS = 4
X_HEAD_DIM = 128
D_FF = 14336
N_EXPERTS = 8
D_EXPERT = 3584

LANES = 128
VMEM_LIMIT = 56 * 1024 * 1024
NEG_BIG = -1e30
LOG2_E = 1.4426950408889634

Z_CQ = 0
Z_CKV = A_Q_RANK
Z_RQ = A_Q_RANK + A_KV_RANK
Z_RK = Z_RQ + B_HEADS * B_QK
Z_RV = Z_RK + B_HEADS * B_QK
Z_RG = Z_RV + B_HEADS * B_V
Z_W = Z_RG + B_HEADS * B_V
A_IN = A_Q_RANK + A_KV_RANK + A_ROPE

MOE_TM = 1024
MOE_SUB = 256


def _params(sem, vmem=VMEM_LIMIT):
    return pltpu.CompilerParams(dimension_semantics=sem, vmem_limit_bytes=vmem)


def _rms(xf, g):
    ms = jnp.mean(xf * xf, axis=-1, keepdims=True)
    return xf * lax.rsqrt(ms + EPS) * g


def _bf(w):
    return w if w.dtype == BF16 else w.astype(BF16)


def _dot(a, b):
    return jnp.dot(a, b, preferred_element_type=F32)


def _dot_nt(a, b):
    return lax.dot_general(a, b, (((1,), (1,)), ((), ())), preferred_element_type=F32)


def _dot_tn(a, b):
    return lax.dot_general(a, b, (((0,), (0,)), ((), ())), preferred_element_type=F32)


def _norm_kernel(x_ref, g_ref, o_ref):
    o_ref[...] = _rms(x_ref[...].astype(F32), g_ref[...]).astype(o_ref.dtype)


def rms_norm_bf16(x, g, tm=512):
    m, d = x.shape
    return pl.pallas_call(
        _norm_kernel,
        grid=(m // tm,),
        in_specs=[pl.BlockSpec((tm, d), lambda i: (i, 0)),
                  pl.BlockSpec((1, d), lambda i: (0, 0))],
        out_specs=pl.BlockSpec((tm, d), lambda i: (i, 0)),
        out_shape=jax.ShapeDtypeStruct((m, d), BF16),
        compiler_params=_params(("arbitrary",)),
        name="rms_norm",
    )(x, g.reshape(1, d))


def _unpack_tiles_f32(words, tile_words):
    parts = []
    for c in range(0, words.shape[1], tile_words):
        w = words[:, c:c + tile_words]
        parts.append(lax.bitcast_convert_type(w << 16, F32))
        parts.append(lax.bitcast_convert_type(w & jnp.uint32(0xFFFF0000), F32))
    return jnp.concatenate(parts, axis=1)


def _resid_kernel(*refs, n_h, gated, with_next, packed_tile):
    h_refs, rest = refs[:n_h], refs[n_h:]

    def load(r):
        return r[...] if packed_tile is None else _unpack_tiles_f32(r[...], packed_tile)

    if gated:
        w = rest[0][...]
        rest = rest[1:]
        h = load(h_refs[0]) * w[:, 0:1]
        for c, r in enumerate(h_refs[1:], start=1):
            h = h + load(r) * w[:, c:c + 1]
    else:
        h = load(h_refs[0])
        for r in h_refs[1:]:
            h = h + load(r)
    x_ref, g_ref = rest[0], rest[1]
    xn = x_ref[...] + _rms(h, g_ref[...])
    if with_next:
        gn_ref, xo_ref, hn_ref = rest[2:]
        xo_ref[...] = xn
        hn_ref[...] = _rms(xn, gn_ref[...]).astype(hn_ref.dtype)
    else:
        xo_ref = rest[2]
        xo_ref[...] = xn


def resid_norm(hs, x, g_post, g_next=None, weights=None, packed_tile=None, tm=256):
    m, d = x.shape
    row = pl.BlockSpec((tm, d), lambda i: (i, 0))
    vec = pl.BlockSpec((1, d), lambda i: (0, 0))
    with_next = g_next is not None
    gated = weights is not None
    h_spec = row if packed_tile is None else pl.BlockSpec((tm, d // 2), lambda i: (i, 0))
    in_specs = [h_spec] * len(hs) + ([pl.BlockSpec((tm, len(hs)), lambda i: (i, 0))] if gated else [])
    in_specs += [row, vec] + ([vec] if with_next else [])
    args = list(hs) + ([weights] if gated else [])
    args += [x, g_post.reshape(1, d)] + ([g_next.reshape(1, d)] if with_next else [])
    out_shape = [jax.ShapeDtypeStruct((m, d), F32)]
    out_specs = [row]
    if with_next:
        out_shape.append(jax.ShapeDtypeStruct((m, d), BF16))
        out_specs.append(row)
    res = pl.pallas_call(
        functools.partial(_resid_kernel, n_h=len(hs), gated=gated, with_next=with_next,
                          packed_tile=packed_tile),
        grid=(m // tm,),
        in_specs=in_specs,
        out_specs=out_specs,
        out_shape=out_shape,
        compiler_params=_params(("arbitrary",)),
        name="resid_norm",
    )(*args)
    return res if with_next else res[0]


def _mm_kernel(x_ref, w_ref, o_ref, *, act):
    acc = _dot(x_ref[...], _bf(w_ref[...]))
    if act == "gelu":
        acc = jax.nn.gelu(acc)
    o_ref[...] = acc.astype(o_ref.dtype)


def matmul(x, w, out_dtype, tm, tn, act=None, name="matmul"):
    m, k = x.shape
    n = w.shape[1]
    return pl.pallas_call(
        functools.partial(_mm_kernel, act=act),
        grid=(n // tn, m // tm),
        in_specs=[pl.BlockSpec((tm, k), lambda j, i: (i, 0)),
                  pl.BlockSpec((k, tn), lambda j, i: (0, j))],
        out_specs=pl.BlockSpec((tm, tn), lambda j, i: (i, j)),
        out_shape=jax.ShapeDtypeStruct((m, n), out_dtype),
        compiler_params=_params(("arbitrary", "arbitrary")),
        name=name,
    )(x, w)


def _mm2_kernel(xa_ref, xb_ref, w_ref, o_ref):
    ka = xa_ref.shape[1]
    acc = _dot(xa_ref[...], _bf(w_ref[:ka, :]))
    acc = acc + _dot(xb_ref[...], _bf(w_ref[ka:, :]))
    o_ref[...] = acc.astype(o_ref.dtype)


def matmul_cat(xa, xb, w, out_dtype, tm, tn, name="matmul_cat"):
    m, ka = xa.shape
    kb = xb.shape[1]
    n = w.shape[1]
    return pl.pallas_call(
        _mm2_kernel,
        grid=(n // tn, m // tm),
        in_specs=[pl.BlockSpec((tm, ka), lambda j, i: (i, 0)),
                  pl.BlockSpec((tm, kb), lambda j, i: (i, 0)),
                  pl.BlockSpec((ka + kb, tn), lambda j, i: (0, j))],
        out_specs=pl.BlockSpec((tm, tn), lambda j, i: (i, j)),
        out_shape=jax.ShapeDtypeStruct((m, n), out_dtype),
        compiler_params=_params(("arbitrary", "arbitrary")),
        name=name,
    )(xa, xb, w)


def _swiglu_kernel(x_ref, wg_ref, wu_ref, o_ref):
    x = x_ref[...]
    g = _dot(x, _bf(wg_ref[...]))
    u = _dot(x, _bf(wu_ref[...]))
    o_ref[...] = (jax.nn.silu(g) * u).astype(o_ref.dtype)


def matmul_swiglu(x, w_gu, tm, tn, name="ffn_gate_up"):
    m, k = x.shape
    f = w_gu.shape[1] // 2
    nf = f // tn
    return pl.pallas_call(
        _swiglu_kernel,
        grid=(nf, m // tm),
        in_specs=[pl.BlockSpec((tm, k), lambda j, i: (i, 0)),
                  pl.BlockSpec((k, tn), lambda j, i: (0, j)),
                  pl.BlockSpec((k, tn), lambda j, i: (0, j + nf))],
        out_specs=pl.BlockSpec((tm, tn), lambda j, i: (i, j)),
        out_shape=jax.ShapeDtypeStruct((m, f), BF16),
        compiler_params=_params(("arbitrary", "arbitrary")),
        name=name,
    )(x, w_gu, w_gu)


def _mm_acc_kernel(x_ref, w_ref, o_ref):
    @pl.when(pl.program_id(2) == 0)
    def _():
        o_ref[...] = jnp.zeros_like(o_ref)

    o_ref[...] += _dot(x_ref[...], _bf(w_ref[...]))


def matmul_ksplit(x, w, tm, tn, tk, name="ffn_down"):
    m, k = x.shape
    n = w.shape[1]
    return pl.pallas_call(
        _mm_acc_kernel,
        grid=(m // tm, n // tn, k // tk),
        in_specs=[pl.BlockSpec((tm, tk), lambda i, j, kk: (i, kk)),
                  pl.BlockSpec((tk, tn), lambda i, j, kk: (kk, j))],
        out_specs=pl.BlockSpec((tm, tn), lambda i, j, kk: (i, j)),
        out_shape=jax.ShapeDtypeStruct((m, n), F32),
        compiler_params=_params(("arbitrary", "arbitrary", "arbitrary")),
        name=name,
    )(x, w)


def _rope64(pe_pair, tab):
    p = pe_pair * tab
    return p + pltpu.roll(p, A_ROPE, axis=1)


def _ab_in_kernel(x_ref, wt_ref, o_ref, w_sc):
    @pl.when(pl.program_id(1) == 0)
    def _():
        w_sc[...] = wt_ref[...].astype(BF16)

    o_ref[...] = _dot_nt(x_ref[...], w_sc[...]).astype(o_ref.dtype)


def ab_in_proj(x, w_in_t, tm=1024, tn=512):
    m, k = x.shape
    assert Z_RQ % tn == 0 and Z_W % tn == 0 and A_IN % 8 == 0

    def feature_row(j):
        return pl.multiple_of(jnp.where(j * tn >= Z_RQ, j * tn - Z_RQ + A_IN, j * tn), 8)

    return pl.pallas_call(
        _ab_in_kernel,
        grid=(Z_W // tn, m // tm),
        in_specs=[pl.BlockSpec((tm, k), lambda j, i: (i, 0)),
                  pl.BlockSpec((pl.Element(tn), pl.Element(k)), lambda j, i: (feature_row(j), 0))],
        out_specs=pl.BlockSpec((tm, tn), lambda j, i: (i, j)),
        out_shape=jax.ShapeDtypeStruct((m, Z_W), BF16),
        scratch_shapes=[pltpu.VMEM((tn, k), BF16)],
        compiler_params=_params(("arbitrary", "arbitrary")),
        name="ab_in_proj",
    )(x, w_in_t)


def _a_norm_kernel(cq_ref, ckv_ref, x_ref, wkr_ref, tabc_ref, tabs_ref, qn_ref, kvn_ref,
                   cqn_ref, ckvn_ref, kpe_ref):
    cqn_ref[...] = _rms(cq_ref[...].astype(F32), qn_ref[...]).astype(BF16)
    ckvn_ref[...] = _rms(ckv_ref[...].astype(F32), kvn_ref[...]).astype(BF16)
    kr = _dot_nt(x_ref[...], _bf(wkr_ref[...]))
    lane = lax.broadcasted_iota(jnp.int32, kr.shape, 1)
    half = A_ROPE // 2
    swapped = jnp.where(lane < half, pltpu.roll(kr, LANES - half, axis=1), pltpu.roll(kr, half, axis=1))
    kpe_ref[...] = (kr * tabc_ref[...] + swapped * tabs_ref[...]).astype(BF16)


def a_norm(z, hn, w_in_t, tab_kc, tab_ks, q_norm, kv_norm, tm=512):
    t, d = hn.shape
    kr_block = (A_Q_RANK + A_KV_RANK) // LANES
    return pl.pallas_call(
        _a_norm_kernel,
        grid=(t // tm,),
        in_specs=[pl.BlockSpec((tm, A_Q_RANK), lambda i: (i, Z_CQ // A_Q_RANK)),
                  pl.BlockSpec((tm, A_KV_RANK), lambda i: (i, Z_CKV // A_KV_RANK)),
                  pl.BlockSpec((tm, d), lambda i: (i, 0)),
                  pl.BlockSpec((LANES, d), lambda i: (kr_block, 0)),
                  pl.BlockSpec((tm, LANES), lambda i: (i, 0)),
                  pl.BlockSpec((tm, LANES), lambda i: (i, 0)),
                  pl.BlockSpec((1, A_Q_RANK), lambda i: (0, 0)),
                  pl.BlockSpec((1, A_KV_RANK), lambda i: (0, 0))],
        out_specs=[pl.BlockSpec((tm, A_Q_RANK), lambda i: (i, 0)),
                   pl.BlockSpec((tm, A_KV_RANK), lambda i: (i, 0)),
                   pl.BlockSpec((tm, LANES), lambda i: (i, 0))],
        out_shape=[jax.ShapeDtypeStruct((t, A_Q_RANK), BF16),
                   jax.ShapeDtypeStruct((t, A_KV_RANK), BF16),
                   jax.ShapeDtypeStruct((t, LANES), BF16)],
        compiler_params=_params(("arbitrary",)),
        name="a_norm",
    )(z, z, hn, w_in_t, tab_kc, tab_ks, q_norm.reshape(1, -1), kv_norm.reshape(1, -1))


def _q_proj_kernel(x_ref, w_ref, tab_ref, o_ref, *, scale):
    x = x_ref[...]
    tab = tab_ref[...]
    for h in range(w_ref.shape[0]):
        acc = _dot(x, w_ref[h])
        nope = acc[:, :A_NOPE] * scale
        pe = _rope64(acc[:, A_NOPE:], tab) * scale
        o_ref[0, h] = jnp.concatenate([nope, pe], axis=1).astype(o_ref.dtype)


def q_proj(cqn, wq_r, tab64, batch, seq, tm=1024, hp=4):
    nsb = seq // tm
    scale = float((A_NOPE + A_ROPE) ** -0.5 * LOG2_E)
    return pl.pallas_call(
        functools.partial(_q_proj_kernel, scale=scale),
        grid=(batch * nsb, A_HEADS // hp),
        in_specs=[pl.BlockSpec((tm, A_Q_RANK), lambda i, h: (i, 0)),
                  pl.BlockSpec((hp, A_Q_RANK, 2 * LANES), lambda i, h: (h, 0, 0)),
                  pl.BlockSpec((tm, LANES), lambda i, h: (i, 0))],
        out_specs=pl.BlockSpec((1, hp, tm, 2 * LANES), lambda i, h: (i // nsb, h, i % nsb, 0)),
        out_shape=jax.ShapeDtypeStruct((batch, A_HEADS, seq, 2 * LANES), BF16),
        compiler_params=_params(("arbitrary", "arbitrary")),
        name="q_proj",
    )(cqn, wq_r, tab64)


def _kv_proj_kernel(x_ref, w_ref, kpe_ref, k_ref, v_ref):
    acc = _dot(x_ref[...], _bf(w_ref[...]))
    kpe = kpe_ref[...]
    hw = A_NOPE + A_V
    for h in range(k_ref.shape[1]):
        k_ref[0, h] = jnp.concatenate([acc[:, h * hw:h * hw + A_NOPE].astype(BF16), kpe], axis=1)
        v_ref[0, h] = acc[:, h * hw + A_NOPE:(h + 1) * hw].astype(BF16)


def kv_proj(ckvn, w_ukv, kpe, batch, seq, tm=1024, hp=4):
    nsb = seq // tm
    hw = A_NOPE + A_V
    return pl.pallas_call(
        _kv_proj_kernel,
        grid=(batch * nsb, A_HEADS // hp),
        in_specs=[pl.BlockSpec((tm, A_KV_RANK), lambda i, h: (i, 0)),
                  pl.BlockSpec((A_KV_RANK, hp * hw), lambda i, h: (0, h)),
                  pl.BlockSpec((tm, LANES), lambda i, h: (i, 0))],
        out_specs=[pl.BlockSpec((1, hp, tm, 2 * LANES), lambda i, h: (i // nsb, h, i % nsb, 0)),
                   pl.BlockSpec((1, hp, tm, A_V), lambda i, h: (i // nsb, h, i % nsb, 0))],
        out_shape=[jax.ShapeDtypeStruct((batch, A_HEADS, seq, 2 * LANES), BF16),
                   jax.ShapeDtypeStruct((batch, A_HEADS, seq, A_V), BF16)],
        compiler_params=_params(("arbitrary", "arbitrary")),
        name="kv_proj",
    )(ckvn, w_ukv, kpe)


def _flash_kernel(q_ref, k_ref, v_ref, o_ref, *, tq, tk, hp):
    qi = pl.program_id(2)
    dv = v_ref.shape[-1]

    def step_one(h, off, width, carry, masked):
        m, l, acc = carry
        s = _dot_nt(q_ref[0, h], k_ref[0, h, pl.ds(off, width), :])
        if masked:
            row = lax.broadcasted_iota(jnp.int32, s.shape, 0)
            col = lax.broadcasted_iota(jnp.int32, s.shape, 1)
            s = jnp.where(col <= row, s, NEG_BIG)
        m_new = jnp.maximum(m, jnp.max(s, axis=-1, keepdims=True))
        alpha = jnp.exp2(m - m_new)
        p = jnp.exp2(s - m_new)
        l = alpha * l + jnp.sum(p, axis=-1, keepdims=True)
        acc = alpha * acc + _dot(p.astype(BF16), v_ref[0, h, pl.ds(off, width), :])
        return m_new, l, acc

    def step(off, width, carry, masked):
        return tuple(step_one(h, off, width, carry[h], masked) for h in range(hp))

    def body(j, carry):
        return step(pl.multiple_of(j * tk, tk), tk, carry, False)

    init = (jnp.full((tq, 1), NEG_BIG, F32), jnp.zeros((tq, 1), F32), jnp.zeros((tq, dv), F32))
    carry = (init,) * hp
    n_wide = (qi * tq) // tk
    carry = lax.fori_loop(0, n_wide, body, carry)
    if tk > tq:
        carry = lax.cond(qi * tq > n_wide * tk,
                         lambda c: step(pl.multiple_of(n_wide * tk, tq), tq, c, False),
                         lambda c: c, carry)
    carry = step(pl.multiple_of(qi * tq, tq), tq, carry, True)
    o_ref[...] = jnp.concatenate([acc / l for _, l, acc in carry], axis=1).astype(o_ref.dtype)


def causal_attention(q, k, v, tq=512, tk=1024, hp=4):
    b, h, s, dk = q.shape
    dv = v.shape[-1]
    nq = s // tq
    assert tk in (tq, 2 * tq) and s % tk == 0 and h % hp == 0
    return pl.pallas_call(
        functools.partial(_flash_kernel, tq=tq, tk=tk, hp=hp),
        grid=(b, h // hp, nq),
        in_specs=[pl.BlockSpec((1, hp, tq, dk), lambda bi, hi, qi: (bi, hi, qi, 0)),
                  pl.BlockSpec((1, hp, s, dk), lambda bi, hi, qi: (bi, hi, 0, 0)),
                  pl.BlockSpec((1, hp, s, dv), lambda bi, hi, qi: (bi, hi, 0, 0))],
        out_specs=pl.BlockSpec((tq, hp * dv), lambda bi, hi, qi: (bi * nq + qi, hi)),
        out_shape=jax.ShapeDtypeStruct((b * s, h * dv), BF16),
        compiler_params=_params(("arbitrary", "arbitrary", "arbitrary")),
        name="causal_attention",
    )(q, k, v)


def _rope_half(x, cos, sin):
    half = x.shape[1] // 2
    x1, x2 = x[:, :half], x[:, half:]
    return jnp.concatenate([x1 * cos - x2 * sin, x1 * sin + x2 * cos], axis=1)


def _retention_kernel(cdec_ref, q_ref, k_ref, v_ref, g_ref, cos_ref, sin_ref, intra_ref, qdec_ref,
                      kdec_ref, rn_ref, o_ref, state_sc, *, n_chunks):
    hi = pl.program_id(1)

    @pl.when(pl.program_id(2) == 0)
    def _():
        state_sc[...] = jnp.zeros_like(state_sc)

    chunk_decay = cdec_ref[hi]
    intra = intra_ref[0]
    q_decay = qdec_ref[0]
    k_decay = kdec_ref[0]
    rn = rn_ref[...]
    k_scale = float(B_QK ** -0.5)
    for c in range(n_chunks):
        sl = slice(c * RET_CHUNK, (c + 1) * RET_CHUNK)
        cos = cos_ref[sl, :]
        sin = sin_ref[sl, :]
        qr = _rope_half(q_ref[sl, :].astype(F32), cos, sin)
        kr = _rope_half(k_ref[sl, :].astype(F32), cos, sin) * k_scale
        v = v_ref[sl, :]
        qb = qr.astype(BF16)
        s = _dot_nt(qb, kr.astype(BF16)) * intra
        inner = _dot(s.astype(BF16), v)
        state = state_sc[...]
        cross = _dot(qb, state.astype(BF16)) * q_decay
        state_sc[...] = state * chunk_decay + _dot_tn((kr * k_decay).astype(BF16), v)
        y = inner + cross
        mu = jnp.mean(y, axis=-1, keepdims=True)
        yc = y - mu
        var = jnp.mean(yc * yc, axis=-1, keepdims=True)
        yn = yc * lax.rsqrt(var + EPS) * rn
        o_ref[sl, :] = (yn * jax.nn.silu(g_ref[sl, :].astype(F32))).astype(o_ref.dtype)


def retention(z, cos, sin, ret_norm, batch, seq, rows=1024):
    t = z.shape[0]
    nr = seq // rows
    hh = jnp.arange(B_HEADS, dtype=F32)
    log_g = jnp.log1p(-jnp.exp2(-5.0 - hh))
    idx = jnp.arange(RET_CHUNK, dtype=F32)
    diff = idx[:, None] - idx[None, :]
    intra = jnp.where(diff >= 0, jnp.exp(log_g[:, None, None] * jnp.maximum(diff, 0.0)), 0.0)
    ones = jnp.ones((1, 1, B_QK), F32)
    q_decay = jnp.exp(log_g[:, None] * (idx + 1.0))[:, :, None] * ones
    k_decay = jnp.exp(log_g[:, None] * (RET_CHUNK - 1.0 - idx))[:, :, None] * ones
    chunk_decay = jnp.exp(log_g * RET_CHUNK)

    def zspec(off):
        return pl.BlockSpec((rows, B_QK), lambda b, h, r, cd: (b * nr + r, off // B_QK + h))

    tab = pl.BlockSpec((rows, B_QK // 2), lambda b, h, r, cd: (b * nr + r, 0))
    per_head = pl.BlockSpec((1, RET_CHUNK, B_QK), lambda b, h, r, cd: (h, 0, 0))
    grid_spec = pltpu.PrefetchScalarGridSpec(
        num_scalar_prefetch=1,
        grid=(batch, B_HEADS, nr),
        in_specs=[zspec(Z_RQ), zspec(Z_RK), zspec(Z_RV), zspec(Z_RG), tab, tab,
                  pl.BlockSpec((1, RET_CHUNK, RET_CHUNK), lambda b, h, r, cd: (h, 0, 0)),
                  per_head, per_head,
                  pl.BlockSpec((1, B_V), lambda b, h, r, cd: (0, h))],
        out_specs=pl.BlockSpec((rows, B_V), lambda b, h, r, cd: (b * nr + r, h)),
        scratch_shapes=[pltpu.VMEM((B_QK, B_V), F32)],
    )
    return pl.pallas_call(
        functools.partial(_retention_kernel, n_chunks=rows // RET_CHUNK),
        grid_spec=grid_spec,
        out_shape=jax.ShapeDtypeStruct((t, B_HEADS * B_V), BF16),
        compiler_params=_params(("arbitrary", "arbitrary", "arbitrary")),
        name="retention",
    )(chunk_decay, z, z, z, z, cos, sin, intra, q_decay, k_decay, ret_norm.reshape(1, -1))


def _sgu_kernel(u_ref, v_ref, vn_ref, ws_ref, bs_ref, o_ref, *, n_chunks):
    gw = C_WIDTH // C_GROUPS
    v = v_ref[...].astype(F32)
    mu = jnp.mean(v, axis=-1, keepdims=True)
    vc = v - mu
    var = jnp.mean(vc * vc, axis=-1, keepdims=True)
    vn = (vc * lax.rsqrt(var + EPS) * vn_ref[...]).astype(BF16)
    r = lax.broadcasted_iota(jnp.int32, (C_CHUNK, C_CHUNK), 0)
    c = lax.broadcasted_iota(jnp.int32, (C_CHUNK, C_CHUNK), 1)
    bs = bs_ref[...]
    for g in range(C_GROUPS):
        ws = jnp.where(c <= r, ws_ref[g], 0.0).astype(BF16)
        bias = bs[:, g:g + 1]
        cols = slice(g * gw, (g + 1) * gw)
        for ch in range(n_chunks):
            rows = slice(ch * C_CHUNK, (ch + 1) * C_CHUNK)
            mixed = _dot(ws, vn[rows, cols]) + bias
            o_ref[rows, cols] = (u_ref[rows, cols].astype(F32) * mixed).astype(o_ref.dtype)


def spatial_gate(z, v_norm, w_s, b_s, rows=512):
    t = z.shape[0]
    return pl.pallas_call(
        functools.partial(_sgu_kernel, n_chunks=rows // C_CHUNK),
        grid=(t // rows,),
        in_specs=[pl.BlockSpec((rows, C_WIDTH), lambda i: (i, 0)),
                  pl.BlockSpec((rows, C_WIDTH), lambda i: (i, 1)),
                  pl.BlockSpec((1, C_WIDTH), lambda i: (0, 0)),
                  pl.BlockSpec((C_GROUPS, C_CHUNK, C_CHUNK), lambda i: (0, 0, 0)),
                  pl.BlockSpec((C_CHUNK, C_GROUPS), lambda i: (0, 0))],
        out_specs=pl.BlockSpec((rows, C_WIDTH), lambda i: (i, 0)),
        out_shape=jax.ShapeDtypeStruct((t, C_WIDTH), BF16),
        compiler_params=_params(("arbitrary",)),
        name="spatial_gate",
    )(z, z, v_norm.reshape(1, -1), w_s, b_s.T)


def _pack_bf16_pair(lo, hi):
    lo_bits = lax.bitcast_convert_type(lo.astype(BF16).astype(F32), jnp.uint32)
    hi_bits = lax.bitcast_convert_type(hi.astype(BF16).astype(F32), jnp.uint32)
    return (lo_bits >> 16) | (hi_bits & jnp.uint32(0xFFFF0000))


def _unpack_bf16_pair(words):
    lo = lax.bitcast_convert_type(words << 16, F32).astype(BF16)
    hi = lax.bitcast_convert_type(words & jnp.uint32(0xFFFF0000), F32).astype(BF16)
    return lo, hi


def _xattn_kernel(hmix_ref, gmix_ref, x_ref, gpre_ref, wq_ref, k_ref, v_ref, wo_ref, gpost_ref, gnext_ref,
                  xo_ref, hn_ref, *, pack_next):
    x = x_ref[...] + _rms(hmix_ref[...], gmix_ref[...])
    hn = _rms(x, gpre_ref[...]).astype(BF16)
    q = (_dot(hn, wq_ref[...]) * float(X_HEAD_DIM ** -0.5)).astype(BF16)
    k = k_ref[...]
    v = v_ref[...]
    outs = []
    for h in range(X_HEADS):
        cols = slice(h * X_HEAD_DIM, (h + 1) * X_HEAD_DIM)
        s = _dot_nt(q[:, cols], k[:, cols])
        p = jnp.exp(s - jnp.max(s, axis=-1, keepdims=True))
        o = _dot(p.astype(BF16), v[:, cols]) / jnp.sum(p, axis=-1, keepdims=True)
        outs.append(o.astype(BF16))
    hh = _dot(jnp.concatenate(outs, axis=1), wo_ref[...])
    xn = x + _rms(hh, gpost_ref[...])
    xo_ref[...] = xn
    hn = _rms(xn, gnext_ref[...])
    if pack_next:
        half = hn.shape[1] // 2
        hn_ref[...] = _pack_bf16_pair(hn[:, :half], hn[:, half:])
    else:
        hn_ref[...] = hn.astype(hn_ref.dtype)


def cross_attention(h_mix, g_mix, x, kv, wq, wo, g_pre, g_post, g_next, seq, pack_next=False, tm=256):
    t, d = x.shape
    hd = X_HEADS * X_HEAD_DIM
    nsb = seq // tm
    row = pl.BlockSpec((tm, d), lambda i: (i, 0))
    vec = pl.BlockSpec((1, d), lambda i: (0, 0))
    if pack_next:
        hn_spec = pl.BlockSpec((tm, d // 2), lambda i: (i, 0))
        hn_shape = jax.ShapeDtypeStruct((t, d // 2), jnp.uint32)
    else:
        hn_spec = row
        hn_shape = jax.ShapeDtypeStruct((t, d), BF16)
    return pl.pallas_call(
        functools.partial(_xattn_kernel, pack_next=pack_next),
        grid=(t // tm,),
        in_specs=[row, vec, row, vec,
                  pl.BlockSpec((d, hd), lambda i: (0, 0)),
                  pl.BlockSpec((N_MEM, hd), lambda i: (i // nsb, 0)),
                  pl.BlockSpec((N_MEM, hd), lambda i: (i // nsb, 1)),
                  pl.BlockSpec((hd, d), lambda i: (0, 0)),
                  vec, vec],
        out_specs=[row, hn_spec],
        out_shape=[jax.ShapeDtypeStruct((t, d), F32), hn_shape],
        compiler_params=_params(("arbitrary",)),
        name="cross_attention",
    )(h_mix, g_mix.reshape(1, d), x, g_pre.reshape(1, d), wq, kv, kv, wo, g_post.reshape(1, d),
      g_next.reshape(1, d))


def _router_kernel(x_ref, g_ref, wt_ref, idx_ref, gate_ref):
    hn = _rms(x_ref[...], g_ref[...])
    wt = wt_ref[...]
    h_hi = hn.astype(BF16)
    h_lo = (hn - h_hi.astype(F32)).astype(BF16)
    w_hi = wt.astype(BF16)
    w_lo = (wt - w_hi.astype(F32)).astype(BF16)
    logits = _dot_nt(w_hi, h_hi) + (_dot_nt(w_hi, h_lo) + _dot_nt(w_lo, h_hi))
    e = lax.broadcasted_iota(jnp.int32, logits.shape, 0)
    m1 = jnp.max(logits, axis=0, keepdims=True)
    i1 = jnp.min(jnp.where(logits == m1, e, N_EXPERTS), axis=0, keepdims=True)
    rest = jnp.where(e == i1, -jnp.inf, logits)
    m2 = jnp.max(rest, axis=0, keepdims=True)
    i2 = jnp.min(jnp.where(rest == m2, e, N_EXPERTS), axis=0, keepdims=True)
    ex = jnp.exp(m2 - m1)
    g1 = 1.0 / (1.0 + ex)
    idx_ref[...] = jnp.concatenate([i1, i2], axis=0)
    gate_ref[...] = jnp.concatenate([g1, ex * g1], axis=0)


def router(x, g, w_router, tm=512):
    t, d = x.shape
    return pl.pallas_call(
        _router_kernel,
        grid=(t // tm,),
        in_specs=[pl.BlockSpec((tm, d), lambda i: (i, 0)),
                  pl.BlockSpec((1, d), lambda i: (0, 0)),
                  pl.BlockSpec((N_EXPERTS, d), lambda i: (0, 0))],
        out_specs=[pl.BlockSpec((2, tm), lambda i: (0, i)),
                   pl.BlockSpec((2, tm), lambda i: (0, i))],
        out_shape=[jax.ShapeDtypeStruct((2, t), jnp.int32), jax.ShapeDtypeStruct((2, t), F32)],
        compiler_params=_params(("arbitrary",)),
        name="router",
    )(x, g.reshape(1, d), w_router.T)


def _live_col(tile_sub, t, j, n_cols):
    return jnp.where(tile_sub[t] > 0, j, n_cols - 1)


def _moe_gu_kernel(te_ref, tx_ref, tq_ref, x_ref, wg_ref, wu_ref, o_ref):
    n_sub = tq_ref[pl.program_id(0)]
    tm = o_ref.shape[0]
    half = wg_ref.shape[1] // 2

    for q in range(1, tm // MOE_SUB + 1):
        rows = q * MOE_SUB

        @pl.when(n_sub == q)
        def _(rows=rows):
            x_lo, x_hi = _unpack_bf16_pair(x_ref[:rows, :])
            g = _dot(x_lo, _bf(wg_ref[0, :half, :])) + _dot(x_hi, _bf(wg_ref[0, half:, :]))
            u = _dot(x_lo, _bf(wu_ref[0, :half, :])) + _dot(x_hi, _bf(wu_ref[0, half:, :]))
            o_ref[:rows, :] = (jax.nn.silu(g) * u).astype(o_ref.dtype)
            if rows < tm:
                o_ref[rows:, :] = jnp.zeros((tm - rows, o_ref.shape[1]), o_ref.dtype)

    @pl.when(n_sub == 0)
    def _():
        o_ref[...] = jnp.zeros_like(o_ref)


def moe_gate_up(xs, w_gu, tile_e, tile_x, tile_sub, tn=256):
    p = xs.shape[0]
    d = w_gu.shape[1]
    f = w_gu.shape[2] // 2
    nf = f // tn
    nt = p // MOE_TM
    grid_spec = pltpu.PrefetchScalarGridSpec(
        num_scalar_prefetch=3,
        grid=(nt, nf),
        in_specs=[pl.BlockSpec((MOE_TM, d // 2), lambda t, j, te, tx, ts: (tx[t], 0)),
                  pl.BlockSpec((1, d, tn), lambda t, j, te, tx, ts: (te[t], 0, _live_col(ts, t, j, nf))),
                  pl.BlockSpec((1, d, tn), lambda t, j, te, tx, ts: (te[t], 0, _live_col(ts, t, j, nf) + nf))],
        out_specs=pl.BlockSpec((MOE_TM, tn), lambda t, j, te, tx, ts: (t, j)),
    )
    return pl.pallas_call(
        _moe_gu_kernel,
        grid_spec=grid_spec,
        out_shape=jax.ShapeDtypeStruct((p, f), BF16),
        compiler_params=_params(("arbitrary", "arbitrary")),
        name="moe_gate_up",
    )(tile_e, tile_x, tile_sub, xs, w_gu, w_gu)


def _moe_down_kernel(te_ref, tx_ref, tq_ref, a_ref, w_ref, o_ref):
    n_sub = tq_ref[pl.program_id(0)]
    tm = o_ref.shape[0]

    for q in range(1, tm // MOE_SUB + 1):
        rows = q * MOE_SUB

        @pl.when(n_sub == q)
        def _(rows=rows):
            y = _dot(a_ref[:rows, :], _bf(w_ref[0]))
            half = y.shape[1] // 2
            o_ref[:rows, :] = _pack_bf16_pair(y[:, :half], y[:, half:])
            if rows < tm:
                o_ref[rows:, :] = jnp.zeros((tm - rows, o_ref.shape[1]), o_ref.dtype)

    @pl.when(n_sub == 0)
    def _():
        o_ref[...] = jnp.zeros_like(o_ref)


MOE_DOWN_TN = 512


def moe_down(act, w_down, tile_e, tile_x, tile_sub, tn=MOE_DOWN_TN):
    p, f = act.shape
    d = w_down.shape[2]
    nt = p // MOE_TM
    grid_spec = pltpu.PrefetchScalarGridSpec(
        num_scalar_prefetch=3,
        grid=(nt, d // tn),
        in_specs=[pl.BlockSpec((MOE_TM, f), lambda t, j, te, tx, ts: (tx[t], 0)),
                  pl.BlockSpec((1, f, tn), lambda t, j, te, tx, ts: (te[t], 0, _live_col(ts, t, j, d // tn)))],
        out_specs=pl.BlockSpec((MOE_TM, tn // 2), lambda t, j, te, tx, ts: (t, j)),
    )
    return pl.pallas_call(
        _moe_down_kernel,
        grid_spec=grid_spec,
        out_shape=jax.ShapeDtypeStruct((p, d // 2), jnp.uint32),
        compiler_params=_params(("arbitrary", "arbitrary")),
        name="moe_down",
    )(tile_e, tile_x, tile_sub, act, w_down)


def moe_layer(x, g_pre, hn_packed, w_router, w_gu, w_down):
    t = x.shape[0]
    idx, gates = router(x, g_pre, w_router)
    experts = jnp.arange(N_EXPERTS, dtype=jnp.int32)
    e_flat = idx.reshape(-1)
    csum = jnp.cumsum((e_flat[:, None] == experts[None, :]).astype(jnp.int32), axis=0)
    rank = jnp.take_along_axis(csum, e_flat[:, None], axis=1)[:, 0] - 1
    count = csum[-1]
    n_tiles = (count + MOE_TM - 1) // MOE_TM
    tile_end = jnp.cumsum(n_tiles)
    tile_start = tile_end - n_tiles
    pos = tile_start[e_flat] * MOE_TM + rank
    nt = 2 * t // MOE_TM + N_EXPERTS
    p = nt * MOE_TM
    row_tok = (jnp.arange(p, dtype=jnp.int32) % t).at[pos].set(jnp.tile(jnp.arange(t, dtype=jnp.int32), 2))
    n_used = tile_end[-1]
    tile_id = jnp.arange(nt, dtype=jnp.int32)
    tile_x = jnp.minimum(tile_id, n_used - 1)
    tile_e = jnp.sum((tile_end[None, :] <= tile_x[:, None]).astype(jnp.int32), axis=1)
    tile_e = jnp.minimum(tile_e, N_EXPERTS - 1)
    rows_left = count[tile_e] - (tile_x - tile_start[tile_e]) * MOE_TM
    tile_sub = (jnp.clip(rows_left, 0, MOE_TM) + MOE_SUB - 1) // MOE_SUB
    tile_sub = jnp.where(tile_id < n_used, tile_sub, 0).astype(jnp.int32)
    xs = hn_packed.at[row_tok].get(mode="promise_in_bounds")
    act = moe_gate_up(xs, w_gu, tile_e, tile_x, tile_sub)
    y = moe_down(act, w_down, tile_e, tile_x, tile_sub)
    y0 = y.at[pos[:t]].get(mode="promise_in_bounds")
    y1 = y.at[pos[t:]].get(mode="promise_in_bounds")
    return y0, y1, gates.T


def _rope_tables(positions):
    pos = positions.reshape(-1).astype(F32)
    inv_a = ROPE_BASE ** (-jnp.arange(0, A_ROPE, 2, dtype=F32) / A_ROPE)
    ang_a = pos[:, None] * inv_a
    cos_a, sin_a = jnp.cos(ang_a), jnp.sin(ang_a)
    zero = jnp.zeros_like(cos_a)
    tab64 = jnp.concatenate([cos_a, cos_a, -sin_a, sin_a], axis=1)
    tab_kc = jnp.concatenate([cos_a, cos_a, zero, zero], axis=1)
    tab_ks = jnp.concatenate([-sin_a, sin_a, zero, zero], axis=1)
    inv_b = ROPE_BASE ** (-jnp.arange(0, B_QK, 2, dtype=F32) / B_QK)
    ang_b = pos[:, None] * inv_b
    return tab64, tab_kc, tab_ks, jnp.cos(ang_b), jnp.sin(ang_b)


def _swap_halves(w):
    half = w.shape[-1] // 2
    return jnp.concatenate([w[..., half:], w[..., :half]], axis=-1)


def kernel(x, mem, positions, norm_mix_pre, norm_mix_post, norm_x_pre, norm_x_post, norm_ffn_pre, norm_ffn_post, norm_mem, x_w_q, x_w_kv, x_w_o, ab_w_in, ab_q_norm, ab_w_uq, ab_kv_norm, ab_w_ukv, ab_ret_norm, ab_w_out, ffn_w_gu, ffn_w_down, c_w_in, c_v_norm, c_w_s, c_b_s, c_w_out, moe_router, moe_w_gu, moe_w_down):
    batch, seq, d = x.shape
    t = batch * seq
    x = x.reshape(t, d)
    mem2 = mem.reshape(batch * N_MEM, d)
    tab64, tab_kc, tab_ks, cos_b, sin_b = _rope_tables(positions)

    w_uq = ab_w_uq[0].reshape(A_Q_RANK, A_HEADS, A_NOPE + A_ROPE)
    w_pe = w_uq[:, :, A_NOPE:]
    wq_r = jnp.concatenate([w_uq[:, :, :A_NOPE], w_pe, _swap_halves(w_pe)], axis=-1)
    wq_r = wq_r.transpose(1, 0, 2).astype(BF16)

    hn = rms_norm_bf16(x, norm_mix_pre[0])
    w_in_t = ab_w_in[0].T
    z = ab_in_proj(hn, w_in_t)
    cqn, ckvn, kpe = a_norm(z, hn, w_in_t, tab_kc, tab_ks, ab_q_norm[0], ab_kv_norm[0])
    q = q_proj(cqn, wq_r, tab64, batch, seq)
    k, v = kv_proj(ckvn, ab_w_ukv[0], kpe, batch, seq)
    ya = causal_attention(q, k, v)
    yb = retention(z, cos_b, sin_b, ab_ret_norm[0], batch, seq)
    h = matmul_cat(ya, yb, ab_w_out[0], F32, tm=512, tn=1024, name="ab_out_proj")
    memn = rms_norm_bf16(mem2, norm_mem[0])
    kv = matmul(memn, x_w_kv[0], BF16, tm=512, tn=512, name="mem_kv_proj")
    x, hn = cross_attention(h, norm_mix_post[0], x, kv, x_w_q[0].astype(BF16), x_w_o[0].astype(BF16),
                            norm_x_pre[0], norm_x_post[0], norm_ffn_pre[0], seq)
    act = matmul_swiglu(hn, ffn_w_gu[0], tm=1024, tn=256)
    h = matmul_ksplit(act, ffn_w_down[0], tm=1024, tn=2048, tk=1024)
    x, hn = resid_norm([h], x, norm_ffn_post[0], norm_mix_pre[1])

    zc = matmul(hn, c_w_in[0], BF16, tm=1024, tn=512, act="gelu", name="c_in_proj")
    yc = spatial_gate(zc, c_v_norm[0], c_w_s[0], c_b_s[0])
    h = matmul(yc, c_w_out[0], F32, tm=512, tn=1024, name="c_out_proj")
    memn = rms_norm_bf16(mem2, norm_mem[1])
    kv = matmul(memn, x_w_kv[1], BF16, tm=512, tn=512, name="mem_kv_proj")
    x, hn_packed = cross_attention(h, norm_mix_post[1], x, kv, x_w_q[1].astype(BF16), x_w_o[1].astype(BF16),
                                   norm_x_pre[1], norm_x_post[1], norm_ffn_pre[1], seq, pack_next=True)
    y0, y1, gates = moe_layer(x, norm_ffn_pre[1], hn_packed, moe_router[0], moe_w_gu[0], moe_w_down[0])
    x = resid_norm([y0, y1], x, norm_ffn_post[1], weights=gates, packed_tile=MOE_DOWN_TN // 2)
    return x.reshape(batch, seq, d)
```

```python
import functools

import jax
import jax.numpy as jnp
from jax import lax
from jax.experimental import pallas as pl
from jax.experimental.pallas import tpu as pltpu

F32 = jnp.float32
BF16 = jnp.bfloat16

D_MODEL = 4096
N_MEM = 256
EPS = 1e-6
ROPE_BASE = 10000.0
A_HEADS = 16
A_Q_RANK = 1024
A_KV_RANK = 512
A_NOPE = 128
A_ROPE = 64
A_V = 128
B_HEADS = 8
B_QK = 256
B_V = 256
RET_CHUNK = 128
C_GROUPS = 8
C_WIDTH = 4096
C_CHUNK = 128
X_HEADS = 4
X_HEAD_DIM = 128
D_FF = 14336
N_EXPERTS = 8
D_EXPERT = 3584

LANES = 128
VMEM_LIMIT = 56 * 1024 * 1024
NEG_BIG = -1e30
LOG2_E = 1.4426950408889634

Z_CQ = 0
Z_CKV = A_Q_RANK
Z_RQ = A_Q_RANK + A_KV_RANK
Z_RK = Z_RQ + B_HEADS * B_QK
Z_RV = Z_RK + B_HEADS * B_QK
Z_RG = Z_RV + B_HEADS * B_V
Z_W = Z_RG + B_HEADS * B_V
A_IN = A_Q_RANK + A_KV_RANK + A_ROPE

MOE_TM = 1280
MOE_SUB = 256


def _params(sem, vmem=VMEM_LIMIT):
    return pltpu.CompilerParams(dimension_semantics=sem, vmem_limit_bytes=vmem)


def _rms(xf, g):
    ms = jnp.mean(xf * xf, axis=-1, keepdims=True)
    return xf * lax.rsqrt(ms + EPS) * g


def _bf(w):
    return w if w.dtype == BF16 else w.astype(BF16)


def _dot(a, b):
    return jnp.dot(a, b, preferred_element_type=F32)


def _dot_nt(a, b):
    return lax.dot_general(a, b, (((1,), (1,)), ((), ())), preferred_element_type=F32)


def _dot_tn(a, b):
    return lax.dot_general(a, b, (((0,), (0,)), ((), ())), preferred_element_type=F32)


def _norm_kernel(x_ref, g_ref, o_ref):
    o_ref[...] = _rms(x_ref[...].astype(F32), g_ref[...]).astype(o_ref.dtype)


def rms_norm_bf16(x, g, tm=512):
    m, d = x.shape
    return pl.pallas_call(
        _norm_kernel,
        grid=(m // tm,),
        in_specs=[pl.BlockSpec((tm, d), lambda i: (i, 0)),
                  pl.BlockSpec((1, d), lambda i: (0, 0))],
        out_specs=pl.BlockSpec((tm, d), lambda i: (i, 0)),
        out_shape=jax.ShapeDtypeStruct((m, d), BF16),
        compiler_params=_params(("arbitrary",)),
        name="rms_norm",
    )(x, g.reshape(1, d))


def _unpack_tiles_f32(words, tile_words):
    parts = []
    for c in range(0, words.shape[1], tile_words):
        w = words[:, c:c + tile_words]
        parts.append(lax.bitcast_convert_type(w << 16, F32))
        parts.append(lax.bitcast_convert_type(w & jnp.uint32(0xFFFF0000), F32))
    return jnp.concatenate(parts, axis=1)


def _resid_kernel(*refs, n_h, gated, with_next, packed_tile):
    h_refs, rest = refs[:n_h], refs[n_h:]

    def load(r):
        return r[...] if packed_tile is None else _unpack_tiles_f32(r[...], packed_tile)

    if gated:
        w = rest[0][...]
        rest = rest[1:]
        h = load(h_refs[0]) * w[:, 0:1]
        for c, r in enumerate(h_refs[1:], start=1):
            h = h + load(r) * w[:, c:c + 1]
    else:
        h = load(h_refs[0])
        for r in h_refs[1:]:
            h = h + load(r)
    x_ref, g_ref = rest[0], rest[1]
    xn = x_ref[...] + _rms(h, g_ref[...])
    if with_next:
        gn_ref, xo_ref, hn_ref = rest[2:]
        xo_ref[...] = xn
        hn_ref[...] = _rms(xn, gn_ref[...]).astype(hn_ref.dtype)
    else:
        xo_ref = rest[2]
        xo_ref[...] = xn


def resid_norm(hs, x, g_post, g_next=None, weights=None, packed_tile=None, tm=256):
    m, d = x.shape
    row = pl.BlockSpec((tm, d), lambda i: (i, 0))
    vec = pl.BlockSpec((1, d), lambda i: (0, 0))
    with_next = g_next is not None
    gated = weights is not None
    h_spec = row if packed_tile is None else pl.BlockSpec((tm, d // 2), lambda i: (i, 0))
    in_specs = [h_spec] * len(hs) + ([pl.BlockSpec((tm, len(hs)), lambda i: (i, 0))] if gated else [])
    in_specs += [row, vec] + ([vec] if with_next else [])
    args = list(hs) + ([weights] if gated else [])
    args += [x, g_post.reshape(1, d)] + ([g_next.reshape(1, d)] if with_next else [])
    out_shape = [jax.ShapeDtypeStruct((m, d), F32)]
    out_specs = [row]
    if with_next:
        out_shape.append(jax.ShapeDtypeStruct((m, d), BF16))
        out_specs.append(row)
    res = pl.pallas_call(
        functools.partial(_resid_kernel, n_h=len(hs), gated=gated, with_next=with_next,
                          packed_tile=packed_tile),
        grid=(m // tm,),
        in_specs=in_specs,
        out_specs=out_specs,
        out_shape=out_shape,
        compiler_params=_params(("arbitrary",)),
        name="resid_norm",
    )(*args)
    return res if with_next else res[0]


def _mm_kernel(x_ref, w_ref, o_ref, *, act):
    acc = _dot(x_ref[...], _bf(w_ref[...]))
    if act == "gelu":
        acc = jax.nn.gelu(acc)
    o_ref[...] = acc.astype(o_ref.dtype)


def matmul(x, w, out_dtype, tm, tn, act=None, name="matmul"):
    m, k = x.shape
    n = w.shape[1]
    return pl.pallas_call(
        functools.partial(_mm_kernel, act=act),
        grid=(n // tn, m // tm),
        in_specs=[pl.BlockSpec((tm, k), lambda j, i: (i, 0)),
                  pl.BlockSpec((k, tn), lambda j, i: (0, j))],
        out_specs=pl.BlockSpec((tm, tn), lambda j, i: (i, j)),
        out_shape=jax.ShapeDtypeStruct((m, n), out_dtype),
        compiler_params=_params(("arbitrary", "arbitrary")),
        name=name,
    )(x, w)


def _mm2_kernel(xa_ref, xb_ref, w_ref, o_ref):
    ka = xa_ref.shape[1]
    acc = _dot(xa_ref[...], _bf(w_ref[:ka, :]))
    acc = acc + _dot(xb_ref[...], _bf(w_ref[ka:, :]))
    o_ref[...] = acc.astype(o_ref.dtype)


def matmul_cat(xa, xb, w, out_dtype, tm, tn, name="matmul_cat"):
    m, ka = xa.shape
    kb = xb.shape[1]
    n = w.shape[1]
    return pl.pallas_call(
        _mm2_kernel,
        grid=(n // tn, m // tm),
        in_specs=[pl.BlockSpec((tm, ka), lambda j, i: (i, 0)),
                  pl.BlockSpec((tm, kb), lambda j, i: (i, 0)),
                  pl.BlockSpec((ka + kb, tn), lambda j, i: (0, j))],
        out_specs=pl.BlockSpec((tm, tn), lambda j, i: (i, j)),
        out_shape=jax.ShapeDtypeStruct((m, n), out_dtype),
        compiler_params=_params(("arbitrary", "arbitrary")),
        name=name,
    )(xa, xb, w)


def _swiglu_kernel(x_ref, wg_ref, wu_ref, o_ref):
    x = x_ref[...]
    g = _dot(x, _bf(wg_ref[...]))
    u = _dot(x, _bf(wu_ref[...]))
    o_ref[...] = (jax.nn.silu(g) * u).astype(o_ref.dtype)


def matmul_swiglu(x, w_gu, tm, tn, name="ffn_gate_up"):
    m, k = x.shape
    f = w_gu.shape[1] // 2
    nf = f // tn
    return pl.pallas_call(
        _swiglu_kernel,
        grid=(nf, m // tm),
        in_specs=[pl.BlockSpec((tm, k), lambda j, i: (i, 0)),
                  pl.BlockSpec((k, tn), lambda j, i: (0, j)),
                  pl.BlockSpec((k, tn), lambda j, i: (0, j + nf))],
        out_specs=pl.BlockSpec((tm, tn), lambda j, i: (i, j)),
        out_shape=jax.ShapeDtypeStruct((m, f), BF16),
        compiler_params=_params(("arbitrary", "arbitrary")),
        name=name,
    )(x, w_gu, w_gu)


def _mm_acc_kernel(x_ref, w_ref, o_ref):
    @pl.when(pl.program_id(2) == 0)
    def _():
        o_ref[...] = jnp.zeros_like(o_ref)

    o_ref[...] += _dot(x_ref[...], _bf(w_ref[...]))


def matmul_ksplit(x, w, tm, tn, tk, name="ffn_down"):
    m, k = x.shape
    n = w.shape[1]
    return pl.pallas_call(
        _mm_acc_kernel,
        grid=(m // tm, n // tn, k // tk),
        in_specs=[pl.BlockSpec((tm, tk), lambda i, j, kk: (i, kk)),
                  pl.BlockSpec((tk, tn), lambda i, j, kk: (kk, j))],
        out_specs=pl.BlockSpec((tm, tn), lambda i, j, kk: (i, j)),
        out_shape=jax.ShapeDtypeStruct((m, n), F32),
        compiler_params=_params(("arbitrary", "arbitrary", "arbitrary")),
        name=name,
    )(x, w)


def _rope64(pe_pair, tab):
    p = pe_pair * tab
    return p + pltpu.roll(p, A_ROPE, axis=1)


def _ab_in_kernel(x_ref, wt_ref, o_ref, w_sc):
    @pl.when(pl.program_id(1) == 0)
    def _():
        w_sc[...] = wt_ref[...].astype(BF16)

    o_ref[...] = _dot_nt(x_ref[...], w_sc[...]).astype(o_ref.dtype)


def ab_in_proj(x, w_in_t, tm=1024, tn=512):
    m, k = x.shape
    assert Z_RQ % tn == 0 and Z_W % tn == 0 and A_IN % 8 == 0

    def feature_row(j):
        return pl.multiple_of(jnp.where(j * tn >= Z_RQ, j * tn - Z_RQ + A_IN, j * tn), 8)

    return pl.pallas_call(
        _ab_in_kernel,
        grid=(Z_W // tn, m // tm),
        in_specs=[pl.BlockSpec((tm, k), lambda j, i: (i, 0)),
                  pl.BlockSpec((pl.Element(tn), pl.Element(k)), lambda j, i: (feature_row(j), 0))],
        out_specs=pl.BlockSpec((tm, tn), lambda j, i: (i, j)),
        out_shape=jax.ShapeDtypeStruct((m, Z_W), BF16),
        scratch_shapes=[pltpu.VMEM((tn, k), BF16)],
        compiler_params=_params(("arbitrary", "arbitrary")),
        name="ab_in_proj",
    )(x, w_in_t)


def _a_norm_kernel(cq_ref, ckv_ref, x_ref, wkr_ref, tabc_ref, tabs_ref, qn_ref, kvn_ref,
                   cqn_ref, ckvn_ref, kpe_ref):
    cqn_ref[...] = _rms(cq_ref[...].astype(F32), qn_ref[...]).astype(BF16)
    ckvn_ref[...] = _rms(ckv_ref[...].astype(F32), kvn_ref[...]).astype(BF16)
    kr = _dot_nt(x_ref[...], _bf(wkr_ref[...]))
    lane = lax.broadcasted_iota(jnp.int32, kr.shape, 1)
    half = A_ROPE // 2
    swapped = jnp.where(lane < half, pltpu.roll(kr, LANES - half, axis=1), pltpu.roll(kr, half, axis=1))
    kpe_ref[...] = (kr * tabc_ref[...] + swapped * tabs_ref[...]).astype(BF16)


def a_norm(z, hn, w_in_t, tab_kc, tab_ks, q_norm, kv_norm, tm=512):
    t, d = hn.shape
    kr_block = (A_Q_RANK + A_KV_RANK) // LANES
    return pl.pallas_call(
        _a_norm_kernel,
        grid=(t // tm,),
        in_specs=[pl.BlockSpec((tm, A_Q_RANK), lambda i: (i, Z_CQ // A_Q_RANK)),
                  pl.BlockSpec((tm, A_KV_RANK), lambda i: (i, Z_CKV // A_KV_RANK)),
                  pl.BlockSpec((tm, d), lambda i: (i, 0)),
                  pl.BlockSpec((LANES, d), lambda i: (kr_block, 0)),
                  pl.BlockSpec((tm, LANES), lambda i: (i, 0)),
                  pl.BlockSpec((tm, LANES), lambda i: (i, 0)),
                  pl.BlockSpec((1, A_Q_RANK), lambda i: (0, 0)),
                  pl.BlockSpec((1, A_KV_RANK), lambda i: (0, 0))],
        out_specs=[pl.BlockSpec((tm, A_Q_RANK), lambda i: (i, 0)),
                   pl.BlockSpec((tm, A_KV_RANK), lambda i: (i, 0)),
                   pl.BlockSpec((tm, LANES), lambda i: (i, 0))],
        out_shape=[jax.ShapeDtypeStruct((t, A_Q_RANK), BF16),
                   jax.ShapeDtypeStruct((t, A_KV_RANK), BF16),
                   jax.ShapeDtypeStruct((t, LANES), BF16)],
        compiler_params=_params(("arbitrary",)),
        name="a_norm",
    )(z, z, hn, w_in_t, tab_kc, tab_ks, q_norm.reshape(1, -1), kv_norm.reshape(1, -1))


def _q_proj_kernel(x_ref, w_ref, tab_ref, o_ref, *, scale):
    x = x_ref[...]
    tab = tab_ref[...]
    for h in range(w_ref.shape[0]):
        acc = _dot(x, w_ref[h])
        nope = acc[:, :A_NOPE] * scale
        pe = _rope64(acc[:, A_NOPE:], tab) * scale
        o_ref[0, h] = jnp.concatenate([nope, pe], axis=1).astype(o_ref.dtype)


def q_proj(cqn, wq_r, tab64, batch, seq, tm=1024, hp=4):
    nsb = seq // tm
    scale = float((A_NOPE + A_ROPE) ** -0.5 * LOG2_E)
    return pl.pallas_call(
        functools.partial(_q_proj_kernel, scale=scale),
        grid=(batch * nsb, A_HEADS // hp),
        in_specs=[pl.BlockSpec((tm, A_Q_RANK), lambda i, h: (i, 0)),
                  pl.BlockSpec((hp, A_Q_RANK, 2 * LANES), lambda i, h: (h, 0, 0)),
                  pl.BlockSpec((tm, LANES), lambda i, h: (i, 0))],
        out_specs=pl.BlockSpec((1, hp, tm, 2 * LANES), lambda i, h: (i // nsb, h, i % nsb, 0)),
        out_shape=jax.ShapeDtypeStruct((batch, A_HEADS, seq, 2 * LANES), BF16),
        compiler_params=_params(("arbitrary", "arbitrary")),
        name="q_proj",
    )(cqn, wq_r, tab64)


def _kv_proj_kernel(x_ref, w_ref, kpe_ref, k_ref, v_ref):
    acc = _dot(x_ref[...], _bf(w_ref[...]))
    kpe = kpe_ref[...]
    hw = A_NOPE + A_V
    for h in range(k_ref.shape[1]):
        k_ref[0, h] = jnp.concatenate([acc[:, h * hw:h * hw + A_NOPE].astype(BF16), kpe], axis=1)
        v_ref[0, h] = acc[:, h * hw + A_NOPE:(h + 1) * hw].astype(BF16)


def kv_proj(ckvn, w_ukv, kpe, batch, seq, tm=1024, hp=4):
    nsb = seq // tm
    hw = A_NOPE + A_V
    return pl.pallas_call(
        _kv_proj_kernel,
        grid=(batch * nsb, A_HEADS // hp),
        in_specs=[pl.BlockSpec((tm, A_KV_RANK), lambda i, h: (i, 0)),
                  pl.BlockSpec((A_KV_RANK, hp * hw), lambda i, h: (0, h)),
                  pl.BlockSpec((tm, LANES), lambda i, h: (i, 0))],
        out_specs=[pl.BlockSpec((1, hp, tm, 2 * LANES), lambda i, h: (i // nsb, h, i % nsb, 0)),
                   pl.BlockSpec((1, hp, tm, A_V), lambda i, h: (i // nsb, h, i % nsb, 0))],
        out_shape=[jax.ShapeDtypeStruct((batch, A_HEADS, seq, 2 * LANES), BF16),
                   jax.ShapeDtypeStruct((batch, A_HEADS, seq, A_V), BF16)],
        compiler_params=_params(("arbitrary", "arbitrary")),
        name="kv_proj",
    )(ckvn, w_ukv, kpe)


def _flash_kernel(q_ref, k_ref, v_ref, o_ref, *, tq, tk, hp):
    qi = pl.program_id(2)
    dv = v_ref.shape[-1]

    def step_one(h, off, width, carry, masked):
        m, l, acc = carry
        s = _dot_nt(q_ref[0, h], k_ref[0, h, pl.ds(off, width), :])
        if masked:
            row = lax.broadcasted_iota(jnp.int32, s.shape, 0)
            col = lax.broadcasted_iota(jnp.int32, s.shape, 1)
            s = jnp.where(col <= row, s, NEG_BIG)
        m_new = jnp.maximum(m, jnp.max(s, axis=-1, keepdims=True))
        alpha = jnp.exp2(m - m_new)
        p = jnp.exp2(s - m_new)
        l = alpha * l + jnp.sum(p, axis=-1, keepdims=True)
        acc = alpha * acc + _dot(p.astype(BF16), v_ref[0, h, pl.ds(off, width), :])
        return m_new, l, acc

    def step(off, width, carry, masked):
        return tuple(step_one(h, off, width, carry[h], masked) for h in range(hp))

    def body(j, carry):
        return step(pl.multiple_of(j * tk, tk), tk, carry, False)

    init = (jnp.full((tq, 1), NEG_BIG, F32), jnp.zeros((tq, 1), F32), jnp.zeros((tq, dv), F32))
    carry = (init,) * hp
    n_wide = (qi * tq) // tk
    carry = lax.fori_loop(0, n_wide, body, carry)
    if tk > tq:
        carry = lax.cond(qi * tq > n_wide * tk,
                         lambda c: step(pl.multiple_of(n_wide * tk, tq), tq, c, False),
                         lambda c: c, carry)
    carry = step(pl.multiple_of(qi * tq, tq), tq, carry, True)
    o_ref[...] = jnp.concatenate([acc / l for _, l, acc in carry], axis=1).astype(o_ref.dtype)


def causal_attention(q, k, v, tq=512, tk=1024, hp=4):
    b, h, s, dk = q.shape
    dv = v.shape[-1]
    nq = s // tq
    assert tk in (tq, 2 * tq) and s % tk == 0 and h % hp == 0
    return pl.pallas_call(
        functools.partial(_flash_kernel, tq=tq, tk=tk, hp=hp),
        grid=(b, h // hp, nq),
        in_specs=[pl.BlockSpec((1, hp, tq, dk), lambda bi, hi, qi: (bi, hi, qi, 0)),
                  pl.BlockSpec((1, hp, s, dk), lambda bi, hi, qi: (bi, hi, 0, 0)),
                  pl.BlockSpec((1, hp, s, dv), lambda bi, hi, qi: (bi, hi, 0, 0))],
        out_specs=pl.BlockSpec((tq, hp * dv), lambda bi, hi, qi: (bi * nq + qi, hi)),
        out_shape=jax.ShapeDtypeStruct((b * s, h * dv), BF16),
        compiler_params=_params(("arbitrary", "arbitrary", "arbitrary")),
        name="causal_attention",
    )(q, k, v)


def _rope_half(x, cos, sin):
    half = x.shape[1] // 2
    x1, x2 = x[:, :half], x[:, half:]
    return jnp.concatenate([x1 * cos - x2 * sin, x1 * sin + x2 * cos], axis=1)


def _retention_kernel(cdec_ref, q_ref, k_ref, v_ref, g_ref, cos_ref, sin_ref, intra_ref, qdec_ref,
                      kdec_ref, rn_ref, o_ref, state_sc, *, n_chunks):
    hi = pl.program_id(1)

    @pl.when(pl.program_id(2) == 0)
    def _():
        state_sc[...] = jnp.zeros_like(state_sc)

    chunk_decay = cdec_ref[hi]
    intra = intra_ref[0]
    q_decay = qdec_ref[0]
    k_decay = kdec_ref[0]
    rn = rn_ref[...]
    k_scale = float(B_QK ** -0.5)
    for c in range(n_chunks):
        sl = slice(c * RET_CHUNK, (c + 1) * RET_CHUNK)
        cos = cos_ref[sl, :]
        sin = sin_ref[sl, :]
        qr = _rope_half(q_ref[sl, :].astype(F32), cos, sin)
        kr = _rope_half(k_ref[sl, :].astype(F32), cos, sin) * k_scale
        v = v_ref[sl, :]
        qb = qr.astype(BF16)
        s = _dot_nt(qb, kr.astype(BF16)) * intra
        inner = _dot(s.astype(BF16), v)
        state = state_sc[...]
        cross = _dot(qb, state.astype(BF16)) * q_decay
        state_sc[...] = state * chunk_decay + _dot_tn((kr * k_decay).astype(BF16), v)
        y = inner + cross
        mu = jnp.mean(y, axis=-1, keepdims=True)
        yc = y - mu
        var = jnp.mean(yc * yc, axis=-1, keepdims=True)
        yn = yc * lax.rsqrt(var + EPS) * rn
        o_ref[sl, :] = (yn * jax.nn.silu(g_ref[sl, :].astype(F32))).astype(o_ref.dtype)


def retention(z, cos, sin, ret_norm, batch, seq, rows=1024):
    t = z.shape[0]
    nr = seq // rows
    hh = jnp.arange(B_HEADS, dtype=F32)
    log_g = jnp.log1p(-jnp.exp2(-5.0 - hh))
    idx = jnp.arange(RET_CHUNK, dtype=F32)
    diff = idx[:, None] - idx[None, :]
    intra = jnp.where(diff >= 0, jnp.exp(log_g[:, None, None] * jnp.maximum(diff, 0.0)), 0.0)
    ones = jnp.ones((1, 1, B_QK), F32)
    q_decay = jnp.exp(log_g[:, None] * (idx + 1.0))[:, :, None] * ones
    k_decay = jnp.exp(log_g[:, None] * (RET_CHUNK - 1.0 - idx))[:, :, None] * ones
    chunk_decay = jnp.exp(log_g * RET_CHUNK)

    def zspec(off):
        return pl.BlockSpec((rows, B_QK), lambda b, h, r, cd: (b * nr + r, off // B_QK + h))

    tab = pl.BlockSpec((rows, B_QK // 2), lambda b, h, r, cd: (b * nr + r, 0))
    per_head = pl.BlockSpec((1, RET_CHUNK, B_QK), lambda b, h, r, cd: (h, 0, 0))
    grid_spec = pltpu.PrefetchScalarGridSpec(
        num_scalar_prefetch=1,
        grid=(batch, B_HEADS, nr),
        in_specs=[zspec(Z_RQ), zspec(Z_RK), zspec(Z_RV), zspec(Z_RG), tab, tab,
                  pl.BlockSpec((1, RET_CHUNK, RET_CHUNK), lambda b, h, r, cd: (h, 0, 0)),
                  per_head, per_head,
                  pl.BlockSpec((1, B_V), lambda b, h, r, cd: (0, h))],
        out_specs=pl.BlockSpec((rows, B_V), lambda b, h, r, cd: (b * nr + r, h)),
        scratch_shapes=[pltpu.VMEM((B_QK, B_V), F32)],
    )
    return pl.pallas_call(
        functools.partial(_retention_kernel, n_chunks=rows // RET_CHUNK),
        grid_spec=grid_spec,
        out_shape=jax.ShapeDtypeStruct((t, B_HEADS * B_V), BF16),
        compiler_params=_params(("arbitrary", "arbitrary", "arbitrary")),
        name="retention",
    )(chunk_decay, z, z, z, z, cos, sin, intra, q_decay, k_decay, ret_norm.reshape(1, -1))


def _sgu_kernel(u_ref, v_ref, vn_ref, ws_ref, bs_ref, o_ref, *, n_chunks):
    gw = C_WIDTH // C_GROUPS
    v = v_ref[...].astype(F32)
    mu = jnp.mean(v, axis=-1, keepdims=True)
    vc = v - mu
    var = jnp.mean(vc * vc, axis=-1, keepdims=True)
    vn = (vc * lax.rsqrt(var + EPS) * vn_ref[...]).astype(BF16)
    r = lax.broadcasted_iota(jnp.int32, (C_CHUNK, C_CHUNK), 0)
    c = lax.broadcasted_iota(jnp.int32, (C_CHUNK, C_CHUNK), 1)
    bs = bs_ref[...]
    for g in range(C_GROUPS):
        ws = jnp.where(c <= r, ws_ref[g], 0.0).astype(BF16)
        bias = bs[:, g:g + 1]
        cols = slice(g * gw, (g + 1) * gw)
        for ch in range(n_chunks):
            rows = slice(ch * C_CHUNK, (ch + 1) * C_CHUNK)
            mixed = _dot(ws, vn[rows, cols]) + bias
            o_ref[rows, cols] = (u_ref[rows, cols].astype(F32) * mixed).astype(o_ref.dtype)


def spatial_gate(z, v_norm, w_s, b_s, rows=512):
    t = z.shape[0]
    return pl.pallas_call(
        functools.partial(_sgu_kernel, n_chunks=rows // C_CHUNK),
        grid=(t // rows,),
        in_specs=[pl.BlockSpec((rows, C_WIDTH), lambda i: (i, 0)),
                  pl.BlockSpec((rows, C_WIDTH), lambda i: (i, 1)),
                  pl.BlockSpec((1, C_WIDTH), lambda i: (0, 0)),
                  pl.BlockSpec((C_GROUPS, C_CHUNK, C_CHUNK), lambda i: (0, 0, 0)),
                  pl.BlockSpec((C_CHUNK, C_GROUPS), lambda i: (0, 0))],
        out_specs=pl.BlockSpec((rows, C_WIDTH), lambda i: (i, 0)),
        out_shape=jax.ShapeDtypeStruct((t, C_WIDTH), BF16),
        compiler_params=_params(("arbitrary",)),
        name="spatial_gate",
    )(z, z, v_norm.reshape(1, -1), w_s, b_s.T)


def _pack_bf16_pair(lo, hi):
    lo_bits = lax.bitcast_convert_type(lo.astype(BF16).astype(F32), jnp.uint32)
    hi_bits = lax.bitcast_convert_type(hi.astype(BF16).astype(F32), jnp.uint32)
    return (lo_bits >> 16) | (hi_bits & jnp.uint32(0xFFFF0000))


def _unpack_bf16_pair(words):
    lo = lax.bitcast_convert_type(words << 16, F32).astype(BF16)
    hi = lax.bitcast_convert_type(words & jnp.uint32(0xFFFF0000), F32).astype(BF16)
    return lo, hi


def _xattn_kernel(hmix_ref, gmix_ref, x_ref, gpre_ref, wq_ref, k_ref, v_ref, wo_ref, gpost_ref, gnext_ref,
                  xo_ref, hn_ref, *, pack_next):
    x = x_ref[...] + _rms(hmix_ref[...], gmix_ref[...])
    hn = _rms(x, gpre_ref[...]).astype(BF16)
    q = (_dot(hn, wq_ref[...]) * float(X_HEAD_DIM ** -0.5)).astype(BF16)
    k = k_ref[...]
    v = v_ref[...]
    outs = []
    for h in range(X_HEADS):
        cols = slice(h * X_HEAD_DIM, (h + 1) * X_HEAD_DIM)
        s = _dot_nt(q[:, cols], k[:, cols])
        p = jnp.exp(s - jnp.max(s, axis=-1, keepdims=True))
        o = _dot(p.astype(BF16), v[:, cols]) / jnp.sum(p, axis=-1, keepdims=True)
        outs.append(o.astype(BF16))
    hh = _dot(jnp.concatenate(outs, axis=1), wo_ref[...])
    xn = x + _rms(hh, gpost_ref[...])
    xo_ref[...] = xn
    hn = _rms(xn, gnext_ref[...])
    if pack_next:
        half = hn.shape[1] // 2
        hn_ref[...] = _pack_bf16_pair(hn[:, :half], hn[:, half:])
    else:
        hn_ref[...] = hn.astype(hn_ref.dtype)


def cross_attention(h_mix, g_mix, x, kv, wq, wo, g_pre, g_post, g_next, seq, pack_next=False, tm=256):
    t, d = x.shape
    hd = X_HEADS * X_HEAD_DIM
    nsb = seq // tm
    row = pl.BlockSpec((tm, d), lambda i: (i, 0))
    vec = pl.BlockSpec((1, d), lambda i: (0, 0))
    if pack_next:
        hn_spec = pl.BlockSpec((tm, d // 2), lambda i: (i, 0))
        hn_shape = jax.ShapeDtypeStruct((t, d // 2), jnp.uint32)
    else:
        hn_spec = row
        hn_shape = jax.ShapeDtypeStruct((t, d), BF16)
    return pl.pallas_call(
        functools.partial(_xattn_kernel, pack_next=pack_next),
        grid=(t // tm,),
        in_specs=[row, vec, row, vec,
                  pl.BlockSpec((d, hd), lambda i: (0, 0)),
                  pl.BlockSpec((N_MEM, hd), lambda i: (i // nsb, 0)),
                  pl.BlockSpec((N_MEM, hd), lambda i: (i // nsb, 1)),
                  pl.BlockSpec((hd, d), lambda i: (0, 0)),
                  vec, vec],
        out_specs=[row, hn_spec],
        out_shape=[jax.ShapeDtypeStruct((t, d), F32), hn_shape],
        compiler_params=_params(("arbitrary",)),
        name="cross_attention",
    )(h_mix, g_mix.reshape(1, d), x, g_pre.reshape(1, d), wq, kv, kv, wo, g_post.reshape(1, d),
      g_next.reshape(1, d))


def _router_kernel(x_ref, g_ref, wt_ref, idx_ref, gate_ref):
    hn = _rms(x_ref[...], g_ref[...])
    wt = wt_ref[...]
    h_hi = hn.astype(BF16)
    h_lo = (hn - h_hi.astype(F32)).astype(BF16)
    w_hi = wt.astype(BF16)
    w_lo = (wt - w_hi.astype(F32)).astype(BF16)
    logits = _dot_nt(w_hi, h_hi) + (_dot_nt(w_hi, h_lo) + _dot_nt(w_lo, h_hi))
    e = lax.broadcasted_iota(jnp.int32, logits.shape, 0)
    m1 = jnp.max(logits, axis=0, keepdims=True)
    i1 = jnp.min(jnp.where(logits == m1, e, N_EXPERTS), axis=0, keepdims=True)
    rest = jnp.where(e == i1, -jnp.inf, logits)
    m2 = jnp.max(rest, axis=0, keepdims=True)
    i2 = jnp.min(jnp.where(rest == m2, e, N_EXPERTS), axis=0, keepdims=True)
    ex = jnp.exp(m2 - m1)
    g1 = 1.0 / (1.0 + ex)
    idx_ref[...] = jnp.concatenate([i1, i2], axis=0)
    gate_ref[...] = jnp.concatenate([g1, ex * g1], axis=0)


def router(x, g, w_router, tm=512):
    t, d = x.shape
    return pl.pallas_call(
        _router_kernel,
        grid=(t // tm,),
        in_specs=[pl.BlockSpec((tm, d), lambda i: (i, 0)),
                  pl.BlockSpec((1, d), lambda i: (0, 0)),
                  pl.BlockSpec((N_EXPERTS, d), lambda i: (0, 0))],
        out_specs=[pl.BlockSpec((2, tm), lambda i: (0, i)),
                   pl.BlockSpec((2, tm), lambda i: (0, i))],
        out_shape=[jax.ShapeDtypeStruct((2, t), jnp.int32), jax.ShapeDtypeStruct((2, t), F32)],
        compiler_params=_params(("arbitrary",)),
        name="router",
    )(x, g.reshape(1, d), w_router.T)


def _live_col(tile_sub, t, j, n_cols):
    return jnp.where(tile_sub[t] > 0, j, n_cols - 1)


def _moe_gu_kernel(te_ref, tx_ref, tq_ref, x_ref, wg_ref, wu_ref, o_ref):
    n_sub = tq_ref[pl.program_id(0)]
    tm = o_ref.shape[0]
    half = wg_ref.shape[1] // 2

    for q in range(1, tm // MOE_SUB + 1):
        rows = q * MOE_SUB

        @pl.when(n_sub == q)
        def _(rows=rows):
            x_lo, x_hi = _unpack_bf16_pair(x_ref[:rows, :])
            g = _dot(x_lo, _bf(wg_ref[0, :half, :])) + _dot(x_hi, _bf(wg_ref[0, half:, :]))
            u = _dot(x_lo, _bf(wu_ref[0, :half, :])) + _dot(x_hi, _bf(wu_ref[0, half:, :]))
            o_ref[:rows, :] = (jax.nn.silu(g) * u).astype(o_ref.dtype)
            if rows < tm:
                o_ref[rows:, :] = jnp.zeros((tm - rows, o_ref.shape[1]), o_ref.dtype)

    @pl.when(n_sub == 0)
    def _():
        o_ref[...] = jnp.zeros_like(o_ref)


def moe_gate_up(xs, w_gu, tile_e, tile_x, tile_sub, tn=256):
    p = xs.shape[0]
    d = w_gu.shape[1]
    f = w_gu.shape[2] // 2
    nf = f // tn
    nt = p // MOE_TM
    grid_spec = pltpu.PrefetchScalarGridSpec(
        num_scalar_prefetch=3,
        grid=(nt, nf),
        in_specs=[pl.BlockSpec((MOE_TM, d // 2), lambda t, j, te, tx, ts: (tx[t], 0)),
                  pl.BlockSpec((1, d, tn), lambda t, j, te, tx, ts: (te[t], 0, _live_col(ts, t, j, nf))),
                  pl.BlockSpec((1, d, tn), lambda t, j, te, tx, ts: (te[t], 0, _live_col(ts, t, j, nf) + nf))],
        out_specs=pl.BlockSpec((MOE_TM, tn), lambda t, j, te, tx, ts: (t, j)),
    )
    return pl.pallas_call(
        _moe_gu_kernel,
        grid_spec=grid_spec,
        out_shape=jax.ShapeDtypeStruct((p, f), BF16),
        compiler_params=_params(("arbitrary", "arbitrary")),
        name="moe_gate_up",
    )(tile_e, tile_x, tile_sub, xs, w_gu, w_gu)


def _moe_down_kernel(te_ref, tx_ref, tq_ref, a_ref, w_ref, o_ref):
    n_sub = tq_ref[pl.program_id(0)]
    tm = o_ref.shape[0]

    for q in range(1, tm // MOE_SUB + 1):
        rows = q * MOE_SUB

        @pl.when(n_sub == q)
        def _(rows=rows):
            y = _dot(a_ref[:rows, :], _bf(w_ref[0]))
            half = y.shape[1] // 2
            o_ref[:rows, :] = _pack_bf16_pair(y[:, :half], y[:, half:])
            if rows < tm:
                o_ref[rows:, :] = jnp.zeros((tm - rows, o_ref.shape[1]), o_ref.dtype)

    @pl.when(n_sub == 0)
    def _():
        o_ref[...] = jnp.zeros_like(o_ref)


MOE_DOWN_TN = 512


def moe_down(act, w_down, tile_e, tile_x, tile_sub, tn=MOE_DOWN_TN):
    p, f = act.shape
    d = w_down.shape[2]
    nt = p // MOE_TM
    grid_spec = pltpu.PrefetchScalarGridSpec(
        num_scalar_prefetch=3,
        grid=(nt, d // tn),
        in_specs=[pl.BlockSpec((MOE_TM, f), lambda t, j, te, tx, ts: (tx[t], 0)),
                  pl.BlockSpec((1, f, tn), lambda t, j, te, tx, ts: (te[t], 0, _live_col(ts, t, j, d // tn)))],
        out_specs=pl.BlockSpec((MOE_TM, tn // 2), lambda t, j, te, tx, ts: (t, j)),
    )
    return pl.pallas_call(
        _moe_down_kernel,
        grid_spec=grid_spec,
        out_shape=jax.ShapeDtypeStruct((p, d // 2), jnp.uint32),
        compiler_params=_params(("arbitrary", "arbitrary")),
        name="moe_down",
    )(tile_e, tile_x, tile_sub, act, w_down)


def moe_layer(x, g_pre, hn_packed, w_router, w_gu, w_down):
    t = x.shape[0]
    idx, gates = router(x, g_pre, w_router)
    experts = jnp.arange(N_EXPERTS, dtype=jnp.int32)
    e_flat = idx.reshape(-1)
    csum = jnp.cumsum((e_flat[:, None] == experts[None, :]).astype(jnp.int32), axis=0)
    rank = jnp.take_along_axis(csum, e_flat[:, None], axis=1)[:, 0] - 1
    count = csum[-1]
    n_tiles = (count + MOE_TM - 1) // MOE_TM
    tile_end = jnp.cumsum(n_tiles)
    tile_start = tile_end - n_tiles
    pos = tile_start[e_flat] * MOE_TM + rank
    nt = (2 * t + N_EXPERTS * (MOE_TM - 1)) // MOE_TM
    p = nt * MOE_TM
    row_tok = (jnp.arange(p, dtype=jnp.int32) % t).at[pos].set(jnp.tile(jnp.arange(t, dtype=jnp.int32), 2))
    n_used = tile_end[-1]
    tile_id = jnp.arange(nt, dtype=jnp.int32)
    tile_x = jnp.minimum(tile_id, n_used - 1)
    tile_e = jnp.sum((tile_end[None, :] <= tile_x[:, None]).astype(jnp.int32), axis=1)
    tile_e = jnp.minimum(tile_e, N_EXPERTS - 1)
    rows_left = count[tile_e] - (tile_x - tile_start[tile_e]) * MOE_TM
    tile_sub = (jnp.clip(rows_left, 0, MOE_TM) + MOE_SUB - 1) // MOE_SUB
    tile_sub = jnp.where(tile_id < n_used, tile_sub, 0).astype(jnp.int32)
    xs = hn_packed.at[row_tok].get(mode="promise_in_bounds")
    act = moe_gate_up(xs, w_gu, tile_e, tile_x, tile_sub)
    y = moe_down(act, w_down, tile_e, tile_x, tile_sub)
    y0 = y.at[pos[:t]].get(mode="promise_in_bounds")
    y1 = y.at[pos[t:]].get(mode="promise_in_bounds")
    return y0, y1, gates.T


def _rope_tables(positions):
    pos = positions.reshape(-1).astype(F32)
    inv_a = ROPE_BASE ** (-jnp.arange(0, A_ROPE, 2, dtype=F32) / A_ROPE)
    ang_a = pos[:, None] * inv_a
    cos_a, sin_a = jnp.cos(ang_a), jnp.sin(ang_a)
    zero = jnp.zeros_like(cos_a)
    tab64 = jnp.concatenate([cos_a, cos_a, -sin_a, sin_a], axis=1)
    tab_kc = jnp.concatenate([cos_a, cos_a, zero, zero], axis=1)
    tab_ks = jnp.concatenate([-sin_a, sin_a, zero, zero], axis=1)
    inv_b = ROPE_BASE ** (-jnp.arange(0, B_QK, 2, dtype=F32) / B_QK)
    ang_b = pos[:, None] * inv_b
    return tab64, tab_kc, tab_ks, jnp.cos(ang_b), jnp.sin(ang_b)


def _swap_halves(w):
    half = w.shape[-1] // 2
    return jnp.concatenate([w[..., half:], w[..., :half]], axis=-1)


def kernel(x, mem, positions, norm_mix_pre, norm_mix_post, norm_x_pre, norm_x_post, norm_ffn_pre, norm_ffn_post, norm_mem, x_w_q, x_w_kv, x_w_o, ab_w_in, ab_q_norm, ab_w_uq, ab_kv_norm, ab_w_ukv, ab_ret_norm, ab_w_out, ffn_w_gu, ffn_w_down, c_w_in, c_v_norm, c_w_s, c_b_s, c_w_out, moe_router, moe_w_gu, moe_w_down):
    batch, seq, d = x.shape
    t = batch * seq
    x = x.reshape(t, d)
    mem2 = mem.reshape(batch * N_MEM, d)
    tab64, tab_kc, tab_ks, cos_b, sin_b = _rope_tables(positions)

    w_uq = ab_w_uq[0].reshape(A_Q_RANK, A_HEADS, A_NOPE + A_ROPE)
    w_pe = w_uq[:, :, A_NOPE:]
    wq_r = jnp.concatenate([w_uq[:, :, :A_NOPE], w_pe, _swap_halves(w_pe)], axis=-1)
    wq_r = wq_r.transpose(1, 0, 2).astype(BF16)

    hn = rms_norm_bf16(x, norm_mix_pre[0])
    w_in_t = ab_w_in[0].T
    z = ab_in_proj(hn, w_in_t)
    cqn, ckvn, kpe = a_norm(z, hn, w_in_t, tab_kc, tab_ks, ab_q_norm[0], ab_kv_norm[0])
    q = q_proj(cqn, wq_r, tab64, batch, seq)
    k, v = kv_proj(ckvn, ab_w_ukv[0], kpe, batch, seq)
    ya = causal_attention(q, k, v)
    yb = retention(z, cos_b, sin_b, ab_ret_norm[0], batch, seq)
    h = matmul_cat(ya, yb, ab_w_out[0], F32, tm=512, tn=1024, name="ab_out_proj")
    memn = rms_norm_bf16(mem2, norm_mem[0])
    kv = matmul(memn, x_w_kv[0], BF16, tm=512, tn=512, name="mem_kv_proj")
    x, hn = cross_attention(h, norm_mix_post[0], x, kv, x_w_q[0].astype(BF16), x_w_o[0].astype(BF16),
                            norm_x_pre[0], norm_x_post[0], norm_ffn_pre[0], seq)
    act = matmul_swiglu(hn, ffn_w_gu[0], tm=1024, tn=256)
    h = matmul_ksplit(act, ffn_w_down[0], tm=1024, tn=2048, tk=1024)
    x, hn = resid_norm([h], x, norm_ffn_post[0], norm_mix_pre[1])

    zc = matmul(hn, c_w_in[0], BF16, tm=1024, tn=512, act="gelu", name="c_in_proj")
    yc = spatial_gate(zc, c_v_norm[0], c_w_s[0], c_b_s[0])
    h = matmul(yc, c_w_out[0], F32, tm=512, tn=1024, name="c_out_proj")
    memn = rms_norm_bf16(mem2, norm_mem[1])
    kv = matmul(memn, x_w_kv[1], BF16, tm=512, tn=512, name="mem_kv_proj")
    x, hn_packed = cross_attention(h, norm_mix_post[1], x, kv, x_w_q[1].astype(BF16), x_w_o[1].astype(BF16),
                                   norm_x_pre[1], norm_x_post[1], norm_ffn_pre[1], seq, pack_next=True)
    y0, y1, gates = moe_layer(x, norm_ffn_pre[1], hn_packed, moe_router[0], moe_w_gu[0], moe_w_down[0])
    x = resid_norm([y0, y1], x, norm_ffn_post[1], weights=gates, packed_tile=MOE_DOWN_TN // 2)
    return x.reshape(batch, seq, d)
```

```python
import functools

import jax
import jax.numpy as jnp
from jax import lax
from jax.experimental import pallas as pl
from jax.experimental.pallas import tpu as pltpu

F32 = jnp.float32
BF16 = jnp.bfloat16

D_MODEL = 4096
N_MEM = 256
EPS = 1e-6
ROPE_BASE = 10000.0
A_HEADS = 16
A_Q_RANK = 1024
A_KV_RANK = 512
A_NOPE = 128
A_ROPE = 64
A_V = 128
B_HEADS = 8
B_QK = 256
B_V = 256
RET_CHUNK = 128
C_GROUPS = 8
C_WIDTH = 4096
C_CHUNK = 128
X_HEADS = 4
X_HEAD_DIM = 128
D_FF = 14336
N_EXPERTS = 8
D_EXPERT = 3584

LANES = 128
VMEM_LIMIT = 56 * 1024 * 1024
NEG_BIG = -1e30
LOG2_E = 1.4426950408889634

Z_CQ = 0
Z_CKV = A_Q_RANK
Z_RQ = A_Q_RANK + A_KV_RANK
Z_RK = Z_RQ + B_HEADS * B_QK
Z_RV = Z_RK + B_HEADS * B_QK
Z_RG = Z_RV + B_HEADS * B_V
Z_W = Z_RG + B_HEADS * B_V
A_IN = A_Q_RANK + A_KV_RANK + A_ROPE

MOE_TM = 1280
MOE_SUB = 128


def _params(sem, vmem=VMEM_LIMIT):
    return pltpu.CompilerParams(dimension_semantics=sem, vmem_limit_bytes=vmem)


def _rms(xf, g):
    ms = jnp.mean(xf * xf, axis=-1, keepdims=True)
    return xf * lax.rsqrt(ms + EPS) * g


def _bf(w):
    return w if w.dtype == BF16 else w.astype(BF16)


def _dot(a, b):
    return jnp.dot(a, b, preferred_element_type=F32)


def _dot_nt(a, b):
    return lax.dot_general(a, b, (((1,), (1,)), ((), ())), preferred_element_type=F32)


def _dot_tn(a, b):
    return lax.dot_general(a, b, (((0,), (0,)), ((), ())), preferred_element_type=F32)


def _norm_kernel(x_ref, g_ref, o_ref):
    o_ref[...] = _rms(x_ref[...].astype(F32), g_ref[...]).astype(o_ref.dtype)


def rms_norm_bf16(x, g, tm=512):
    m, d = x.shape
    return pl.pallas_call(
        _norm_kernel,
        grid=(m // tm,),
        in_specs=[pl.BlockSpec((tm, d), lambda i: (i, 0)),
                  pl.BlockSpec((1, d), lambda i: (0, 0))],
        out_specs=pl.BlockSpec((tm, d), lambda i: (i, 0)),
        out_shape=jax.ShapeDtypeStruct((m, d), BF16),
        compiler_params=_params(("arbitrary",)),
        name="rms_norm",
    )(x, g.reshape(1, d))


def _unpack_tiles_f32(words, tile_words):
    parts = []
    for c in range(0, words.shape[1], tile_words):
        w = words[:, c:c + tile_words]
        parts.append(lax.bitcast_convert_type(w << 16, F32))
        parts.append(lax.bitcast_convert_type(w & jnp.uint32(0xFFFF0000), F32))
    return jnp.concatenate(parts, axis=1)


def _resid_kernel(*refs, n_h, gated, with_next, packed_tile):
    h_refs, rest = refs[:n_h], refs[n_h:]

    def load(r):
        return r[...] if packed_tile is None else _unpack_tiles_f32(r[...], packed_tile)

    if gated:
        w = rest[0][...]
        rest = rest[1:]
        h = load(h_refs[0]) * w[:, 0:1]
        for c, r in enumerate(h_refs[1:], start=1):
            h = h + load(r) * w[:, c:c + 1]
    else:
        h = load(h_refs[0])
        for r in h_refs[1:]:
            h = h + load(r)
    x_ref, g_ref = rest[0], rest[1]
    xn = x_ref[...] + _rms(h, g_ref[...])
    if with_next:
        gn_ref, xo_ref, hn_ref = rest[2:]
        xo_ref[...] = xn
        hn_ref[...] = _rms(xn, gn_ref[...]).astype(hn_ref.dtype)
    else:
        xo_ref = rest[2]
        xo_ref[...] = xn


def resid_norm(hs, x, g_post, g_next=None, weights=None, packed_tile=None, tm=256):
    m, d = x.shape
    row = pl.BlockSpec((tm, d), lambda i: (i, 0))
    vec = pl.BlockSpec((1, d), lambda i: (0, 0))
    with_next = g_next is not None
    gated = weights is not None
    h_spec = row if packed_tile is None else pl.BlockSpec((tm, d // 2), lambda i: (i, 0))
    in_specs = [h_spec] * len(hs) + ([pl.BlockSpec((tm, len(hs)), lambda i: (i, 0))] if gated else [])
    in_specs += [row, vec] + ([vec] if with_next else [])
    args = list(hs) + ([weights] if gated else [])
    args += [x, g_post.reshape(1, d)] + ([g_next.reshape(1, d)] if with_next else [])
    out_shape = [jax.ShapeDtypeStruct((m, d), F32)]
    out_specs = [row]
    if with_next:
        out_shape.append(jax.ShapeDtypeStruct((m, d), BF16))
        out_specs.append(row)
    res = pl.pallas_call(
        functools.partial(_resid_kernel, n_h=len(hs), gated=gated, with_next=with_next,
                          packed_tile=packed_tile),
        grid=(m // tm,),
        in_specs=in_specs,
        out_specs=out_specs,
        out_shape=out_shape,
        compiler_params=_params(("arbitrary",)),
        name="resid_norm",
    )(*args)
    return res if with_next else res[0]


def _mm_kernel(x_ref, w_ref, o_ref, *, act):
    acc = _dot(x_ref[...], _bf(w_ref[...]))
    if act == "gelu":
        acc = jax.nn.gelu(acc)
    o_ref[...] = acc.astype(o_ref.dtype)


def matmul(x, w, out_dtype, tm, tn, act=None, name="matmul"):
    m, k = x.shape
    n = w.shape[1]
    return pl.pallas_call(
        functools.partial(_mm_kernel, act=act),
        grid=(n // tn, m // tm),
        in_specs=[pl.BlockSpec((tm, k), lambda j, i: (i, 0)),
                  pl.BlockSpec((k, tn), lambda j, i: (0, j))],
        out_specs=pl.BlockSpec((tm, tn), lambda j, i: (i, j)),
        out_shape=jax.ShapeDtypeStruct((m, n), out_dtype),
        compiler_params=_params(("arbitrary", "arbitrary")),
        name=name,
    )(x, w)


def _mm2_kernel(xa_ref, xb_ref, w_ref, o_ref):
    ka = xa_ref.shape[1]
    acc = _dot(xa_ref[...], _bf(w_ref[:ka, :]))
    acc = acc + _dot(xb_ref[...], _bf(w_ref[ka:, :]))
    o_ref[...] = acc.astype(o_ref.dtype)


def matmul_cat(xa, xb, w, out_dtype, tm, tn, name="matmul_cat"):
    m, ka = xa.shape
    kb = xb.shape[1]
    n = w.shape[1]
    return pl.pallas_call(
        _mm2_kernel,
        grid=(n // tn, m // tm),
        in_specs=[pl.BlockSpec((tm, ka), lambda j, i: (i, 0)),
                  pl.BlockSpec((tm, kb), lambda j, i: (i, 0)),
                  pl.BlockSpec((ka + kb, tn), lambda j, i: (0, j))],
        out_specs=pl.BlockSpec((tm, tn), lambda j, i: (i, j)),
        out_shape=jax.ShapeDtypeStruct((m, n), out_dtype),
        compiler_params=_params(("arbitrary", "arbitrary")),
        name=name,
    )(xa, xb, w)


def _swiglu_kernel(x_ref, wg_ref, wu_ref, o_ref):
    x = x_ref[...]
    g = _dot(x, _bf(wg_ref[...]))
    u = _dot(x, _bf(wu_ref[...]))
    o_ref[...] = (jax.nn.silu(g) * u).astype(o_ref.dtype)


def matmul_swiglu(x, w_gu, tm, tn, name="ffn_gate_up"):
    m, k = x.shape
    f = w_gu.shape[1] // 2
    nf = f // tn
    return pl.pallas_call(
        _swiglu_kernel,
        grid=(nf, m // tm),
        in_specs=[pl.BlockSpec((tm, k), lambda j, i: (i, 0)),
                  pl.BlockSpec((k, tn), lambda j, i: (0, j)),
                  pl.BlockSpec((k, tn), lambda j, i: (0, j + nf))],
        out_specs=pl.BlockSpec((tm, tn), lambda j, i: (i, j)),
        out_shape=jax.ShapeDtypeStruct((m, f), BF16),
        compiler_params=_params(("arbitrary", "arbitrary")),
        name=name,
    )(x, w_gu, w_gu)


def _mm_acc_kernel(x_ref, w_ref, o_ref):
    @pl.when(pl.program_id(2) == 0)
    def _():
        o_ref[...] = jnp.zeros_like(o_ref)

    o_ref[...] += _dot(x_ref[...], _bf(w_ref[...]))


def matmul_ksplit(x, w, tm, tn, tk, name="ffn_down"):
    m, k = x.shape
    n = w.shape[1]
    return pl.pallas_call(
        _mm_acc_kernel,
        grid=(m // tm, n // tn, k // tk),
        in_specs=[pl.BlockSpec((tm, tk), lambda i, j, kk: (i, kk)),
                  pl.BlockSpec((tk, tn), lambda i, j, kk: (kk, j))],
        out_specs=pl.BlockSpec((tm, tn), lambda i, j, kk: (i, j)),
        out_shape=jax.ShapeDtypeStruct((m, n), F32),
        compiler_params=_params(("arbitrary", "arbitrary", "arbitrary")),
        name=name,
    )(x, w)


def _rope64(pe_pair, tab):
    p = pe_pair * tab
    return p + pltpu.roll(p, A_ROPE, axis=1)


def _ab_in_kernel(x_ref, wt_ref, o_ref, w_sc):
    @pl.when(pl.program_id(1) == 0)
    def _():
        w_sc[...] = wt_ref[...].astype(BF16)

    o_ref[...] = _dot_nt(x_ref[...], w_sc[...]).astype(o_ref.dtype)


def ab_in_proj(x, w_in_t, tm=1024, tn=512):
    m, k = x.shape
    assert Z_RQ % tn == 0 and Z_W % tn == 0 and A_IN % 8 == 0

    def feature_row(j):
        return pl.multiple_of(jnp.where(j * tn >= Z_RQ, j * tn - Z_RQ + A_IN, j * tn), 8)

    return pl.pallas_call(
        _ab_in_kernel,
        grid=(Z_W // tn, m // tm),
        in_specs=[pl.BlockSpec((tm, k), lambda j, i: (i, 0)),
                  pl.BlockSpec((pl.Element(tn), pl.Element(k)), lambda j, i: (feature_row(j), 0))],
        out_specs=pl.BlockSpec((tm, tn), lambda j, i: (i, j)),
        out_shape=jax.ShapeDtypeStruct((m, Z_W), BF16),
        scratch_shapes=[pltpu.VMEM((tn, k), BF16)],
        compiler_params=_params(("arbitrary", "arbitrary")),
        name="ab_in_proj",
    )(x, w_in_t)


def _a_norm_kernel(cq_ref, ckv_ref, x_ref, wkr_ref, tabc_ref, tabs_ref, qn_ref, kvn_ref,
                   cqn_ref, ckvn_ref, kpe_ref):
    cqn_ref[...] = _rms(cq_ref[...].astype(F32), qn_ref[...]).astype(BF16)
    ckvn_ref[...] = _rms(ckv_ref[...].astype(F32), kvn_ref[...]).astype(BF16)
    kr = _dot_nt(x_ref[...], _bf(wkr_ref[...]))
    lane = lax.broadcasted_iota(jnp.int32, kr.shape, 1)
    half = A_ROPE // 2
    swapped = jnp.where(lane < half, pltpu.roll(kr, LANES - half, axis=1), pltpu.roll(kr, half, axis=1))
    kpe_ref[...] = (kr * tabc_ref[...] + swapped * tabs_ref[...]).astype(BF16)


def a_norm(z, hn, w_in_t, tab_kc, tab_ks, q_norm, kv_norm, tm=512):
    t, d = hn.shape
    kr_block = (A_Q_RANK + A_KV_RANK) // LANES
    return pl.pallas_call(
        _a_norm_kernel,
        grid=(t // tm,),
        in_specs=[pl.BlockSpec((tm, A_Q_RANK), lambda i: (i, Z_CQ // A_Q_RANK)),
                  pl.BlockSpec((tm, A_KV_RANK), lambda i: (i, Z_CKV // A_KV_RANK)),
                  pl.BlockSpec((tm, d), lambda i: (i, 0)),
                  pl.BlockSpec((LANES, d), lambda i: (kr_block, 0)),
                  pl.BlockSpec((tm, LANES), lambda i: (i, 0)),
                  pl.BlockSpec((tm, LANES), lambda i: (i, 0)),
                  pl.BlockSpec((1, A_Q_RANK), lambda i: (0, 0)),
                  pl.BlockSpec((1, A_KV_RANK), lambda i: (0, 0))],
        out_specs=[pl.BlockSpec((tm, A_Q_RANK), lambda i: (i, 0)),
                   pl.BlockSpec((tm, A_KV_RANK), lambda i: (i, 0)),
                   pl.BlockSpec((tm, LANES), lambda i: (i, 0))],
        out_shape=[jax.ShapeDtypeStruct((t, A_Q_RANK), BF16),
                   jax.ShapeDtypeStruct((t, A_KV_RANK), BF16),
                   jax.ShapeDtypeStruct((t, LANES), BF16)],
        compiler_params=_params(("arbitrary",)),
        name="a_norm",
    )(z, z, hn, w_in_t, tab_kc, tab_ks, q_norm.reshape(1, -1), kv_norm.reshape(1, -1))


def _q_proj_kernel(x_ref, w_ref, tab_ref, o_ref, *, scale):
    x = x_ref[...]
    tab = tab_ref[...]
    for h in range(w_ref.shape[0]):
        acc = _dot(x, w_ref[h])
        nope = acc[:, :A_NOPE] * scale
        pe = _rope64(acc[:, A_NOPE:], tab) * scale
        o_ref[0, h] = jnp.concatenate([nope, pe], axis=1).astype(o_ref.dtype)


def q_proj(cqn, wq_r, tab64, batch, seq, tm=1024, hp=4):
    nsb = seq // tm
    scale = float((A_NOPE + A_ROPE) ** -0.5 * LOG2_E)
    return pl.pallas_call(
        functools.partial(_q_proj_kernel, scale=scale),
        grid=(batch * nsb, A_HEADS // hp),
        in_specs=[pl.BlockSpec((tm, A_Q_RANK), lambda i, h: (i, 0)),
                  pl.BlockSpec((hp, A_Q_RANK, 2 * LANES), lambda i, h: (h, 0, 0)),
                  pl.BlockSpec((tm, LANES), lambda i, h: (i, 0))],
        out_specs=pl.BlockSpec((1, hp, tm, 2 * LANES), lambda i, h: (i // nsb, h, i % nsb, 0)),
        out_shape=jax.ShapeDtypeStruct((batch, A_HEADS, seq, 2 * LANES), BF16),
        compiler_params=_params(("arbitrary", "arbitrary")),
        name="q_proj",
    )(cqn, wq_r, tab64)


def _kv_proj_kernel(x_ref, w_ref, kpe_ref, k_ref, v_ref):
    acc = _dot(x_ref[...], _bf(w_ref[...]))
    kpe = kpe_ref[...]
    hw = A_NOPE + A_V
    for h in range(k_ref.shape[1]):
        k_ref[0, h] = jnp.concatenate([acc[:, h * hw:h * hw + A_NOPE].astype(BF16), kpe], axis=1)
        v_ref[0, h] = acc[:, h * hw + A_NOPE:(h + 1) * hw].astype(BF16)


def kv_proj(ckvn, w_ukv, kpe, batch, seq, tm=1024, hp=4):
    nsb = seq // tm
    hw = A_NOPE + A_V
    return pl.pallas_call(
        _kv_proj_kernel,
        grid=(batch * nsb, A_HEADS // hp),
        in_specs=[pl.BlockSpec((tm, A_KV_RANK), lambda i, h: (i, 0)),
                  pl.BlockSpec((A_KV_RANK, hp * hw), lambda i, h: (0, h)),
                  pl.BlockSpec((tm, LANES), lambda i, h: (i, 0))],
        out_specs=[pl.BlockSpec((1, hp, tm, 2 * LANES), lambda i, h: (i // nsb, h, i % nsb, 0)),
                   pl.BlockSpec((1, hp, tm, A_V), lambda i, h: (i // nsb, h, i % nsb, 0))],
        out_shape=[jax.ShapeDtypeStruct((batch, A_HEADS, seq, 2 * LANES), BF16),
                   jax.ShapeDtypeStruct((batch, A_HEADS, seq, A_V), BF16)],
        compiler_params=_params(("arbitrary", "arbitrary")),
        name="kv_proj",
    )(ckvn, w_ukv, kpe)


def _flash_kernel(q_ref, k_ref, v_ref, o_ref, *, tq, tk, hp):
    qi = pl.program_id(2)
    dv = v_ref.shape[-1]

    def step_one(h, off, width, carry, masked):
        m, l, acc = carry
        s = _dot_nt(q_ref[0, h], k_ref[0, h, pl.ds(off, width), :])
        if masked:
            row = lax.broadcasted_iota(jnp.int32, s.shape, 0)
            col = lax.broadcasted_iota(jnp.int32, s.shape, 1)
            s = jnp.where(col <= row, s, NEG_BIG)
        m_new = jnp.maximum(m, jnp.max(s, axis=-1, keepdims=True))
        alpha = jnp.exp2(m - m_new)
        p = jnp.exp2(s - m_new)
        l = alpha * l + jnp.sum(p, axis=-1, keepdims=True)
        acc = alpha * acc + _dot(p.astype(BF16), v_ref[0, h, pl.ds(off, width), :])
        return m_new, l, acc

    def step(off, width, carry, masked):
        return tuple(step_one(h, off, width, carry[h], masked) for h in range(hp))

    def body(j, carry):
        return step(pl.multiple_of(j * tk, tk), tk, carry, False)

    init = (jnp.full((tq, 1), NEG_BIG, F32), jnp.zeros((tq, 1), F32), jnp.zeros((tq, dv), F32))
    carry = (init,) * hp
    n_wide = (qi * tq) // tk
    carry = lax.fori_loop(0, n_wide, body, carry)
    if tk > tq:
        carry = lax.cond(qi * tq > n_wide * tk,
                         lambda c: step(pl.multiple_of(n_wide * tk, tq), tq, c, False),
                         lambda c: c, carry)
    carry = step(pl.multiple_of(qi * tq, tq), tq, carry, True)
    o_ref[...] = jnp.concatenate([acc / l for _, l, acc in carry], axis=1).astype(o_ref.dtype)


def causal_attention(q, k, v, tq=1024, tk=1024, hp=2):
    b, h, s, dk = q.shape
    dv = v.shape[-1]
    nq = s // tq
    assert tk in (tq, 2 * tq) and s % tk == 0 and h % hp == 0
    return pl.pallas_call(
        functools.partial(_flash_kernel, tq=tq, tk=tk, hp=hp),
        grid=(b, h // hp, nq),
        in_specs=[pl.BlockSpec((1, hp, tq, dk), lambda bi, hi, qi: (bi, hi, qi, 0)),
                  pl.BlockSpec((1, hp, s, dk), lambda bi, hi, qi: (bi, hi, 0, 0)),
                  pl.BlockSpec((1, hp, s, dv), lambda bi, hi, qi: (bi, hi, 0, 0))],
        out_specs=pl.BlockSpec((tq, hp * dv), lambda bi, hi, qi: (bi * nq + qi, hi)),
        out_shape=jax.ShapeDtypeStruct((b * s, h * dv), BF16),
        compiler_params=_params(("arbitrary", "arbitrary", "arbitrary")),
        name="causal_attention",
    )(q, k, v)


def _rope_half(x, cos, sin):
    half = x.shape[1] // 2
    x1, x2 = x[:, :half], x[:, half:]
    return jnp.concatenate([x1 * cos - x2 * sin, x1 * sin + x2 * cos], axis=1)


def _retention_kernel(cdec_ref, q_ref, k_ref, v_ref, g_ref, cos_ref, sin_ref, intra_ref, qdec_ref,
                      kdec_ref, rn_ref, o_ref, state_sc, *, n_chunks):
    hi = pl.program_id(1)

    @pl.when(pl.program_id(2) == 0)
    def _():
        state_sc[...] = jnp.zeros_like(state_sc)

    chunk_decay = cdec_ref[hi]
    intra = intra_ref[0]
    q_decay = qdec_ref[0]
    k_decay = kdec_ref[0]
    rn = rn_ref[...]
    k_scale = float(B_QK ** -0.5)
    for c in range(n_chunks):
        sl = slice(c * RET_CHUNK, (c + 1) * RET_CHUNK)
        cos = cos_ref[sl, :]
        sin = sin_ref[sl, :]
        qr = _rope_half(q_ref[sl, :].astype(F32), cos, sin)
        kr = _rope_half(k_ref[sl, :].astype(F32), cos, sin) * k_scale
        v = v_ref[sl, :]
        qb = qr.astype(BF16)
        s = _dot_nt(qb, kr.astype(BF16)) * intra
        inner = _dot(s.astype(BF16), v)
        state = state_sc[...]
        cross = _dot(qb, state.astype(BF16)) * q_decay
        state_sc[...] = state * chunk_decay + _dot_tn((kr * k_decay).astype(BF16), v)
        y = inner + cross
        mu = jnp.mean(y, axis=-1, keepdims=True)
        yc = y - mu
        var = jnp.mean(yc * yc, axis=-1, keepdims=True)
        yn = yc * lax.rsqrt(var + EPS) * rn
        o_ref[sl, :] = (yn * jax.nn.silu(g_ref[sl, :].astype(F32))).astype(o_ref.dtype)


def retention(z, cos, sin, ret_norm, batch, seq, rows=1024):
    t = z.shape[0]
    nr = seq // rows
    hh = jnp.arange(B_HEADS, dtype=F32)
    log_g = jnp.log1p(-jnp.exp2(-5.0 - hh))
    idx = jnp.arange(RET_CHUNK, dtype=F32)
    diff = idx[:, None] - idx[None, :]
    intra = jnp.where(diff >= 0, jnp.exp(log_g[:, None, None] * jnp.maximum(diff, 0.0)), 0.0)
    ones = jnp.ones((1, 1, B_QK), F32)
    q_decay = jnp.exp(log_g[:, None] * (idx + 1.0))[:, :, None] * ones
    k_decay = jnp.exp(log_g[:, None] * (RET_CHUNK - 1.0 - idx))[:, :, None] * ones
    chunk_decay = jnp.exp(log_g * RET_CHUNK)

    def zspec(off):
        return pl.BlockSpec((rows, B_QK), lambda b, h, r, cd: (b * nr + r, off // B_QK + h))

    tab = pl.BlockSpec((rows, B_QK // 2), lambda b, h, r, cd: (b * nr + r, 0))
    per_head = pl.BlockSpec((1, RET_CHUNK, B_QK), lambda b, h, r, cd: (h, 0, 0))
    grid_spec = pltpu.PrefetchScalarGridSpec(
        num_scalar_prefetch=1,
        grid=(batch, B_HEADS, nr),
        in_specs=[zspec(Z_RQ), zspec(Z_RK), zspec(Z_RV), zspec(Z_RG), tab, tab,
                  pl.BlockSpec((1, RET_CHUNK, RET_CHUNK), lambda b, h, r, cd: (h, 0, 0)),
                  per_head, per_head,
                  pl.BlockSpec((1, B_V), lambda b, h, r, cd: (0, h))],
        out_specs=pl.BlockSpec((rows, B_V), lambda b, h, r, cd: (b * nr + r, h)),
        scratch_shapes=[pltpu.VMEM((B_QK, B_V), F32)],
    )
    return pl.pallas_call(
        functools.partial(_retention_kernel, n_chunks=rows // RET_CHUNK),
        grid_spec=grid_spec,
        out_shape=jax.ShapeDtypeStruct((t, B_HEADS * B_V), BF16),
        compiler_params=_params(("arbitrary", "arbitrary", "arbitrary")),
        name="retention",
    )(chunk_decay, z, z, z, z, cos, sin, intra, q_decay, k_decay, ret_norm.reshape(1, -1))


def _sgu_kernel(u_ref, v_ref, vn_ref, ws_ref, bs_ref, o_ref, *, n_chunks):
    gw = C_WIDTH // C_GROUPS
    v = v_ref[...].astype(F32)
    mu = jnp.mean(v, axis=-1, keepdims=True)
    vc = v - mu
    var = jnp.mean(vc * vc, axis=-1, keepdims=True)
    vn = (vc * lax.rsqrt(var + EPS) * vn_ref[...]).astype(BF16)
    r = lax.broadcasted_iota(jnp.int32, (C_CHUNK, C_CHUNK), 0)
    c = lax.broadcasted_iota(jnp.int32, (C_CHUNK, C_CHUNK), 1)
    bs = bs_ref[...]
    for g in range(C_GROUPS):
        ws = jnp.where(c <= r, ws_ref[g], 0.0).astype(BF16)
        bias = bs[:, g:g + 1]
        cols = slice(g * gw, (g + 1) * gw)
        for ch in range(n_chunks):
            rows = slice(ch * C_CHUNK, (ch + 1) * C_CHUNK)
            mixed = _dot(ws, vn[rows, cols]) + bias
            o_ref[rows, cols] = (u_ref[rows, cols].astype(F32) * mixed).astype(o_ref.dtype)


def spatial_gate(z, v_norm, w_s, b_s, rows=512):
    t = z.shape[0]
    return pl.pallas_call(
        functools.partial(_sgu_kernel, n_chunks=rows // C_CHUNK),
        grid=(t // rows,),
        in_specs=[pl.BlockSpec((rows, C_WIDTH), lambda i: (i, 0)),
                  pl.BlockSpec((rows, C_WIDTH), lambda i: (i, 1)),
                  pl.BlockSpec((1, C_WIDTH), lambda i: (0, 0)),
                  pl.BlockSpec((C_GROUPS, C_CHUNK, C_CHUNK), lambda i: (0, 0, 0)),
                  pl.BlockSpec((C_CHUNK, C_GROUPS), lambda i: (0, 0))],
        out_specs=pl.BlockSpec((rows, C_WIDTH), lambda i: (i, 0)),
        out_shape=jax.ShapeDtypeStruct((t, C_WIDTH), BF16),
        compiler_params=_params(("arbitrary",)),
        name="spatial_gate",
    )(z, z, v_norm.reshape(1, -1), w_s, b_s.T)


def _pack_bf16_pair(lo, hi):
    lo_bits = lax.bitcast_convert_type(lo.astype(BF16).astype(F32), jnp.uint32)
    hi_bits = lax.bitcast_convert_type(hi.astype(BF16).astype(F32), jnp.uint32)
    return (lo_bits >> 16) | (hi_bits & jnp.uint32(0xFFFF0000))


def _unpack_bf16_pair(words):
    lo = lax.bitcast_convert_type(words << 16, F32).astype(BF16)
    hi = lax.bitcast_convert_type(words & jnp.uint32(0xFFFF0000), F32).astype(BF16)
    return lo, hi


def _xattn_kernel(hmix_ref, gmix_ref, x_ref, gpre_ref, wq_ref, k_ref, v_ref, wo_ref, gpost_ref, gnext_ref,
                  xo_ref, hn_ref, *, pack_next):
    x = x_ref[...] + _rms(hmix_ref[...], gmix_ref[...])
    hn = _rms(x, gpre_ref[...]).astype(BF16)
    q = (_dot(hn, wq_ref[...]) * float(X_HEAD_DIM ** -0.5)).astype(BF16)
    k = k_ref[...]
    v = v_ref[...]
    outs = []
    for h in range(X_HEADS):
        cols = slice(h * X_HEAD_DIM, (h + 1) * X_HEAD_DIM)
        s = _dot_nt(q[:, cols], k[:, cols])
        p = jnp.exp(s - jnp.max(s, axis=-1, keepdims=True))
        o = _dot(p.astype(BF16), v[:, cols]) / jnp.sum(p, axis=-1, keepdims=True)
        outs.append(o.astype(BF16))
    hh = _dot(jnp.concatenate(outs, axis=1), wo_ref[...])
    xn = x + _rms(hh, gpost_ref[...])
    xo_ref[...] = xn
    hn = _rms(xn, gnext_ref[...])
    if pack_next:
        half = hn.shape[1] // 2
        hn_ref[...] = _pack_bf16_pair(hn[:, :half], hn[:, half:])
    else:
        hn_ref[...] = hn.astype(hn_ref.dtype)


def cross_attention(h_mix, g_mix, x, kv, wq, wo, g_pre, g_post, g_next, seq, pack_next=False, tm=256):
    t, d = x.shape
    hd = X_HEADS * X_HEAD_DIM
    nsb = seq // tm
    row = pl.BlockSpec((tm, d), lambda i: (i, 0))
    vec = pl.BlockSpec((1, d), lambda i: (0, 0))
    if pack_next:
        hn_spec = pl.BlockSpec((tm, d // 2), lambda i: (i, 0))
        hn_shape = jax.ShapeDtypeStruct((t, d // 2), jnp.uint32)
    else:
        hn_spec = row
        hn_shape = jax.ShapeDtypeStruct((t, d), BF16)
    return pl.pallas_call(
        functools.partial(_xattn_kernel, pack_next=pack_next),
        grid=(t // tm,),
        in_specs=[row, vec, row, vec,
                  pl.BlockSpec((d, hd), lambda i: (0, 0)),
                  pl.BlockSpec((N_MEM, hd), lambda i: (i // nsb, 0)),
                  pl.BlockSpec((N_MEM, hd), lambda i: (i // nsb, 1)),
                  pl.BlockSpec((hd, d), lambda i: (0, 0)),
                  vec, vec],
        out_specs=[row, hn_spec],
        out_shape=[jax.ShapeDtypeStruct((t, d), F32), hn_shape],
        compiler_params=_params(("arbitrary",)),
        name="cross_attention",
    )(h_mix, g_mix.reshape(1, d), x, g_pre.reshape(1, d), wq, kv, kv, wo, g_post.reshape(1, d),
      g_next.reshape(1, d))


def _router_kernel(x_ref, g_ref, wt_ref, idx_ref, gate_ref):
    hn = _rms(x_ref[...], g_ref[...])
    wt = wt_ref[...]
    h_hi = hn.astype(BF16)
    h_lo = (hn - h_hi.astype(F32)).astype(BF16)
    w_hi = wt.astype(BF16)
    w_lo = (wt - w_hi.astype(F32)).astype(BF16)
    logits = _dot_nt(w_hi, h_hi) + (_dot_nt(w_hi, h_lo) + _dot_nt(w_lo, h_hi))
    e = lax.broadcasted_iota(jnp.int32, logits.shape, 0)
    m1 = jnp.max(logits, axis=0, keepdims=True)
    i1 = jnp.min(jnp.where(logits == m1, e, N_EXPERTS), axis=0, keepdims=True)
    rest = jnp.where(e == i1, -jnp.inf, logits)
    m2 = jnp.max(rest, axis=0, keepdims=True)
    i2 = jnp.min(jnp.where(rest == m2, e, N_EXPERTS), axis=0, keepdims=True)
    ex = jnp.exp(m2 - m1)
    g1 = 1.0 / (1.0 + ex)
    idx_ref[...] = jnp.concatenate([i1, i2], axis=0)
    gate_ref[...] = jnp.concatenate([g1, ex * g1], axis=0)


def router(x, g, w_router, tm=512):
    t, d = x.shape
    return pl.pallas_call(
        _router_kernel,
        grid=(t // tm,),
        in_specs=[pl.BlockSpec((tm, d), lambda i: (i, 0)),
                  pl.BlockSpec((1, d), lambda i: (0, 0)),
                  pl.BlockSpec((N_EXPERTS, d), lambda i: (0, 0))],
        out_specs=[pl.BlockSpec((2, tm), lambda i: (0, i)),
                   pl.BlockSpec((2, tm), lambda i: (0, i))],
        out_shape=[jax.ShapeDtypeStruct((2, t), jnp.int32), jax.ShapeDtypeStruct((2, t), F32)],
        compiler_params=_params(("arbitrary",)),
        name="router",
    )(x, g.reshape(1, d), w_router.T)


def _live_col(tile_sub, t, j, n_cols):
    return jnp.where(tile_sub[t] > 0, j, n_cols - 1)


def _moe_gu_kernel(te_ref, tx_ref, tq_ref, x_ref, wg_ref, wu_ref, o_ref):
    n_sub = tq_ref[pl.program_id(0)]
    tm = o_ref.shape[0]
    half = wg_ref.shape[1] // 2

    for q in range(1, tm // MOE_SUB + 1):
        rows = q * MOE_SUB

        @pl.when(n_sub == q)
        def _(rows=rows):
            x_lo, x_hi = _unpack_bf16_pair(x_ref[:rows, :])
            g = _dot(x_lo, _bf(wg_ref[0, :half, :])) + _dot(x_hi, _bf(wg_ref[0, half:, :]))
            u = _dot(x_lo, _bf(wu_ref[0, :half, :])) + _dot(x_hi, _bf(wu_ref[0, half:, :]))
            o_ref[:rows, :] = (jax.nn.silu(g) * u).astype(o_ref.dtype)
            if rows < tm:
                o_ref[rows:, :] = jnp.zeros((tm - rows, o_ref.shape[1]), o_ref.dtype)

    @pl.when(n_sub == 0)
    def _():
        o_ref[...] = jnp.zeros_like(o_ref)


def moe_gate_up(xs, w_gu, tile_e, tile_x, tile_sub, tn=256):
    p = xs.shape[0]
    d = w_gu.shape[1]
    f = w_gu.shape[2] // 2
    nf = f // tn
    nt = p // MOE_TM
    grid_spec = pltpu.PrefetchScalarGridSpec(
        num_scalar_prefetch=3,
        grid=(nt, nf),
        in_specs=[pl.BlockSpec((MOE_TM, d // 2), lambda t, j, te, tx, ts: (tx[t], 0)),
                  pl.BlockSpec((1, d, tn), lambda t, j, te, tx, ts: (te[t], 0, _live_col(ts, t, j, nf))),
                  pl.BlockSpec((1, d, tn), lambda t, j, te, tx, ts: (te[t], 0, _live_col(ts, t, j, nf) + nf))],
        out_specs=pl.BlockSpec((MOE_TM, tn), lambda t, j, te, tx, ts: (t, j)),
    )
    return pl.pallas_call(
        _moe_gu_kernel,
        grid_spec=grid_spec,
        out_shape=jax.ShapeDtypeStruct((p, f), BF16),
        compiler_params=_params(("arbitrary", "arbitrary")),
        name="moe_gate_up",
    )(tile_e, tile_x, tile_sub, xs, w_gu, w_gu)


def _moe_down_kernel(te_ref, tx_ref, tq_ref, a_ref, w_ref, o_ref):
    n_sub = tq_ref[pl.program_id(0)]
    tm = o_ref.shape[0]

    for q in range(1, tm // MOE_SUB + 1):
        rows = q * MOE_SUB

        @pl.when(n_sub == q)
        def _(rows=rows):
            y = _dot(a_ref[:rows, :], _bf(w_ref[0]))
            half = y.shape[1] // 2
            o_ref[:rows, :] = _pack_bf16_pair(y[:, :half], y[:, half:])
            if rows < tm:
                o_ref[rows:, :] = jnp.zeros((tm - rows, o_ref.shape[1]), o_ref.dtype)

    @pl.when(n_sub == 0)
    def _():
        o_ref[...] = jnp.zeros_like(o_ref)


MOE_DOWN_TN = 512


def moe_down(act, w_down, tile_e, tile_x, tile_sub, tn=MOE_DOWN_TN):
    p, f = act.shape
    d = w_down.shape[2]
    nt = p // MOE_TM
    grid_spec = pltpu.PrefetchScalarGridSpec(
        num_scalar_prefetch=3,
        grid=(nt, d // tn),
        in_specs=[pl.BlockSpec((MOE_TM, f), lambda t, j, te, tx, ts: (tx[t], 0)),
                  pl.BlockSpec((1, f, tn), lambda t, j, te, tx, ts: (te[t], 0, _live_col(ts, t, j, d // tn)))],
        out_specs=pl.BlockSpec((MOE_TM, tn // 2), lambda t, j, te, tx, ts: (t, j)),
    )
    return pl.pallas_call(
        _moe_down_kernel,
        grid_spec=grid_spec,
        out_shape=jax.ShapeDtypeStruct((p, d // 2), jnp.uint32),
        compiler_params=_params(("arbitrary", "arbitrary")),
        name="moe_down",
    )(tile_e, tile_x, tile_sub, act, w_down)


def moe_layer(x, g_pre, hn_packed, w_router, w_gu, w_down):
    t = x.shape[0]
    idx, gates = router(x, g_pre, w_router)
    experts = jnp.arange(N_EXPERTS, dtype=jnp.int32)
    e_flat = idx.reshape(-1)
    csum = jnp.cumsum((e_flat[:, None] == experts[None, :]).astype(jnp.int32), axis=0)
    rank = jnp.take_along_axis(csum, e_flat[:, None], axis=1)[:, 0] - 1
    count = csum[-1]
    n_tiles = (count + MOE_TM - 1) // MOE_TM
    tile_end = jnp.cumsum(n_tiles)
    tile_start = tile_end - n_tiles
    pos = tile_start[e_flat] * MOE_TM + rank
    nt = (2 * t + N_EXPERTS * (MOE_TM - 1)) // MOE_TM
    p = nt * MOE_TM
    row_tok = (jnp.arange(p, dtype=jnp.int32) % t).at[pos].set(jnp.tile(jnp.arange(t, dtype=jnp.int32), 2))
    n_used = tile_end[-1]
    tile_id = jnp.arange(nt, dtype=jnp.int32)
    tile_x = jnp.minimum(tile_id, n_used - 1)
    tile_e = jnp.sum((tile_end[None, :] <= tile_x[:, None]).astype(jnp.int32), axis=1)
    tile_e = jnp.minimum(tile_e, N_EXPERTS - 1)
    rows_left = count[tile_e] - (tile_x - tile_start[tile_e]) * MOE_TM
    tile_sub = (jnp.clip(rows_left, 0, MOE_TM) + MOE_SUB - 1) // MOE_SUB
    tile_sub = jnp.where(tile_id < n_used, tile_sub, 0).astype(jnp.int32)
    xs = hn_packed.at[row_tok].get(mode="promise_in_bounds")
    act = moe_gate_up(xs, w_gu, tile_e, tile_x, tile_sub)
    y = moe_down(act, w_down, tile_e, tile_x, tile_sub)
    y0 = y.at[pos[:t]].get(mode="promise_in_bounds")
    y1 = y.at[pos[t:]].get(mode="promise_in_bounds")
    return y0, y1, gates.T


def _rope_tables(positions):
    pos = positions.reshape(-1).astype(F32)
    inv_a = ROPE_BASE ** (-jnp.arange(0, A_ROPE, 2, dtype=F32) / A_ROPE)
    ang_a = pos[:, None] * inv_a
    cos_a, sin_a = jnp.cos(ang_a), jnp.sin(ang_a)
    zero = jnp.zeros_like(cos_a)
    tab64 = jnp.concatenate([cos_a, cos_a, -sin_a, sin_a], axis=1)
    tab_kc = jnp.concatenate([cos_a, cos_a, zero, zero], axis=1)
    tab_ks = jnp.concatenate([-sin_a, sin_a, zero, zero], axis=1)
    inv_b = ROPE_BASE ** (-jnp.arange(0, B_QK, 2, dtype=F32) / B_QK)
    ang_b = pos[:, None] * inv_b
    return tab64, tab_kc, tab_ks, jnp.cos(ang_b), jnp.sin(ang_b)


def _swap_halves(w):
    half = w.shape[-1] // 2
    return jnp.concatenate([w[..., half:], w[..., :half]], axis=-1)


def kernel(x, mem, positions, norm_mix_pre, norm_mix_post, norm_x_pre, norm_x_post, norm_ffn_pre, norm_ffn_post, norm_mem, x_w_q, x_w_kv, x_w_o, ab_w_in, ab_q_norm, ab_w_uq, ab_kv_norm, ab_w_ukv, ab_ret_norm, ab_w_out, ffn_w_gu, ffn_w_down, c_w_in, c_v_norm, c_w_s, c_b_s, c_w_out, moe_router, moe_w_gu, moe_w_down):
    batch, seq, d = x.shape
    t = batch * seq
    x = x.reshape(t, d)
    mem2 = mem.reshape(batch * N_MEM, d)
    tab64, tab_kc, tab_ks, cos_b, sin_b = _rope_tables(positions)

    w_uq = ab_w_uq[0].reshape(A_Q_RANK, A_HEADS, A_NOPE + A_ROPE)
    w_pe = w_uq[:, :, A_NOPE:]
    wq_r = jnp.concatenate([w_uq[:, :, :A_NOPE], w_pe, _swap_halves(w_pe)], axis=-1)
    wq_r = wq_r.transpose(1, 0, 2).astype(BF16)

    hn = rms_norm_bf16(x, norm_mix_pre[0])
    w_in_t = ab_w_in[0].T
    z = ab_in_proj(hn, w_in_t)
    cqn, ckvn, kpe = a_norm(z, hn, w_in_t, tab_kc, tab_ks, ab_q_norm[0], ab_kv_norm[0])
    q = q_proj(cqn, wq_r, tab64, batch, seq)
    k, v = kv_proj(ckvn, ab_w_ukv[0], kpe, batch, seq)
    ya = causal_attention(q, k, v)
    yb = retention(z, cos_b, sin_b, ab_ret_norm[0], batch, seq)
    h = matmul_cat(ya, yb, ab_w_out[0], F32, tm=512, tn=1024, name="ab_out_proj")
    memn = rms_norm_bf16(mem2, norm_mem[0])
    kv = matmul(memn, x_w_kv[0], BF16, tm=512, tn=512, name="mem_kv_proj")
    x, hn = cross_attention(h, norm_mix_post[0], x, kv, x_w_q[0].astype(BF16), x_w_o[0].astype(BF16),
                            norm_x_pre[0], norm_x_post[0], norm_ffn_pre[0], seq)
    act = matmul_swiglu(hn, ffn_w_gu[0], tm=1024, tn=256)
    h = matmul_ksplit(act, ffn_w_down[0], tm=1024, tn=2048, tk=1024)
    x, hn = resid_norm([h], x, norm_ffn_post[0], norm_mix_pre[1])

    zc = matmul(hn, c_w_in[0], BF16, tm=1024, tn=512, act="gelu", name="c_in_proj")
    yc = spatial_gate(zc, c_v_norm[0], c_w_s[0], c_b_s[0])
    h = matmul(yc, c_w_out[0], F32, tm=512, tn=1024, name="c_out_proj")
    memn = rms_norm_bf16(mem2, norm_mem[1])
    kv = matmul(memn, x_w_kv[1], BF16, tm=512, tn=512, name="mem_kv_proj")
    x, hn_packed = cross_attention(h, norm_mix_post[1], x, kv, x_w_q[1].astype(BF16), x_w_o[1].astype(BF16),
                                   norm_x_pre[1], norm_x_post[1], norm_ffn_pre[1], seq, pack_next=True)
    y0, y1, gates = moe_layer(x, norm_ffn_pre[1], hn_packed, moe_router[0], moe_w_gu[0], moe_w_down[0])
    x = resid_norm([y0, y1], x, norm_ffn_post[1], weights=gates, packed_tile=MOE_DOWN_TN // 2)
    return x.reshape(batch, seq, d)
```
